```python
import math
import jax, jax.numpy as jnp
from jax import lax
import numpy as np

D_MODEL = 1024
BATCH = 32
SEQ = 2048
DEPTH = 2
DEC_BATCH = 16
DEC_SEQ = 64
PAST_LEN = 4096

CHUNK = 64
N_RS_LAYERS = (DEPTH + 1) // 2
N_ML_LAYERS = DEPTH // 2
RET_HEADS = 4
RET_DK = 128
RET_DV = 128
RET_QK = RET_HEADS * RET_DK
RET_WIDTH = RET_HEADS * RET_DV
S5_WIDTH = D_MODEL - RET_WIDTH
S5_GROUP = 16
S5_GROUPS = S5_WIDTH // S5_GROUP
S5_STATE = 64
RS_IN = 2 * RET_QK + 2 * RET_WIDTH + S5_WIDTH
ML_HEADS = 4
ML_INNER = 2 * D_MODEL
ML_DH = ML_INNER // ML_HEADS
ML_CONV = 4
MEM_LEN = 256
X_HEADS = 4
X_DH = D_MODEL // X_HEADS
D_FF = 4 * D_MODEL
ROPE_BASE = 10000.0
EPS = 1e-6

kernel_name = 'hybrid_streaming_encoder_step'


def rmsnorm(x, g):
    xf = x.astype(jnp.float32)
    y = xf * lax.rsqrt(jnp.mean(xf * xf, axis=-1, keepdims=True) + EPS)
    return (y * g.astype(jnp.float32)).astype(x.dtype)


def head_layernorm(x, g):
    xf = x.astype(jnp.float32)
    mu = jnp.mean(xf, axis=-1, keepdims=True)
    var = jnp.mean(jnp.square(xf - mu), axis=-1, keepdims=True)
    y = (xf - mu) * lax.rsqrt(var + EPS)
    return y.reshape(*x.shape[:-2], -1) * g.astype(jnp.float32)


def rope(x, pos):
    half = x.shape[-1] // 2
    inv = ROPE_BASE ** (-jnp.arange(half, dtype=jnp.float32) / half)
    ang = pos.astype(jnp.float32)[:, None] * inv[None, :]
    cos = jnp.cos(ang)[None, :, None, :]
    sin = jnp.sin(ang)[None, :, None, :]
    xf = x.astype(jnp.float32)
    x1, x2 = xf[..., :half], xf[..., half:]
    return jnp.concatenate([x1 * cos - x2 * sin, x1 * sin + x2 * cos], axis=-1)


def retention(q, k, v, s0):
    bsz, L, H, _ = q.shape
    c = min(CHUNK, L)
    n = L // c
    log_g = jnp.log1p(-jnp.exp2(-5.0 - jnp.arange(H, dtype=jnp.float32)))
    idx = jnp.arange(c, dtype=jnp.float32)
    diff = idx[:, None] - idx[None, :]
    decay = jnp.where(diff[None] >= 0, jnp.exp(jnp.maximum(diff, 0.0)[None] * log_g[:, None, None]), 0.0)
    qc = q.reshape(bsz, n, c, H, -1)
    kc = k.reshape(bsz, n, c, H, -1)
    vc = v.reshape(bsz, n, c, H, -1)
    scores = jnp.einsum('bnihd,bnjhd->bnhij', qc, kc) * decay[None, None]
    inner = jnp.einsum('bnhij,bnjhe->bnihe', scores, vc)
    w_kv = jnp.exp((c - 1.0 - idx)[:, None] * log_g[None, :])
    kv = jnp.einsum('bnjhd,bnjhe->nbhde', kc * w_kv[None, None, :, :, None], vc)
    g_c = jnp.exp(c * log_g)[None, :, None, None]

    def step(s, kv_n):
        return g_c * s + kv_n, s

    s_final, s_prev = lax.scan(step, s0, kv)
    w_q = jnp.exp((idx + 1.0)[:, None] * log_g[None, :])
    cross = jnp.einsum('bnihd,nbhde->bnihe', qc, s_prev) * w_q[None, None, :, :, None]
    return (inner + cross).reshape(bsz, L, H, -1), s_final


def _complex_affine_combine(e1, e2):
    ar1, ai1, br1, bi1 = e1
    ar2, ai2, br2, bi2 = e2
    return (ar2 * ar1 - ai2 * ai1, ar2 * ai1 + ai2 * ar1,
            ar2 * br1 - ai2 * bi1 + br2, ar2 * bi1 + ai2 * br1 + bi2)


def s5_layer(u, h0_re, h0_im, a_re, a_im, log_dt, b_re, b_im, c_re, c_im, d_skip, w_glu, b_glu):
    to32 = lambda t: t.astype(jnp.float32)
    a_re, a_im, b_re, b_im, c_re, c_im = map(to32, (a_re, a_im, b_re, b_im, c_re, c_im))
    bsz, L, _ = u.shape
    dt = jnp.exp(to32(log_dt))[:, None]
    mag = jnp.exp(a_re * dt)
    ab_re = mag * jnp.cos(a_im * dt)
    ab_im = mag * jnp.sin(a_im * dt)
    den = a_re * a_re + a_im * a_im
    x_re = ab_re - 1.0
    f_re = (x_re * a_re + ab_im * a_im) / den
    f_im = (ab_im * a_re - x_re * a_im) / den
    bb_re = f_re[..., None] * b_re - f_im[..., None] * b_im
    bb_im = f_re[..., None] * b_im + f_im[..., None] * b_re
    ug = u.reshape(bsz, L, S5_GROUPS, S5_GROUP)
    bu_re = jnp.einsum('blgc,gpc->blgp', ug, bb_re)
    bu_im = jnp.einsum('blgc,gpc->blgp', ug, bb_im)
    h0_re = to32(h0_re)
    h0_im = to32(h0_im)
    bu_re = bu_re.at[:, 0].add(ab_re * h0_re - ab_im * h0_im)
    bu_im = bu_im.at[:, 0].add(ab_re * h0_im + ab_im * h0_re)
    shape = (1, L, S5_GROUPS, S5_STATE)
    _, _, h_re, h_im = lax.associative_scan(
        _complex_affine_combine,
        (jnp.broadcast_to(ab_re, shape), jnp.broadcast_to(ab_im, shape), bu_re, bu_im), axis=1)
    y = jnp.einsum('blgp,gcp->blgc', h_re, c_re) - jnp.einsum('blgp,gcp->blgc', h_im, c_im)
    y = y.reshape(bsz, L, S5_WIDTH) + to32(d_skip) * u
    y = jax.nn.gelu(y)
    y = y * jax.nn.sigmoid(y @ to32(w_glu) + to32(b_glu))
    return y, h_re[:, -1], h_im[:, -1]


def ret_s5_block(xn, pos, s_ret, h_re, h_im, w_in, w_out, ret_gn, a_re, a_im, log_dt,
                 b_re, b_im, c_re, c_im, d_skip, w_glu, b_glu):
    bsz, L, _ = xn.shape
    proj = xn @ w_in
    q, k, v, g, u = jnp.split(proj, [RET_QK, 2 * RET_QK, 2 * RET_QK + RET_WIDTH,
                                     2 * RET_QK + 2 * RET_WIDTH], axis=-1)
    q = rope(q.reshape(bsz, L, RET_HEADS, RET_DK), pos)
    k = rope(k.reshape(bsz, L, RET_HEADS, RET_DK), pos) * (RET_DK ** -0.5)
    v = v.reshape(bsz, L, RET_HEADS, RET_DV).astype(jnp.float32)
    o_ret, s_new = retention(q, k, v, s_ret.astype(jnp.float32))
    y_ret = jax.nn.silu(g.astype(jnp.float32)) * head_layernorm(o_ret, ret_gn)
    y_s5, hr, hi = s5_layer(u.astype(jnp.float32), h_re, h_im, a_re, a_im, log_dt,
                            b_re, b_im, c_re, c_im, d_skip, w_glu, b_glu)
    y = jnp.concatenate([y_ret, y_s5], axis=-1).astype(xn.dtype) @ w_out
    return y, s_new, hr, hi


def mlstm_cell(q, k, v, ig, lf, c0, n0, m0):
    bsz, L, H, d = q.shape
    c = min(CHUNK, L)
    n = L // c
    causal = jnp.tril(jnp.ones((c, c), dtype=bool))

    def to_chunks(t):
        return jnp.swapaxes(t.reshape(bsz, n, c, *t.shape[2:]), 0, 1)

    def step(carry, xs):
        cm, nv, m = carry
        qc, kc, vc, igc, lfc = xs
        b = jnp.cumsum(lfc, axis=1).transpose(0, 2, 1)
        i_h = igc.transpose(0, 2, 1)
        dlog = b[..., :, None] - b[..., None, :] + i_h[..., None, :]
        dlog = jnp.where(causal, dlog, -jnp.inf)
        inter = b + m[..., None]
        m_t = jnp.maximum(inter, jnp.max(dlog, axis=-1))
        w = jnp.exp(dlog - m_t[..., None])
        s_inter = jnp.exp(inter - m_t)
        qk = jnp.einsum('bthd,bshd->bhts', qc, kc) * w
        num = s_inter[..., None] * jnp.einsum('bthd,bhde->bhte', qc, cm) + jnp.einsum('bhts,bshe->bhte', qk, vc)
        den = s_inter * jnp.einsum('bthd,bhd->bht', qc, nv) + jnp.sum(qk, axis=-1)
        h = num / jnp.maximum(jnp.abs(den), jnp.exp(-m_t))[..., None]
        m_new = m_t[..., -1]
        w_s = jnp.exp(b[..., -1:] - b + i_h - m_new[..., None])
        dec = jnp.exp(b[..., -1] + m - m_new)
        kw = kc * w_s.transpose(0, 2, 1)[..., None]
        c_new = dec[..., None, None] * cm + jnp.einsum('bshd,bshe->bhde', kw, vc)
        n_new = dec[..., None] * nv + jnp.sum(kw, axis=1)
        return (c_new, n_new, m_new), h.transpose(0, 2, 1, 3)

    (cf, nf, mf), hs = lax.scan(step, (c0, n0, m0),
                                (to_chunks(q), to_chunks(k), to_chunks(v), to_chunks(ig), to_chunks(lf)))
    return jnp.swapaxes(hs, 0, 1).reshape(bsz, L, H, d), cf, nf, mf


def mlstm_block(xn, conv_state, c0, n0, m0, w_in, conv_w, conv_b, wq, wk, wv,
                w_gates, b_gates, gn, skip, w_out):
    bsz, L, _ = xn.shape
    proj = xn @ w_in
    xm, z = jnp.split(proj, 2, axis=-1)
    xp = jnp.concatenate([conv_state.astype(xm.dtype), xm], axis=1)
    xc = conv_b + sum(xp[:, j:j + L] * conv_w[j] for j in range(ML_CONV))
    xc = jax.nn.silu(xc)
    new_conv = xp[:, L:]
    xch = xc.reshape(bsz, L, ML_HEADS, ML_DH)
    xmh = xm.reshape(bsz, L, ML_HEADS, ML_DH)
    q = jnp.einsum('blhd,hde->blhe', xch, wq)
    k = jnp.einsum('blhd,hde->blhe', xch, wk)
    v = jnp.einsum('blhd,hde->blhe', xmh, wv)
    qkv = jnp.concatenate([q.reshape(bsz, L, -1), k.reshape(bsz, L, -1), v.reshape(bsz, L, -1)], axis=-1)
    gates = (qkv @ w_gates + b_gates).astype(jnp.float32)
    ig, fg = jnp.split(gates, 2, axis=-1)
    lf = jax.nn.log_sigmoid(fg)
    f32 = jnp.float32
    h, cf, nf, mf = mlstm_cell(q.astype(f32), k.astype(f32) * (ML_DH ** -0.5), v.astype(f32), ig, lf,
                               c0.astype(f32), n0.astype(f32), m0.astype(f32))
    o = jax.nn.sigmoid(z.astype(f32)).reshape(bsz, L, ML_HEADS, ML_DH)
    y = head_layernorm(o * h, gn) + skip.astype(f32) * xc.astype(f32)
    return y.astype(xn.dtype) @ w_out, cf, nf, mf, new_conv


def memory_kv(mem, g, wk, wv):
    mn = rmsnorm(mem, g)
    bsz, m_len, _ = mem.shape
    return ((mn @ wk).reshape(bsz, m_len, X_HEADS, X_DH),
            (mn @ wv).reshape(bsz, m_len, X_HEADS, X_DH))


def cross_attend(xn, k, v, wq, wo):
    bsz, L, _ = xn.shape
    q = (xn @ wq).reshape(bsz, L, X_HEADS, X_DH)
    s = jnp.einsum('blhd,bmhd->bhlm', q, k).astype(jnp.float32) * (X_DH ** -0.5)
    p = jax.nn.softmax(s, axis=-1).astype(v.dtype)
    o = jnp.einsum('bhlm,bmhd->blhd', p, v).reshape(bsz, L, D_MODEL)
    return o @ wo


def squared_relu_mlp(xn, w_up, w_down):
    h = jax.nn.relu(xn @ w_up)
    return (h * h) @ w_down


def trunk(x, pos0, st_ret, st_re, st_im, st_c, st_n, st_m, st_conv, mem_k, mem_v, p):
    L = x.shape[1]
    pos = pos0 + jnp.arange(L, dtype=jnp.int32)
    o_ret, o_re, o_im, o_c, o_n, o_m, o_conv = [], [], [], [], [], [], []
    for l in range(DEPTH):
        j = l // 2
        xn = rmsnorm(x, p['norm_mix'][l])
        if l % 2 == 0:
            y, s_new, hr, hi = ret_s5_block(
                xn, pos, st_ret[j], st_re[j], st_im[j], p['rs_w_in'][j], p['rs_w_out'][j], p['ret_gn'][j],
                p['s5_a_re'][j], p['s5_a_im'][j], p['s5_log_dt'][j], p['s5_b_re'][j], p['s5_b_im'][j],
                p['s5_c_re'][j], p['s5_c_im'][j], p['s5_d'][j], p['s5_w_glu'][j], p['s5_b_glu'][j])
            o_ret.append(s_new)
            o_re.append(hr)
            o_im.append(hi)
        else:
            y, cf, nf, mf, cv = mlstm_block(
                xn, st_conv[j], st_c[j], st_n[j], st_m[j], p['ml_w_in'][j], p['ml_conv_w'][j], p['ml_conv_b'][j],
                p['ml_wq'][j], p['ml_wk'][j], p['ml_wv'][j], p['ml_w_gates'][j], p['ml_b_gates'][j],
                p['ml_gn'][j], p['ml_skip'][j], p['ml_w_out'][j])
            o_c.append(cf)
            o_n.append(nf)
            o_m.append(mf)
            o_conv.append(cv)
        x = x + y.astype(x.dtype)
        x = x + cross_attend(rmsnorm(x, p['norm_cross'][l]), mem_k[l], mem_v[l], p['x_wq'][l], p['x_wo'][l])
        x = x + squared_relu_mlp(rmsnorm(x, p['norm_mlp'][l]), p['mlp_w_up'][l], p['mlp_w_down'][l])
    y = rmsnorm(x, p['norm_final'])
    stk = lambda t: jnp.stack(t).astype(x.dtype)
    return (y, stk(o_ret), stk(o_re), stk(o_im), stk(o_c), stk(o_n), stk(o_m), stk(o_conv))


def setup_inputs(seed: int = 0) -> dict:
    key = jax.random.key(seed)
    keys = iter(jax.random.split(key, 64))
    f32 = jnp.float32

    def nrm(shape, scale=1.0):
        return scale * jax.random.normal(next(keys), shape, f32)

    def gain(shape):
        return 1.0 + nrm(shape, 0.02)

    d = D_MODEL
    ne, no = N_RS_LAYERS, N_ML_LAYERS
    inp = {}
    inp['x_prompt'] = nrm((BATCH, SEQ, d))
    inp['x_sample'] = nrm((DEC_BATCH, DEC_SEQ, d))
    inp['mem_prompt'] = nrm((BATCH, MEM_LEN, d))
    inp['state_ret'] = nrm((ne, DEC_BATCH, RET_HEADS, RET_DK, RET_DV), 0.3)
    inp['state_s5_re'] = nrm((ne, DEC_BATCH, S5_GROUPS, S5_STATE), 0.3)
    inp['state_s5_im'] = nrm((ne, DEC_BATCH, S5_GROUPS, S5_STATE), 0.3)
    inp['state_mlstm_c'] = nrm((no, DEC_BATCH, ML_HEADS, ML_DH, ML_DH), 0.05)
    inp['state_mlstm_n'] = nrm((no, DEC_BATCH, ML_HEADS, ML_DH), 0.1)
    inp['state_mlstm_m'] = nrm((no, DEC_BATCH, ML_HEADS), 1.0)
    inp['cache_mlstm_conv'] = nrm((no, DEC_BATCH, ML_CONV - 1, ML_INNER))
    inp['cache_mem_k'] = nrm((DEPTH, DEC_BATCH, MEM_LEN, X_HEADS, X_DH))
    inp['cache_mem_v'] = nrm((DEPTH, DEC_BATCH, MEM_LEN, X_HEADS, X_DH))
    inp['norm_mix'] = gain((DEPTH, d))
    inp['norm_cross'] = gain((DEPTH, d))
    inp['norm_mem'] = gain((DEPTH, d))
    inp['norm_mlp'] = gain((DEPTH, d))
    inp['norm_final'] = gain((d,))
    inp['rs_w_in'] = nrm((ne, d, RS_IN), d ** -0.5)
    inp['rs_w_out'] = nrm((ne, RET_WIDTH + S5_WIDTH, d), 0.5 * (RET_WIDTH + S5_WIDTH) ** -0.5)
    inp['ret_gn'] = gain((ne, RET_WIDTH))
    inp['s5_a_re'] = -0.5 + nrm((ne, S5_GROUPS, S5_STATE), 0.01)
    inp['s5_a_im'] = jnp.pi * jnp.arange(S5_STATE, dtype=f32)[None, None, :] + nrm((ne, S5_GROUPS, S5_STATE), 0.01)
    inp['s5_log_dt'] = jax.random.uniform(next(keys), (ne, S5_GROUPS), f32, math.log(1e-3), math.log(1e-1))
    inp['s5_b_re'] = nrm((ne, S5_GROUPS, S5_STATE, S5_GROUP), (2.0 * S5_GROUP) ** -0.5)
    inp['s5_b_im'] = nrm((ne, S5_GROUPS, S5_STATE, S5_GROUP), (2.0 * S5_GROUP) ** -0.5)
    inp['s5_c_re'] = nrm((ne, S5_GROUPS, S5_GROUP, S5_STATE), (2.0 * S5_STATE) ** -0.5)
    inp['s5_c_im'] = nrm((ne, S5_GROUPS, S5_GROUP, S5_STATE), (2.0 * S5_STATE) ** -0.5)
    inp['s5_d'] = nrm((ne, S5_WIDTH), 1.0)
    inp['s5_w_glu'] = nrm((ne, S5_WIDTH, S5_WIDTH), S5_WIDTH ** -0.5)
    inp['s5_b_glu'] = nrm((ne, S5_WIDTH), 0.02)
    inp['ml_w_in'] = nrm((no, d, 2 * ML_INNER), d ** -0.5)
    inp['ml_conv_w'] = nrm((no, ML_CONV, ML_INNER), ML_CONV ** -0.5)
    inp['ml_conv_b'] = nrm((no, ML_INNER), 0.02)
    inp['ml_wq'] = nrm((no, ML_HEADS, ML_DH, ML_DH), ML_DH ** -0.5)
    inp['ml_wk'] = nrm((no, ML_HEADS, ML_DH, ML_DH), ML_DH ** -0.5)
    inp['ml_wv'] = nrm((no, ML_HEADS, ML_DH, ML_DH), ML_DH ** -0.5)
    inp['ml_w_gates'] = nrm((no, 3 * ML_INNER, 2 * ML_HEADS), 0.5 * (3 * ML_INNER) ** -0.5)
    f_bias = jnp.linspace(3.0, 6.0, ML_HEADS, dtype=f32)[None, :] + nrm((no, ML_HEADS), 0.01)
    inp['ml_b_gates'] = jnp.concatenate([nrm((no, ML_HEADS), 0.1), f_bias], axis=-1)
    inp['ml_gn'] = gain((no, ML_INNER))
    inp['ml_skip'] = gain((no, ML_INNER))
    inp['ml_w_out'] = nrm((no, ML_INNER, d), 0.5 * ML_INNER ** -0.5)
    inp['x_wq'] = nrm((DEPTH, d, d), d ** -0.5)
    inp['x_wk'] = nrm((DEPTH, d, d), d ** -0.5)
    inp['x_wv'] = nrm((DEPTH, d, d), d ** -0.5)
    inp['x_wo'] = nrm((DEPTH, d, d), 0.5 * d ** -0.5)
    inp['mlp_w_up'] = nrm((DEPTH, d, D_FF), d ** -0.5)
    inp['mlp_w_down'] = nrm((DEPTH, D_FF, d), 0.5 * D_FF ** -0.5)
    return inp


def reference(x_prompt, x_sample, mem_prompt, state_ret, state_s5_re, state_s5_im, state_mlstm_c,
              state_mlstm_n, state_mlstm_m, cache_mlstm_conv, cache_mem_k, cache_mem_v,
              norm_mix, norm_cross, norm_mem, norm_mlp, norm_final,
              rs_w_in, rs_w_out, ret_gn, s5_a_re, s5_a_im, s5_log_dt, s5_b_re, s5_b_im, s5_c_re, s5_c_im,
              s5_d, s5_w_glu, s5_b_glu,
              ml_w_in, ml_conv_w, ml_conv_b, ml_wq, ml_wk, ml_wv, ml_w_gates, ml_b_gates, ml_gn, ml_skip, ml_w_out,
              x_wq, x_wk, x_wv, x_wo, mlp_w_up, mlp_w_down):
    p = {'norm_mix': norm_mix, 'norm_cross': norm_cross, 'norm_mlp': norm_mlp, 'norm_final': norm_final,
         'rs_w_in': rs_w_in, 'rs_w_out': rs_w_out, 'ret_gn': ret_gn,
         's5_a_re': s5_a_re, 's5_a_im': s5_a_im, 's5_log_dt': s5_log_dt, 's5_b_re': s5_b_re, 's5_b_im': s5_b_im,
         's5_c_re': s5_c_re, 's5_c_im': s5_c_im, 's5_d': s5_d, 's5_w_glu': s5_w_glu, 's5_b_glu': s5_b_glu,
         'ml_w_in': ml_w_in, 'ml_conv_w': ml_conv_w, 'ml_conv_b': ml_conv_b, 'ml_wq': ml_wq, 'ml_wk': ml_wk,
         'ml_wv': ml_wv, 'ml_w_gates': ml_w_gates, 'ml_b_gates': ml_b_gates, 'ml_gn': ml_gn, 'ml_skip': ml_skip,
         'ml_w_out': ml_w_out, 'x_wq': x_wq, 'x_wo': x_wo, 'mlp_w_up': mlp_w_up, 'mlp_w_down': mlp_w_down}
    bp = x_prompt.shape[0]
    f32 = jnp.float32
    mk_rows, mv_rows = [], []
    for l in range(DEPTH):
        k_l, v_l = memory_kv(mem_prompt, norm_mem[l], x_wk[l], x_wv[l])
        mk_rows.append(k_l)
        mv_rows.append(v_l)
    mem_k_p = jnp.stack(mk_rows)
    mem_v_p = jnp.stack(mv_rows)
    y_prompt, ret_p, s5re_p, s5im_p, mc_p, mn_p, mm_p, conv_p = trunk(
        x_prompt, 0,
        jnp.zeros((N_RS_LAYERS, bp, RET_HEADS, RET_DK, RET_DV), f32),
        jnp.zeros((N_RS_LAYERS, bp, S5_GROUPS, S5_STATE), f32),
        jnp.zeros((N_RS_LAYERS, bp, S5_GROUPS, S5_STATE), f32),
        jnp.zeros((N_ML_LAYERS, bp, ML_HEADS, ML_DH, ML_DH), f32),
        jnp.zeros((N_ML_LAYERS, bp, ML_HEADS, ML_DH), f32),
        jnp.zeros((N_ML_LAYERS, bp, ML_HEADS), f32),
        jnp.zeros((N_ML_LAYERS, bp, ML_CONV - 1, ML_INNER), x_prompt.dtype),
        mem_k_p, mem_v_p, p)
    y_sample, ret_s, s5re_s, s5im_s, mc_s, mn_s, mm_s, conv_s = trunk(
        x_sample, PAST_LEN, state_ret, state_s5_re, state_s5_im, state_mlstm_c, state_mlstm_n,
        state_mlstm_m, cache_mlstm_conv, cache_mem_k, cache_mem_v, p)
    return (y_prompt, y_sample, ret_p, s5re_p, s5im_p, mc_p, mn_p, mm_p, conv_p, mem_k_p, mem_v_p,
            ret_s, s5re_s, s5im_s, mc_s, mn_s, mm_s, conv_s)
```

```python
import functools
import math

import jax
import jax.numpy as jnp
from jax import lax
from jax.experimental import pallas as pl
from jax.experimental.pallas import tpu as pltpu

F32 = jnp.float32
BF16 = jnp.bfloat16

EPS = 1e-6
ROPE_BASE = 10000.0
RET_HEADS = 4
RET_DK = 128
S5_GROUPS = 32
S5_GROUP = 16
S5_STATE = 64
ML_HEADS = 4
ML_DH = 512
ML_CONV = 4
X_HEADS = 4
X_DH = 256
LANES = 128
SUBLANES = 8
VMEM_LIMIT = 56 * 1024 * 1024
TOKEN_TILE = 256
MLP_ROW_TILE = 512
MLP_FF_TILE = 1024
MEM_ROW_TILE = 1024


def _dot(a, b):
    return jnp.dot(a, b, preferred_element_type=F32)


def _dot_nt(a, b):
    return lax.dot_general(a, b, (((1,), (1,)), ((), ())), preferred_element_type=F32)


def _dot_tn(a, b):
    return lax.dot_general(a, b, (((0,), (0,)), ((), ())), preferred_element_type=F32)


def _rms(x, g):
    y = x * lax.rsqrt(jnp.mean(x * x, axis=-1, keepdims=True) + EPS)
    return y * g


def _layernorm(x):
    mu = jnp.mean(x, axis=-1, keepdims=True)
    xc = x - mu
    var = jnp.mean(xc * xc, axis=-1, keepdims=True)
    return xc * lax.rsqrt(var + EPS)


def _sigmoid(x):
    return 1.0 / (1.0 + jnp.exp(-x))


def _log_sigmoid(x):
    return -(jnp.maximum(-x, 0.0) + jnp.log(1.0 + jnp.exp(-jnp.abs(x))))


def _gelu_tanh(x):
    c = math.sqrt(2.0 / math.pi)
    return 0.5 * x * (1.0 + jnp.tanh(c * (x + 0.044715 * (x * x * x))))


def _params(*sem):
    return pltpu.CompilerParams(dimension_semantics=sem, vmem_limit_bytes=VMEM_LIMIT)


def _full(shape):
    n = len(shape)
    return pl.BlockSpec(shape, lambda *_: (0,) * n)


def _s5_prep_kernel(are_ref, aim_ref, ldt_ref, bre_ref, bim_ref, abre_ref, abim_ref, bbre_ref, bbim_ref):
    a_re = are_ref[...]
    a_im = aim_ref[...]
    dt = jnp.exp(ldt_ref[...])
    mag = jnp.exp(a_re * dt)
    ab_re = mag * jnp.cos(a_im * dt)
    ab_im = mag * jnp.sin(a_im * dt)
    den = a_re * a_re + a_im * a_im
    x_re = ab_re - 1.0
    f_re = (x_re * a_re + ab_im * a_im) / den
    f_im = (ab_im * a_re - x_re * a_im) / den
    b_re = bre_ref[...]
    b_im = bim_ref[...]
    abre_ref[...] = ab_re
    abim_ref[...] = ab_im
    bbre_ref[...] = f_re * b_re - f_im * b_im
    bbim_ref[...] = f_re * b_im + f_im * b_re


def _s5_prep(a_re, a_im, log_dt, b_re, b_im):
    g, p, c = b_re.shape
    out = pl.pallas_call(
        _s5_prep_kernel,
        out_shape=(jax.ShapeDtypeStruct((g, 1, p), F32), jax.ShapeDtypeStruct((g, 1, p), F32),
                   jax.ShapeDtypeStruct((g, c, p), F32), jax.ShapeDtypeStruct((g, c, p), F32)),
        name="s5_prep",
    )(a_re.reshape(g, 1, p), a_im.reshape(g, 1, p), log_dt.reshape(g, 1, 1),
      jnp.swapaxes(b_re, 1, 2), jnp.swapaxes(b_im, 1, 2))
    return out


def _s5_matrices(ab_re, ab_im, bb_re, bb_im, c_re, c_im):
    g, c, p = bb_re.shape
    nsl = g * c // LANES
    gs = LANES // c
    eye = jnp.eye(gs, dtype=F32)

    def bmat(bb):
        t = bb.reshape(nsl, gs, c, p)
        w = jnp.einsum('mgcp,hg->mhcgp', t, eye)
        return w.reshape(nsl, gs * c, gs * p)

    def cmat(cc):
        t = cc.reshape(nsl, gs, c, p)
        w = jnp.einsum('mgcp,hg->mhpgc', t, eye)
        return w.reshape(nsl, gs * p, gs * c)

    b_all = jnp.stack([bmat(bb_re), bmat(bb_im)], axis=1).reshape(nsl * 2, gs * c, gs * p)
    c_all = jnp.concatenate([cmat(c_re), -cmat(c_im)], axis=1)
    nst = g * p // (SUBLANES * LANES)
    a_all = jnp.concatenate([ab_re.reshape(nst, SUBLANES, LANES), ab_im.reshape(nst, SUBLANES, LANES)], axis=0)
    return a_all, b_all.astype(BF16), c_all.astype(BF16)


def _rs_kernel(x_ref, gn_ref, win_ref, cos_ref, sin_ref, decay_ref, wq_ref, wkv_ref, gc_ref,
               s0_ref, h0_ref, rgn_ref, a_ref, bm_ref, cm_ref, dsk_ref, wglu_ref, bglu_ref, wout_ref,
               xo_ref, s_ref, h_ref, bu_ref, *, tl, nl):
    l = pl.program_id(1)
    qk_w = RET_HEADS * RET_DK
    nsl = S5_GROUPS * S5_GROUP // LANES
    nst = S5_GROUPS * S5_STATE // (SUBLANES * LANES)
    rows_per_slab = SUBLANES // (nsl // nst)

    @pl.when(l == 0)
    def _():
        s_ref[...] = s0_ref[...]
        h_ref[...] = h0_ref[...]

    x = x_ref[...]
    xn = _rms(x, gn_ref[...]).astype(BF16)
    proj = _dot(xn, win_ref[...])
    cos = cos_ref[...]
    sin = sin_ref[...]
    pieces = []
    for h in range(RET_HEADS):
        hs = slice(h * RET_DK, (h + 1) * RET_DK)
        q = proj[:, h * RET_DK:(h + 1) * RET_DK]
        k = proj[:, qk_w + h * RET_DK: qk_w + (h + 1) * RET_DK]
        v = proj[:, 2 * qk_w + h * RET_DK: 2 * qk_w + (h + 1) * RET_DK]
        g = proj[:, 3 * qk_w + h * RET_DK: 3 * qk_w + (h + 1) * RET_DK]
        qr = q * cos + pltpu.roll(q, RET_DK // 2, 1) * sin
        kr = (k * cos + pltpu.roll(k, RET_DK // 2, 1) * sin) * (RET_DK ** -0.5)
        qb = qr.astype(BF16)
        kb = kr.astype(BF16)
        vb = v.astype(BF16)
        sc = _dot_nt(qb, kb) * decay_ref[h]
        s_prev = s_ref[h]
        o = _dot(sc.astype(BF16), vb) + _dot(qb, s_prev.astype(BF16)) * wq_ref[:, hs]
        s_ref[h] = gc_ref[h] * s_prev + _dot_tn((kr * wkv_ref[:, hs]).astype(BF16), vb)
        y = _layernorm(o) * rgn_ref[:, hs]
        pieces.append((g * _sigmoid(g) * y).astype(BF16))

    u_off = 4 * qk_w
    for m in range(nsl):
        um = proj[:, u_off + m * LANES: u_off + (m + 1) * LANES].astype(BF16)
        for ri in range(2):
            r = _dot(um, bm_ref[2 * m + ri])
            slab = ri * nst + m // (nsl // nst)
            for jl in range(rows_per_slab):
                j = rows_per_slab * (m % (nsl // nst)) + jl
                bu_ref[slab, pl.ds(j, tl, stride=SUBLANES), :] = r[:, jl * LANES:(jl + 1) * LANES]

    a = [a_ref[i] for i in range(2 * nst)]

    def scan_step(t, carry):
        row = pl.multiple_of(t * SUBLANES, SUBLANES)
        new = []
        for s in range(nst):
            hr, hi = carry[s], carry[nst + s]
            ar, ai = a[s], a[nst + s]
            nr = ar * hr - ai * hi + bu_ref[s, pl.ds(row, SUBLANES), :]
            ni = ar * hi + ai * hr + bu_ref[nst + s, pl.ds(row, SUBLANES), :]
            bu_ref[s, pl.ds(row, SUBLANES), :] = nr
            bu_ref[nst + s, pl.ds(row, SUBLANES), :] = ni
            new.append((nr, ni))
        return tuple(n[0] for n in new) + tuple(n[1] for n in new)

    hfin = lax.fori_loop(0, tl, scan_step, tuple(h_ref[i] for i in range(2 * nst)), unroll=8)
    for i in range(2 * nst):
        h_ref[i] = hfin[i]

    ys = []
    for m in range(nsl):
        parts = []
        for ri in range(2):
            slab = ri * nst + m // (nsl // nst)
            for jl in range(rows_per_slab):
                j = rows_per_slab * (m % (nsl // nst)) + jl
                parts.append(bu_ref[slab, pl.ds(j, tl, stride=SUBLANES), :])
        hcat = jnp.concatenate(parts, axis=1).astype(BF16)
        um = proj[:, u_off + m * LANES: u_off + (m + 1) * LANES]
        ym = _dot(hcat, cm_ref[m]) + dsk_ref[:, m * LANES:(m + 1) * LANES] * um
        ys.append(_gelu_tanh(ym))
    yg = jnp.concatenate(ys, axis=1)
    gate = _sigmoid(_dot(yg.astype(BF16), wglu_ref[...]) + bglu_ref[...])
    pieces.append((yg * gate).astype(BF16))
    ycat = jnp.concatenate(pieces, axis=1)
    xo_ref[...] = x + _dot(ycat, wout_ref[...])


def _rs_layer(x, st_ret, st_h, consts, w):
    b, L, d = x.shape
    tl = min(TOKEN_TILE, L)
    nl = L // tl
    cos, sin, decay, wq, wkv, gc = consts
    rs_in = w['rs_w_in'].shape[1]
    nst2 = st_h.shape[1]
    kern = functools.partial(_rs_kernel, tl=tl, nl=nl)
    tok = pl.BlockSpec((None, tl, d), lambda i, j: (i, j, 0))
    in_specs = [
        tok, _full((1, d)), _full((d, rs_in)),
        pl.BlockSpec((tl, LANES), lambda i, j: (j, 0)), pl.BlockSpec((tl, LANES), lambda i, j: (j, 0)),
        _full(decay.shape), _full(wq.shape), _full(wkv.shape), _full(gc.shape),
        pl.BlockSpec((None,) + st_ret.shape[1:], lambda i, j: (i, 0, 0, 0)),
        pl.BlockSpec((None,) + st_h.shape[1:], lambda i, j: (i, 0, 0, 0)),
        _full((1, RET_HEADS * RET_DK)), _full(w['s5_a'].shape), _full(w['s5_b'].shape), _full(w['s5_c'].shape),
        _full((1, w['s5_d'].shape[1])), _full(w['s5_w_glu'].shape), _full((1, w['s5_b_glu'].shape[1])),
        _full(w['rs_w_out'].shape),
    ]
    out_specs = [
        tok,
        pl.BlockSpec((None,) + st_ret.shape[1:], lambda i, j: (i, 0, 0, 0)),
        pl.BlockSpec((None,) + st_h.shape[1:], lambda i, j: (i, 0, 0, 0)),
    ]
    return pl.pallas_call(
        kern, grid=(b, nl), in_specs=in_specs, out_specs=out_specs,
        out_shape=(jax.ShapeDtypeStruct(x.shape, F32), jax.ShapeDtypeStruct(st_ret.shape, F32),
                   jax.ShapeDtypeStruct(st_h.shape, F32)),
        scratch_shapes=[pltpu.VMEM((nst2, tl * SUBLANES, LANES), F32)],
        compiler_params=_params("parallel", "arbitrary"), name="rs_mixer",
    )(x, w['norm_mix0'], w['rs_w_in'], cos, sin, decay, wq, wkv, gc, st_ret, st_h, w['ret_gn'],
      w['s5_a'], w['s5_b'], w['s5_c'], w['s5_d'], w['s5_w_glu'], w['s5_b_glu'], w['rs_w_out'])


def _ret_consts(L, tl, pos0):
    half = RET_DK // 2
    pos = pos0 + jnp.arange(L, dtype=jnp.int32)
    inv = ROPE_BASE ** (-jnp.arange(half, dtype=F32) / half)
    ang = pos.astype(F32)[:, None] * inv[None, :]
    cos = jnp.cos(ang)
    sin = jnp.sin(ang)
    cos2 = jnp.concatenate([cos, cos], axis=1)
    sin2 = jnp.concatenate([-sin, sin], axis=1)
    log_g = jnp.log1p(-jnp.exp2(-5.0 - jnp.arange(RET_HEADS, dtype=F32)))
    idx = jnp.arange(tl, dtype=F32)
    diff = idx[:, None] - idx[None, :]
    decay = jnp.where(diff[None] >= 0, jnp.exp(jnp.maximum(diff, 0.0)[None] * log_g[:, None, None]), 0.0)
    w_q = jnp.exp((idx + 1.0)[:, None] * log_g[None, :])
    w_kv = jnp.exp((tl - 1.0 - idx)[:, None] * log_g[None, :])
    g_c = jnp.exp(tl * log_g)
    rep = lambda t: jnp.repeat(t, RET_DK, axis=1)
    gc = jnp.broadcast_to(g_c[:, None, None], (RET_HEADS, 1, RET_DK))
    return cos2, sin2, decay, rep(w_q), rep(w_kv), gc


def _ml_a_kernel(x_ref, gn_ref, win_ref, cs_ref, cw_ref, cb_ref, wq_ref, wk_ref, wv_ref, wg_ref, bg_ref,
                 q_ref, k_ref, v_ref, gates_ref, z_ref, xc_ref, nc_ref, xs_ref, *, tl, nl):
    l = pl.program_id(1)
    inner = ML_HEADS * ML_DH
    pad = SUBLANES
    hist = ML_CONV - 1

    @pl.when(l == 0)
    def _():
        xs_ref[0:pad, :] = jnp.zeros((pad, inner), F32)
        xs_ref[pl.ds(pad - hist, hist), :] = cs_ref[...]

    xn = _rms(x_ref[...], gn_ref[...]).astype(BF16)
    xm = _dot(xn, win_ref[:, 0:inner])
    z_ref[...] = _dot(xn, win_ref[:, inner:2 * inner]).astype(z_ref.dtype)
    xs_ref[pl.ds(pad, tl), :] = xm
    acc = cw_ref[hist:hist + 1, :] * xm
    for j in range(hist):
        acc = acc + cw_ref[j:j + 1, :] * xs_ref[pl.ds(pad - hist + j, tl), :]
    xc = acc + cb_ref[...]
    xc = xc * _sigmoid(xc)
    xc_ref[...] = xc.astype(xc_ref.dtype)
    gates = jnp.zeros((tl, LANES), F32) + bg_ref[...]
    for h in range(ML_HEADS):
        hs = slice(h * ML_DH, (h + 1) * ML_DH)
        xch = xc[:, hs].astype(BF16)
        xmh = xm[:, hs].astype(BF16)
        qh = _dot(xch, wq_ref[h]).astype(BF16)
        kh = _dot(xch, wk_ref[h]).astype(BF16)
        vh = _dot(xmh, wv_ref[h]).astype(BF16)
        q_ref[:, hs] = qh
        k_ref[:, hs] = kh
        v_ref[:, hs] = vh
        gates = gates + _dot(qh, wg_ref[h * ML_DH:(h + 1) * ML_DH, :])
        gates = gates + _dot(kh, wg_ref[inner + h * ML_DH: inner + (h + 1) * ML_DH, :])
        gates = gates + _dot(vh, wg_ref[2 * inner + h * ML_DH: 2 * inner + (h + 1) * ML_DH, :])
    gates_ref[...] = gates

    @pl.when(l == nl - 1)
    def _():
        nc_ref[...] = xs_ref[pl.ds(pad + tl - hist, hist), :]

    xs_ref[0:pad, :] = xs_ref[pl.ds(tl, pad), :]


def _ml_a(x, conv_state, w):
    b, L, d = x.shape
    tl = min(TOKEN_TILE, L)
    nl = L // tl
    inner = ML_HEADS * ML_DH
    kern = functools.partial(_ml_a_kernel, tl=tl, nl=nl)
    tok = lambda width: pl.BlockSpec((None, tl, width), lambda i, j: (i, j, 0))
    cst = pl.BlockSpec((None, ML_CONV - 1, inner), lambda i, j: (i, 0, 0))
    in_specs = [tok(d), _full((1, d)), _full((d, 2 * inner)), cst, _full((ML_CONV, inner)), _full((1, inner)),
                _full(w['ml_wq'].shape), _full(w['ml_wk'].shape), _full(w['ml_wv'].shape),
                _full(w['ml_w_gates'].shape), _full((1, LANES))]
    out_specs = [tok(inner), tok(inner), tok(inner), tok(LANES), tok(inner), tok(inner), cst]
    sds = jax.ShapeDtypeStruct
    out_shape = (sds((b, L, inner), BF16), sds((b, L, inner), BF16), sds((b, L, inner), BF16),
                 sds((b, L, LANES), F32), sds((b, L, inner), BF16), sds((b, L, inner), BF16),
                 sds((b, ML_CONV - 1, inner), F32))
    return pl.pallas_call(
        kern, grid=(b, nl), in_specs=in_specs, out_specs=out_specs, out_shape=out_shape,
        scratch_shapes=[pltpu.VMEM((tl + SUBLANES, inner), F32)],
        compiler_params=_params("parallel", "arbitrary"), name="mlstm_proj",
    )(x, w['norm_mix1'], w['ml_w_in'], conv_state, w['ml_conv_w'], w['ml_conv_b'],
      w['ml_wq'], w['ml_wk'], w['ml_wv'], w['ml_w_gates'], w['ml_b_gates'])


def _split3(x):
    hi = x.astype(BF16)
    r1 = x - hi.astype(F32)
    mid = r1.astype(BF16)
    lo = (r1 - mid.astype(F32)).astype(BF16)
    return hi, mid, lo


def _ml_b_kernel(*refs, tl, nl, has_state):
    if has_state:
        (x_ref, q_ref, k_ref, v_ref, g_ref, z_ref, xc_ref, c0_ref, n0_ref, m0_ref, gn_ref, skip_ref, wout_ref,
         xo_ref, c_ref, n_ref, m_ref) = refs
    else:
        (x_ref, q_ref, k_ref, v_ref, g_ref, z_ref, xc_ref, gn_ref, skip_ref, wout_ref,
         xo_ref, c_ref, n_ref, m_ref) = refs
    l = pl.program_id(1)
    scale = ML_DH ** -0.5

    @pl.when(l == 0)
    def _():
        if has_state:
            c_ref[...] = c0_ref[...]
            n_ref[...] = n0_ref[...]
            m_ref[...] = m0_ref[...]
        else:
            c_ref[...] = jnp.zeros(c_ref.shape, F32)
            n_ref[...] = jnp.zeros(n_ref.shape, F32)
            m_ref[...] = jnp.zeros(m_ref.shape, F32)

    gates = g_ref[...]
    gates_t = gates.T
    row = lax.broadcasted_iota(jnp.int32, (tl, tl), 0)
    col = lax.broadcasted_iota(jnp.int32, (tl, tl), 1)
    causal = row >= col
    lower = jnp.where(causal, 1.0, 0.0).astype(BF16)
    upper = jnp.where(row <= col, 1.0, 0.0).astype(BF16)
    lf_c = _split3(_log_sigmoid(gates))
    b_col_all = _dot(lower, lf_c[0]) + _dot(lower, lf_c[1]) + _dot(lower, lf_c[2])
    lf_r = _split3(_log_sigmoid(gates_t[0:2 * ML_HEADS, :]))
    b_row_all = _dot(lf_r[0], upper) + _dot(lf_r[1], upper) + _dot(lf_r[2], upper)

    ys = []
    for h in range(ML_HEADS):
        hs = slice(h * ML_DH, (h + 1) * ML_DH)
        q = q_ref[:, hs]
        k = k_ref[:, hs]
        v = v_ref[:, hs]
        b_col = b_col_all[:, ML_HEADS + h: ML_HEADS + h + 1]
        i_col = gates[:, h:h + 1]
        b_row = b_row_all[ML_HEADS + h: ML_HEADS + h + 1, :]
        i_row = gates_t[h:h + 1, :]
        m_prev = m_ref[h]
        dlog = jnp.where(causal, b_col - b_row + i_row, -jnp.inf)
        inter = b_col + m_prev
        m_t = jnp.maximum(inter, jnp.max(dlog, axis=-1, keepdims=True))
        wgt = jnp.exp(dlog - m_t)
        s_inter = jnp.exp(inter - m_t)
        qk = _dot_nt(q, k) * scale * wgt
        c_prev = c_ref[h]
        n_prev = n_ref[h:h + 1, :]
        num = s_inter * _dot(q, c_prev.astype(BF16)) + _dot(qk.astype(BF16), v)
        den = (s_inter * jnp.sum(q.astype(F32) * n_prev, axis=-1, keepdims=True)
               + jnp.sum(qk, axis=-1, keepdims=True))
        hh = num / jnp.maximum(jnp.abs(den), jnp.exp(-m_t))
        m_new = m_t[tl - 1:tl, :]
        b_last = b_col[tl - 1:tl, :]
        w_s = jnp.exp(b_last - b_col + i_col - m_new)
        dec = jnp.exp(b_last + m_prev - m_new)
        kw = k.astype(F32) * (scale * w_s)
        c_ref[h] = dec * c_prev + _dot_tn(kw.astype(BF16), v)
        n_ref[h:h + 1, :] = dec * n_prev + jnp.sum(kw, axis=0, keepdims=True)
        m_ref[h] = m_new
        o = _sigmoid(z_ref[:, hs].astype(F32)) * hh
        y = _layernorm(o) * gn_ref[:, hs] + skip_ref[:, hs] * xc_ref[:, hs].astype(F32)
        ys.append(y.astype(BF16))
    ycat = jnp.concatenate(ys, axis=1)
    xo_ref[...] = x_ref[...] + _dot(ycat, wout_ref[...])


def _ml_b(x, q, k, v, gates, z, xc, state, w):
    b, L, d = x.shape
    tl = min(TOKEN_TILE, L)
    nl = L // tl
    inner = ML_HEADS * ML_DH
    has_state = state is not None
    kern = functools.partial(_ml_b_kernel, tl=tl, nl=nl, has_state=has_state)
    tok = lambda width: pl.BlockSpec((None, tl, width), lambda i, j: (i, j, 0))
    c_spec = pl.BlockSpec((None, ML_HEADS, ML_DH, ML_DH), lambda i, j: (i, 0, 0, 0))
    n_spec = pl.BlockSpec((None, ML_HEADS, ML_DH), lambda i, j: (i, 0, 0))
    m_spec = pl.BlockSpec((None, ML_HEADS, 1, 1), lambda i, j: (i, 0, 0, 0))
    in_specs = [tok(d), tok(inner), tok(inner), tok(inner), tok(LANES), tok(inner), tok(inner)]
    args = [x, q, k, v, gates, z, xc]
    if has_state:
        in_specs += [c_spec, n_spec, m_spec]
        args += [state[0], state[1], state[2].reshape(b, ML_HEADS, 1, 1)]
    in_specs += [_full((1, inner)), _full((1, inner)), _full((inner, d))]
    args += [w['ml_gn'], w['ml_skip'], w['ml_w_out']]
    sds = jax.ShapeDtypeStruct
    out_shape = (sds(x.shape, F32), sds((b, ML_HEADS, ML_DH, ML_DH), F32), sds((b, ML_HEADS, ML_DH), F32),
                 sds((b, ML_HEADS, 1, 1), F32))
    xo, c, n, m = pl.pallas_call(
        kern, grid=(b, nl), in_specs=in_specs, out_specs=[tok(d), c_spec, n_spec, m_spec], out_shape=out_shape,
        compiler_params=_params("parallel", "arbitrary"), name="mlstm_cell",
    )(*args)
    return xo, c, n, m.reshape(b, ML_HEADS)


def _memkv_kernel(m_ref, g_ref, wk_ref, wv_ref, k_ref, v_ref):
    mn = _rms(m_ref[...], g_ref[...]).astype(BF16)
    k_ref[...] = _dot(mn, wk_ref[...])
    v_ref[...] = _dot(mn, wv_ref[...])


def _memkv(mem, norm_mem, wk, wv):
    b, m_len, d = mem.shape
    depth = wk.shape[0]
    rows = b * m_len
    tm = min(MEM_ROW_TILE, rows)
    mem2 = mem.reshape(rows, d)
    wspec = pl.BlockSpec((None, d, d), lambda i, j: (i, 0, 0))
    ospec = pl.BlockSpec((None, tm, d), lambda i, j: (i, j, 0))
    k, v = pl.pallas_call(
        _memkv_kernel, grid=(depth, rows // tm),
        in_specs=[pl.BlockSpec((tm, d), lambda i, j: (j, 0)), pl.BlockSpec((None, 1, d), lambda i, j: (i, 0, 0)),
                  wspec, wspec],
        out_specs=[ospec, ospec],
        out_shape=(jax.ShapeDtypeStruct((depth, rows, d), F32), jax.ShapeDtypeStruct((depth, rows, d), F32)),
        compiler_params=_params("parallel", "parallel"), name="memory_kv",
    )(mem2, norm_mem.reshape(depth, 1, d), wk, wv)
    return k.reshape(depth, b, m_len, d), v.reshape(depth, b, m_len, d)


def _xattn_kernel(x_ref, g_ref, wq_ref, mk_ref, mv_ref, wo_ref, o_ref):
    x = x_ref[...]
    xn = _rms(x, g_ref[...]).astype(BF16)
    q = _dot(xn, wq_ref[...])
    outs = []
    for h in range(X_HEADS):
        hs = slice(h * X_DH, (h + 1) * X_DH)
        qh = q[:, hs].astype(BF16)
        kh = mk_ref[:, hs].astype(BF16)
        vh = mv_ref[:, hs].astype(BF16)
        s = _dot_nt(qh, kh) * (X_DH ** -0.5)
        e = jnp.exp(s - jnp.max(s, axis=-1, keepdims=True))
        p = e / jnp.sum(e, axis=-1, keepdims=True)
        outs.append(_dot(p.astype(BF16), vh).astype(BF16))
    o_ref[...] = x + _dot(jnp.concatenate(outs, axis=1), wo_ref[...])


def _xattn(x, g, wq, mk, mv, wo):
    b, L, d = x.shape
    tl = min(TOKEN_TILE, L)
    m_len = mk.shape[1]
    tok = pl.BlockSpec((None, tl, d), lambda i, j: (i, j, 0))
    mem = pl.BlockSpec((None, m_len, d), lambda i, j: (i, 0, 0))
    return pl.pallas_call(
        _xattn_kernel, grid=(b, L // tl),
        in_specs=[tok, _full((1, d)), _full((d, d)), mem, mem, _full((d, d))],
        out_specs=tok, out_shape=jax.ShapeDtypeStruct(x.shape, F32),
        compiler_params=_params("parallel", "arbitrary"), name="cross_attn",
    )(x, g, wq, mk, mv, wo)


def _mlp_kernel(*refs, final):
    if final:
        x_ref, g_ref, wu_ref, wd_ref, gf_ref, o_ref, acc_ref, xn_ref = refs
    else:
        x_ref, g_ref, wu_ref, wd_ref, o_ref, acc_ref, xn_ref = refs
    j = pl.program_id(1)

    @pl.when(j == 0)
    def _():
        xn_ref[...] = _rms(x_ref[...], g_ref[...]).astype(BF16)
        acc_ref[...] = jnp.zeros(acc_ref.shape, F32)

    hdn = jnp.maximum(_dot(xn_ref[...], wu_ref[...]), 0.0)
    acc_ref[...] += _dot((hdn * hdn).astype(BF16), wd_ref[...])

    @pl.when(j == pl.num_programs(1) - 1)
    def _():
        out = x_ref[...] + acc_ref[...]
        if final:
            out = _rms(out, gf_ref[...])
        o_ref[...] = out


def _mlp(x, g, wu, wd, gf=None):
    shape = x.shape
    d = shape[-1]
    x2 = x.reshape(-1, d)
    rows = x2.shape[0]
    ff = wu.shape[1]
    tm = min(MLP_ROW_TILE, rows)
    tf = min(MLP_FF_TILE, ff)
    final = gf is not None
    in_specs = [pl.BlockSpec((tm, d), lambda i, j: (i, 0)), _full((1, d)),
                pl.BlockSpec((d, tf), lambda i, j: (0, j)), pl.BlockSpec((tf, d), lambda i, j: (j, 0))]
    args = [x2, g, wu, wd]
    if final:
        in_specs.append(_full((1, d)))
        args.append(gf)
    out = pl.pallas_call(
        functools.partial(_mlp_kernel, final=final), grid=(rows // tm, ff // tf),
        in_specs=in_specs, out_specs=pl.BlockSpec((tm, d), lambda i, j: (i, 0)),
        out_shape=jax.ShapeDtypeStruct(x2.shape, F32),
        scratch_shapes=[pltpu.VMEM((tm, d), F32), pltpu.VMEM((tm, d), BF16)],
        compiler_params=_params("parallel", "arbitrary"), name="mlp",
    )(*args)
    return out.reshape(shape)


def _trunk(x, pos0, st_ret, st_re, st_im, ml_state, conv_state, mem_k, mem_v, w):
    b, L, d = x.shape
    tl = min(TOKEN_TILE, L)
    nst = S5_GROUPS * S5_STATE // (SUBLANES * LANES)
    st_h = jnp.concatenate([st_re.reshape(b, nst, SUBLANES, LANES), st_im.reshape(b, nst, SUBLANES, LANES)], axis=1)
    x, s_new, h_new = _rs_layer(x, st_ret, st_h, _ret_consts(L, tl, pos0), w)
    hr = h_new[:, :nst].reshape(b, S5_GROUPS, S5_STATE)
    hi = h_new[:, nst:].reshape(b, S5_GROUPS, S5_STATE)
    x = _xattn(x, w['norm_cross'][0], w['x_wq'][0], mem_k[0], mem_v[0], w['x_wo'][0])
    x = _mlp(x, w['norm_mlp'][0], w['mlp_w_up'][0], w['mlp_w_down'][0])
    q, k, v, gates, z, xc, new_conv = _ml_a(x, conv_state, w)
    x, cf, nf, mf = _ml_b(x, q, k, v, gates, z, xc, ml_state, w)
    x = _xattn(x, w['norm_cross'][1], w['x_wq'][1], mem_k[1], mem_v[1], w['x_wo'][1])
    y = _mlp(x, w['norm_mlp'][1], w['mlp_w_up'][1], w['mlp_w_down'][1], w['norm_final'])
    return y, s_new[None], hr[None], hi[None], cf[None], nf[None], mf[None], new_conv[None]


def kernel(x_prompt, x_sample, mem_prompt, state_ret, state_s5_re, state_s5_im, state_mlstm_c, state_mlstm_n, state_mlstm_m, cache_mlstm_conv, cache_mem_k, cache_mem_v, norm_mix, norm_cross, norm_mem, norm_mlp, norm_final, rs_w_in, rs_w_out, ret_gn, s5_a_re, s5_a_im, s5_log_dt, s5_b_re, s5_b_im, s5_c_re, s5_c_im, s5_d, s5_w_glu, s5_b_glu, ml_w_in, ml_conv_w, ml_conv_b, ml_wq, ml_wk, ml_wv, ml_w_gates, ml_b_gates, ml_gn, ml_skip, ml_w_out, x_wq, x_wk, x_wv, x_wo, mlp_w_up, mlp_w_down):
    d = x_prompt.shape[-1]
    bp = x_prompt.shape[0]
    bf = lambda t: t.astype(BF16)
    row = lambda t: t.reshape(1, -1).astype(F32)
    ab_re, ab_im, bb_re, bb_im = _s5_prep(s5_a_re[0], s5_a_im[0], s5_log_dt[0], s5_b_re[0], s5_b_im[0])
    s5_a, s5_b, s5_c = _s5_matrices(ab_re, ab_im, bb_re, bb_im, s5_c_re[0], s5_c_im[0])
    n_gate = ml_w_gates.shape[-1]
    w = {
        'norm_mix0': row(norm_mix[0]), 'norm_mix1': row(norm_mix[1]),
        'norm_cross': norm_cross.reshape(-1, 1, d), 'norm_mlp': norm_mlp.reshape(-1, 1, d),
        'norm_final': row(norm_final),
        'rs_w_in': bf(rs_w_in[0]), 'rs_w_out': bf(rs_w_out[0]), 'ret_gn': row(ret_gn[0]),
        's5_a': s5_a, 's5_b': s5_b, 's5_c': s5_c, 's5_d': row(s5_d[0]), 's5_w_glu': bf(s5_w_glu[0]),
        's5_b_glu': row(s5_b_glu[0]),
        'ml_w_in': bf(ml_w_in[0]), 'ml_conv_w': ml_conv_w[0], 'ml_conv_b': row(ml_conv_b[0]),
        'ml_wq': bf(ml_wq[0]), 'ml_wk': bf(ml_wk[0]), 'ml_wv': bf(ml_wv[0]),
        'ml_w_gates': bf(jnp.pad(ml_w_gates[0], ((0, 0), (0, LANES - n_gate)))),
        'ml_b_gates': jnp.pad(ml_b_gates[0], (0, LANES - n_gate)).reshape(1, LANES),
        'ml_gn': row(ml_gn[0]), 'ml_skip': row(ml_skip[0]), 'ml_w_out': bf(ml_w_out[0]),
        'x_wq': bf(x_wq), 'x_wo': bf(x_wo), 'mlp_w_up': bf(mlp_w_up), 'mlp_w_down': bf(mlp_w_down),
    }
    mem_len = mem_prompt.shape[1]
    mk_p, mv_p = _memkv(mem_prompt, norm_mem, bf(x_wk), bf(x_wv))
    zeros = lambda *s: jnp.zeros(s, F32)
    out_p = _trunk(x_prompt, 0, zeros(bp, RET_HEADS, RET_DK, RET_DK), zeros(bp, S5_GROUPS, S5_STATE),
                   zeros(bp, S5_GROUPS, S5_STATE), None, zeros(bp, ML_CONV - 1, ML_HEADS * ML_DH), mk_p, mv_p, w)
    bs = x_sample.shape[0]
    past_len = 4096
    out_s = _trunk(x_sample, past_len, state_ret[0], state_s5_re[0], state_s5_im[0],
                   (state_mlstm_c[0], state_mlstm_n[0], state_mlstm_m[0]), cache_mlstm_conv[0],
                   cache_mem_k.reshape(-1, bs, mem_len, d), cache_mem_v.reshape(-1, bs, mem_len, d), w)
    mem_k_p = mk_p.reshape(-1, bp, mem_len, X_HEADS, X_DH)
    mem_v_p = mv_p.reshape(-1, bp, mem_len, X_HEADS, X_DH)
    return (out_p[0], out_s[0]) + tuple(out_p[1:]) + (mem_k_p, mem_v_p) + tuple(out_s[1:])
```

```python
import functools
import math

import jax
import jax.numpy as jnp
from jax import lax
from jax.experimental import pallas as pl
from jax.experimental.pallas import tpu as pltpu

F32 = jnp.float32
BF16 = jnp.bfloat16

EPS = 1e-6
ROPE_BASE = 10000.0
PAST_LEN = 4096
RET_HEADS = 4
RET_DK = 128
S5_GROUPS = 32
S5_GROUP = 16
S5_STATE = 64
ML_HEADS = 4
ML_DH = 512
ML_CONV = 4
X_HEADS = 4
X_DH = 256
LANES = 128
SUBLANES = 8
VMEM_LIMIT = 56 * 1024 * 1024
TOKEN_TILE = 256
XATTN_TILE = 512
MLP_ROW_TILE = 1024
MLP_FF_TILE = 1024
MEM_ROW_TILE = 1024


def _dot(a, b):
    return jnp.dot(a, b, preferred_element_type=F32)


def _dot_nt(a, b):
    return lax.dot_general(a, b, (((1,), (1,)), ((), ())), preferred_element_type=F32)


def _dot_tn(a, b):
    return lax.dot_general(a, b, (((0,), (0,)), ((), ())), preferred_element_type=F32)


def _rms(x, g):
    y = x * lax.rsqrt(jnp.mean(x * x, axis=-1, keepdims=True) + EPS)
    return y * g


def _layernorm(x):
    mu = jnp.mean(x, axis=-1, keepdims=True)
    xc = x - mu
    var = jnp.mean(xc * xc, axis=-1, keepdims=True)
    return xc * lax.rsqrt(var + EPS)


def _sigmoid(x):
    return 1.0 / (1.0 + jnp.exp(-x))


def _log_sigmoid(x):
    return -(jnp.maximum(-x, 0.0) + jnp.log(1.0 + jnp.exp(-jnp.abs(x))))


def _gelu_tanh(x):
    c = math.sqrt(2.0 / math.pi)
    return 0.5 * x * (1.0 + jnp.tanh(c * (x + 0.044715 * (x * x * x))))


def _params(*sem):
    return pltpu.CompilerParams(dimension_semantics=sem, vmem_limit_bytes=VMEM_LIMIT)


def _full(shape):
    n = len(shape)
    return pl.BlockSpec(shape, lambda *_: (0,) * n)


def _s5_prep_kernel(are_ref, aim_ref, ldt_ref, bre_ref, bim_ref, abre_ref, abim_ref, bbre_ref, bbim_ref):
    a_re = are_ref[...]
    a_im = aim_ref[...]
    dt = jnp.exp(ldt_ref[...])
    mag = jnp.exp(a_re * dt)
    ab_re = mag * jnp.cos(a_im * dt)
    ab_im = mag * jnp.sin(a_im * dt)
    den = a_re * a_re + a_im * a_im
    x_re = ab_re - 1.0
    f_re = (x_re * a_re + ab_im * a_im) / den
    f_im = (ab_im * a_re - x_re * a_im) / den
    b_re = bre_ref[...]
    b_im = bim_ref[...]
    abre_ref[...] = ab_re
    abim_ref[...] = ab_im
    bbre_ref[...] = f_re * b_re - f_im * b_im
    bbim_ref[...] = f_re * b_im + f_im * b_re


def _s5_prep(a_re, a_im, log_dt, b_re, b_im):
    g, p, c = b_re.shape
    out = pl.pallas_call(
        _s5_prep_kernel,
        out_shape=(jax.ShapeDtypeStruct((g, 1, p), F32), jax.ShapeDtypeStruct((g, 1, p), F32),
                   jax.ShapeDtypeStruct((g, c, p), F32), jax.ShapeDtypeStruct((g, c, p), F32)),
        name="s5_prep",
    )(a_re.reshape(g, 1, p), a_im.reshape(g, 1, p), log_dt.reshape(g, 1, 1),
      jnp.swapaxes(b_re, 1, 2), jnp.swapaxes(b_im, 1, 2))
    return out


def _s5_matrices(ab_re, ab_im, bb_re, bb_im, c_re, c_im):
    g, c, p = bb_re.shape
    nsl = g * c // LANES
    gs = LANES // c
    eye = jnp.eye(gs, dtype=F32)

    def bmat(bb):
        t = bb.reshape(nsl, gs, c, p)
        w = jnp.einsum('mgcp,hg->mhcgp', t, eye)
        return w.reshape(nsl, gs * c, gs * p)

    def cmat(cc):
        t = cc.reshape(nsl, gs, c, p)
        w = jnp.einsum('mgcp,hg->mhpgc', t, eye)
        return w.reshape(nsl, gs * p, gs * c)

    b_all = jnp.stack([bmat(bb_re), bmat(bb_im)], axis=1).reshape(nsl * 2, gs * c, gs * p)
    c_all = jnp.concatenate([cmat(c_re), -cmat(c_im)], axis=1)
    nst = g * p // (SUBLANES * LANES)
    a_all = jnp.concatenate([ab_re.reshape(nst, SUBLANES, LANES), ab_im.reshape(nst, SUBLANES, LANES)], axis=0)
    return a_all, b_all.astype(BF16), c_all.astype(BF16)


def _rs_kernel(x_ref, gn_ref, win_ref, cos_ref, sin_ref, decay_ref, wq_ref, wkv_ref, gc_ref,
               s0_ref, h0_ref, rgn_ref, a_ref, bm_ref, cm_ref, dsk_ref, wglu_ref, bglu_ref, wout_ref,
               xo_ref, s_ref, h_ref, bu_ref, *, tl, nl):
    l = pl.program_id(1)
    qk_w = RET_HEADS * RET_DK
    nsl = S5_GROUPS * S5_GROUP // LANES
    nst = S5_GROUPS * S5_STATE // (SUBLANES * LANES)
    rows_per_slab = SUBLANES // (nsl // nst)

    @pl.when(l == 0)
    def _():
        s_ref[...] = s0_ref[...]
        h_ref[...] = h0_ref[...]

    x = x_ref[...]
    xn = _rms(x, gn_ref[...]).astype(BF16)
    proj = _dot(xn, win_ref[...])
    cos = cos_ref[...]
    sin = sin_ref[...]
    pieces = []
    for h in range(RET_HEADS):
        hs = slice(h * RET_DK, (h + 1) * RET_DK)
        q = proj[:, h * RET_DK:(h + 1) * RET_DK]
        k = proj[:, qk_w + h * RET_DK: qk_w + (h + 1) * RET_DK]
        v = proj[:, 2 * qk_w + h * RET_DK: 2 * qk_w + (h + 1) * RET_DK]
        g = proj[:, 3 * qk_w + h * RET_DK: 3 * qk_w + (h + 1) * RET_DK]
        qr = q * cos + pltpu.roll(q, RET_DK // 2, 1) * sin
        kr = (k * cos + pltpu.roll(k, RET_DK // 2, 1) * sin) * (RET_DK ** -0.5)
        qb = qr.astype(BF16)
        kb = kr.astype(BF16)
        vb = v.astype(BF16)
        sc = _dot_nt(qb, kb) * decay_ref[h]
        s_prev = s_ref[h]
        o = _dot(sc.astype(BF16), vb) + _dot(qb, s_prev.astype(BF16)) * wq_ref[:, hs]
        s_ref[h] = gc_ref[h] * s_prev + _dot_tn((kr * wkv_ref[:, hs]).astype(BF16), vb)
        y = _layernorm(o) * rgn_ref[:, hs]
        pieces.append((g * _sigmoid(g) * y).astype(BF16))

    u_off = 4 * qk_w
    for m in range(nsl):
        um = proj[:, u_off + m * LANES: u_off + (m + 1) * LANES].astype(BF16)
        for ri in range(2):
            r = _dot(um, bm_ref[2 * m + ri])
            slab = ri * nst + m // (nsl // nst)
            for jl in range(rows_per_slab):
                j = rows_per_slab * (m % (nsl // nst)) + jl
                bu_ref[slab, pl.ds(j, tl, stride=SUBLANES), :] = r[:, jl * LANES:(jl + 1) * LANES]

    a = [a_ref[i] for i in range(2 * nst)]

    def scan_step(t, carry):
        row = pl.multiple_of(t * SUBLANES, SUBLANES)
        new = []
        for s in range(nst):
            hr, hi = carry[s], carry[nst + s]
            ar, ai = a[s], a[nst + s]
            nr = ar * hr - ai * hi + bu_ref[s, pl.ds(row, SUBLANES), :]
            ni = ar * hi + ai * hr + bu_ref[nst + s, pl.ds(row, SUBLANES), :]
            bu_ref[s, pl.ds(row, SUBLANES), :] = nr
            bu_ref[nst + s, pl.ds(row, SUBLANES), :] = ni
            new.append((nr, ni))
        return tuple(n[0] for n in new) + tuple(n[1] for n in new)

    hfin = lax.fori_loop(0, tl, scan_step, tuple(h_ref[i] for i in range(2 * nst)), unroll=8)
    for i in range(2 * nst):
        h_ref[i] = hfin[i]

    ys = []
    for m in range(nsl):
        parts = []
        for ri in range(2):
            slab = ri * nst + m // (nsl // nst)
            for jl in range(rows_per_slab):
                j = rows_per_slab * (m % (nsl // nst)) + jl
                parts.append(bu_ref[slab, pl.ds(j, tl, stride=SUBLANES), :])
        hcat = jnp.concatenate(parts, axis=1).astype(BF16)
        um = proj[:, u_off + m * LANES: u_off + (m + 1) * LANES]
        ym = _dot(hcat, cm_ref[m]) + dsk_ref[:, m * LANES:(m + 1) * LANES] * um
        ys.append(_gelu_tanh(ym))
    yg = jnp.concatenate(ys, axis=1)
    gate = _sigmoid(_dot(yg.astype(BF16), wglu_ref[...]) + bglu_ref[...])
    pieces.append((yg * gate).astype(BF16))
    ycat = jnp.concatenate(pieces, axis=1)
    xo_ref[...] = x + _dot(ycat, wout_ref[...])


def _rs_layer(x, st_ret, st_h, consts, w):
    b, L, d = x.shape
    tl = min(TOKEN_TILE, L)
    nl = L // tl
    cos, sin, decay, wq, wkv, gc = consts
    rs_in = w['rs_w_in'].shape[1]
    nst2 = st_h.shape[1]
    kern = functools.partial(_rs_kernel, tl=tl, nl=nl)
    tok = pl.BlockSpec((None, tl, d), lambda i, j: (i, j, 0))
    in_specs = [
        tok, _full((1, d)), _full((d, rs_in)),
        pl.BlockSpec((tl, LANES), lambda i, j: (j, 0)), pl.BlockSpec((tl, LANES), lambda i, j: (j, 0)),
        _full(decay.shape), _full(wq.shape), _full(wkv.shape), _full(gc.shape),
        pl.BlockSpec((None,) + st_ret.shape[1:], lambda i, j: (i, 0, 0, 0)),
        pl.BlockSpec((None,) + st_h.shape[1:], lambda i, j: (i, 0, 0, 0)),
        _full((1, RET_HEADS * RET_DK)), _full(w['s5_a'].shape), _full(w['s5_b'].shape), _full(w['s5_c'].shape),
        _full((1, w['s5_d'].shape[1])), _full(w['s5_w_glu'].shape), _full((1, w['s5_b_glu'].shape[1])),
        _full(w['rs_w_out'].shape),
    ]
    out_specs = [
        tok,
        pl.BlockSpec((None,) + st_ret.shape[1:], lambda i, j: (i, 0, 0, 0)),
        pl.BlockSpec((None,) + st_h.shape[1:], lambda i, j: (i, 0, 0, 0)),
    ]
    return pl.pallas_call(
        kern, grid=(b, nl), in_specs=in_specs, out_specs=out_specs,
        out_shape=(jax.ShapeDtypeStruct(x.shape, F32), jax.ShapeDtypeStruct(st_ret.shape, F32),
                   jax.ShapeDtypeStruct(st_h.shape, F32)),
        scratch_shapes=[pltpu.VMEM((nst2, tl * SUBLANES, LANES), F32)],
        compiler_params=_params("parallel", "arbitrary"), name="rs_mixer",
    )(x, w['norm_mix0'], w['rs_w_in'], cos, sin, decay, wq, wkv, gc, st_ret, st_h, w['ret_gn'],
      w['s5_a'], w['s5_b'], w['s5_c'], w['s5_d'], w['s5_w_glu'], w['s5_b_glu'], w['rs_w_out'])


def _ret_consts(L, tl, pos0):
    half = RET_DK // 2
    pos = pos0 + jnp.arange(L, dtype=jnp.int32)
    inv = ROPE_BASE ** (-jnp.arange(half, dtype=F32) / half)
    ang = pos.astype(F32)[:, None] * inv[None, :]
    cos = jnp.cos(ang)
    sin = jnp.sin(ang)
    cos2 = jnp.concatenate([cos, cos], axis=1)
    sin2 = jnp.concatenate([-sin, sin], axis=1)
    log_g = jnp.log1p(-jnp.exp2(-5.0 - jnp.arange(RET_HEADS, dtype=F32)))
    idx = jnp.arange(tl, dtype=F32)
    diff = idx[:, None] - idx[None, :]
    decay = jnp.where(diff[None] >= 0, jnp.exp(jnp.maximum(diff, 0.0)[None] * log_g[:, None, None]), 0.0)
    w_q = jnp.exp((idx + 1.0)[:, None] * log_g[None, :])
    w_kv = jnp.exp((tl - 1.0 - idx)[:, None] * log_g[None, :])
    g_c = jnp.exp(tl * log_g)
    rep = lambda t: jnp.repeat(t, RET_DK, axis=1)
    gc = jnp.broadcast_to(g_c[:, None, None], (RET_HEADS, 1, RET_DK))
    return cos2, sin2, decay, rep(w_q), rep(w_kv), gc


def _ml_a_kernel(x_ref, gn_ref, win_ref, cs_ref, cw_ref, cb_ref, wq_ref, wk_ref, wv_ref, wg_ref, bg_ref, skip_ref,
                 q_ref, kt_ref, v_ref, gates_ref, sz_ref, sx_ref, nc_ref, xs_ref, *, tl, nl):
    l = pl.program_id(1)
    inner = ML_HEADS * ML_DH
    pad = SUBLANES
    hist = ML_CONV - 1

    @pl.when(l == 0)
    def _():
        xs_ref[0:pad, :] = jnp.zeros((pad, inner), F32)
        xs_ref[pl.ds(pad - hist, hist), :] = cs_ref[...]

    xn = _rms(x_ref[...], gn_ref[...]).astype(BF16)
    xm = _dot(xn, win_ref[:, 0:inner])
    sz_ref[...] = _sigmoid(_dot(xn, win_ref[:, inner:2 * inner])).astype(sz_ref.dtype)
    xs_ref[pl.ds(pad, tl), :] = xm
    acc = cw_ref[hist:hist + 1, :] * xm
    for j in range(hist):
        acc = acc + cw_ref[j:j + 1, :] * xs_ref[pl.ds(pad - hist + j, tl), :]
    xc = acc + cb_ref[...]
    xc = xc * _sigmoid(xc)
    sx_ref[...] = (skip_ref[...] * xc).astype(sx_ref.dtype)
    gates = jnp.zeros((tl, LANES), F32) + bg_ref[...]
    for h in range(ML_HEADS):
        hs = slice(h * ML_DH, (h + 1) * ML_DH)
        xch = xc[:, hs].astype(BF16)
        xmh = xm[:, hs].astype(BF16)
        qh = _dot(xch, wq_ref[h]).astype(BF16)
        kf = _dot(xch, wk_ref[h])
        kh = kf.astype(BF16)
        vh = _dot(xmh, wv_ref[h]).astype(BF16)
        q_ref[:, hs] = qh
        kt_ref[h] = kf.T.astype(BF16)
        v_ref[:, hs] = vh
        gates = gates + _dot(qh, wg_ref[h * ML_DH:(h + 1) * ML_DH, :])
        gates = gates + _dot(kh, wg_ref[inner + h * ML_DH: inner + (h + 1) * ML_DH, :])
        gates = gates + _dot(vh, wg_ref[2 * inner + h * ML_DH: 2 * inner + (h + 1) * ML_DH, :])
    gates_ref[...] = gates

    @pl.when(l == nl - 1)
    def _():
        nc_ref[...] = xs_ref[pl.ds(pad + tl - hist, hist), :]

    xs_ref[0:pad, :] = xs_ref[pl.ds(tl, pad), :]


def _ml_a(x, conv_state, w):
    b, L, d = x.shape
    tl = min(TOKEN_TILE, L)
    nl = L // tl
    inner = ML_HEADS * ML_DH
    kern = functools.partial(_ml_a_kernel, tl=tl, nl=nl)
    tok = lambda width: pl.BlockSpec((None, tl, width), lambda i, j: (i, j, 0))
    kt_spec = pl.BlockSpec((None, ML_HEADS, ML_DH, tl), lambda i, j: (i, 0, 0, j))
    cst = pl.BlockSpec((None, ML_CONV - 1, inner), lambda i, j: (i, 0, 0))
    in_specs = [tok(d), _full((1, d)), _full((d, 2 * inner)), cst, _full((ML_CONV, inner)), _full((1, inner)),
                _full(w['ml_wq'].shape), _full(w['ml_wk'].shape), _full(w['ml_wv'].shape),
                _full(w['ml_w_gates'].shape), _full((1, LANES)), _full((1, inner))]
    out_specs = [tok(inner), kt_spec, tok(inner), tok(LANES), tok(inner), tok(inner), cst]
    sds = jax.ShapeDtypeStruct
    out_shape = (sds((b, L, inner), BF16), sds((b, ML_HEADS, ML_DH, L), BF16), sds((b, L, inner), BF16),
                 sds((b, L, LANES), F32), sds((b, L, inner), BF16), sds((b, L, inner), BF16),
                 sds((b, ML_CONV - 1, inner), F32))
    return pl.pallas_call(
        kern, grid=(b, nl), in_specs=in_specs, out_specs=out_specs, out_shape=out_shape,
        scratch_shapes=[pltpu.VMEM((tl + SUBLANES, inner), F32)],
        compiler_params=_params("parallel", "arbitrary"), name="mlstm_proj",
    )(x, w['norm_mix1'], w['ml_w_in'], conv_state, w['ml_conv_w'], w['ml_conv_b'],
      w['ml_wq'], w['ml_wk'], w['ml_wv'], w['ml_w_gates'], w['ml_b_gates'], w['ml_skip'])


def _split3(x):
    hi = x.astype(BF16)
    r1 = x - hi.astype(F32)
    mid = r1.astype(BF16)
    lo = (r1 - mid.astype(F32)).astype(BF16)
    return hi, mid, lo


def _ml_b_kernel(*refs, tl, nl, has_state):
    if has_state:
        (x_ref, q_ref, kt_ref, v_ref, g_ref, sz_ref, sx_ref, c0_ref, n0_ref, m0_ref, gn_ref, wout_ref,
         xo_ref, c_ref, n_ref, m_ref, ncm_ref) = refs
    else:
        (x_ref, q_ref, kt_ref, v_ref, g_ref, sz_ref, sx_ref, gn_ref, wout_ref,
         xo_ref, c_ref, n_ref, m_ref, ncm_ref) = refs
    l = pl.program_id(1)
    scale = ML_DH ** -0.5

    @pl.when(l == 0)
    def _():
        if has_state:
            c_ref[...] = c0_ref[...]
            m_ref[...] = m0_ref[...]
            for h in range(ML_HEADS):
                ncm_ref[h] = jnp.broadcast_to(n0_ref[h:h + 1, :], (LANES, ML_DH)).T
        else:
            c_ref[...] = jnp.zeros(c_ref.shape, F32)
            m_ref[...] = jnp.zeros(m_ref.shape, F32)
            ncm_ref[...] = jnp.zeros(ncm_ref.shape, F32)

    gates = g_ref[...]
    gates_t = gates.T
    row = lax.broadcasted_iota(jnp.int32, (tl, tl), 0)
    col = lax.broadcasted_iota(jnp.int32, (tl, tl), 1)
    causal = row >= col
    lower = jnp.where(causal, 1.0, 0.0).astype(BF16)
    upper = jnp.where(row <= col, 1.0, 0.0).astype(BF16)
    ones = jnp.ones((tl, LANES), BF16)
    lf_c = _split3(_log_sigmoid(gates))
    b_f = _dot(lower, lf_c[0]) + _dot(lower, lf_c[1]) + _dot(lower, lf_c[2])
    lf_r = _split3(_log_sigmoid(gates_t[0:2 * ML_HEADS, :]))
    b_row_all = _dot(lf_r[0], upper) + _dot(lf_r[1], upper) + _dot(lf_r[2], upper)
    b_c = pltpu.roll(b_f, LANES - ML_HEADS, 1)
    lane = lax.broadcasted_iota(jnp.int32, (1, LANES), 1)
    m_prev_row = jnp.zeros((1, LANES), F32)
    for h in range(ML_HEADS):
        m_prev_row = jnp.where(lane == h, m_ref[h], m_prev_row)
    run = gates - b_c
    trow = lax.broadcasted_iota(jnp.int32, (tl, LANES), 0)
    shift = 1
    while shift < tl:
        run = jnp.maximum(run, jnp.where(trow >= shift, pltpu.roll(run, shift, 0), -jnp.inf))
        shift *= 2
    m_t_all = b_c + jnp.maximum(m_prev_row, run)
    s_inter_all = jnp.exp(b_c + m_prev_row - m_t_all)
    dmin_all = jnp.exp(-m_t_all)
    a_all = b_c - m_t_all + math.log(scale)
    m_new_row = m_t_all[tl - 1:tl, :]
    b_last_row = b_c[tl - 1:tl, :]
    dec_row = jnp.exp(b_last_row + m_prev_row - m_new_row)
    wsc_row = b_last_row - m_new_row

    hsl = [slice(h * ML_DH, (h + 1) * ML_DH) for h in range(ML_HEADS)]

    def state_dots(h):
        q = q_ref[:, hsl[h]]
        return (_dot(q, kt_ref[h]), _dot(q, c_ref[h].astype(BF16)), _dot(q, ncm_ref[h].astype(BF16))[:, 0:1])

    out = x_ref[...]
    nxt = state_dots(0)
    for h in range(ML_HEADS):
        hs = hsl[h]
        sqk, qc, qn = nxt
        v = v_ref[:, hs]
        r_row = gates_t[h:h + 1, :] - b_row_all[ML_HEADS + h: ML_HEADS + h + 1, :]
        s_inter = s_inter_all[:, h:h + 1]
        dec = dec_row[:, h:h + 1]
        wgt = jnp.exp(jnp.where(causal, a_all[:, h:h + 1] + r_row, -jnp.inf))
        qk = sqk * wgt
        num = s_inter * qc + _dot(qk.astype(BF16), v)
        den = s_inter * qn + jnp.sum(qk, axis=-1, keepdims=True)
        if h + 1 < ML_HEADS:
            nxt = state_dots(h + 1)
        dmax = jnp.maximum(jnp.abs(den), dmin_all[:, h:h + 1])
        xg = sz_ref[:, hs].astype(F32) * num
        mu = jnp.mean(xg, axis=-1, keepdims=True)
        var = jnp.mean(xg * xg, axis=-1, keepdims=True) - mu * mu
        y = (xg - mu) * lax.rsqrt(var + EPS * (dmax * dmax)) * gn_ref[:, hs] + sx_ref[:, hs].astype(F32)
        kwt = (kt_ref[h].astype(F32) * (scale * jnp.exp(r_row + wsc_row[:, h:h + 1]))).astype(BF16)
        c_ref[h] = dec * c_ref[h] + _dot(kwt, v)
        ncm_ref[h] = dec * ncm_ref[h] + _dot(kwt, ones)
        m_ref[h] = m_new_row[:, h:h + 1]
        out = out + _dot(y.astype(BF16), wout_ref[hs, :])
    xo_ref[...] = out

    @pl.when(l == nl - 1)
    def _():
        for h in range(ML_HEADS):
            n_ref[h:h + 1, :] = ncm_ref[h].T[0:1, :]


def _ml_b(x, q, kt, v, gates, sz, sx, state, w):
    b, L, d = x.shape
    tl = min(TOKEN_TILE, L)
    nl = L // tl
    inner = ML_HEADS * ML_DH
    has_state = state is not None
    kern = functools.partial(_ml_b_kernel, tl=tl, nl=nl, has_state=has_state)
    tok = lambda width: pl.BlockSpec((None, tl, width), lambda i, j: (i, j, 0))
    kt_spec = pl.BlockSpec((None, ML_HEADS, ML_DH, tl), lambda i, j: (i, 0, 0, j))
    c_spec = pl.BlockSpec((None, ML_HEADS, ML_DH, ML_DH), lambda i, j: (i, 0, 0, 0))
    n_spec = pl.BlockSpec((None, ML_HEADS, ML_DH), lambda i, j: (i, 0, 0))
    m_spec = pl.BlockSpec((None, ML_HEADS, 1, 1), lambda i, j: (i, 0, 0, 0))
    in_specs = [tok(d), tok(inner), kt_spec, tok(inner), tok(LANES), tok(inner), tok(inner)]
    args = [x, q, kt, v, gates, sz, sx]
    if has_state:
        in_specs += [c_spec, n_spec, m_spec]
        args += [state[0], state[1], state[2].reshape(b, ML_HEADS, 1, 1)]
    in_specs += [_full((1, inner)), _full((inner, d))]
    args += [w['ml_gn'], w['ml_w_out']]
    sds = jax.ShapeDtypeStruct
    out_shape = (sds(x.shape, F32), sds((b, ML_HEADS, ML_DH, ML_DH), F32), sds((b, ML_HEADS, ML_DH), F32),
                 sds((b, ML_HEADS, 1, 1), F32))
    xo, c, n, m = pl.pallas_call(
        kern, grid=(b, nl), in_specs=in_specs, out_specs=[tok(d), c_spec, n_spec, m_spec], out_shape=out_shape,
        scratch_shapes=[pltpu.VMEM((ML_HEADS, ML_DH, LANES), F32)],
        compiler_params=_params("parallel", "arbitrary"), name="mlstm_cell",
    )(*args)
    return xo, c, n, m.reshape(b, ML_HEADS)


def _memkv_kernel(m_ref, g_ref, wk_ref, wv_ref, k_ref, v_ref):
    mn = _rms(m_ref[...], g_ref[...]).astype(BF16)
    k_ref[...] = _dot(mn, wk_ref[...])
    v_ref[...] = _dot(mn, wv_ref[...])


def _memkv(mem, norm_mem, wk, wv):
    b, m_len, d = mem.shape
    depth = wk.shape[0]
    rows = b * m_len
    tm = min(MEM_ROW_TILE, rows)
    mem2 = mem.reshape(rows, d)
    wspec = pl.BlockSpec((None, d, d), lambda i, j: (i, 0, 0))
    ospec = pl.BlockSpec((None, tm, d), lambda i, j: (i, j, 0))
    k, v = pl.pallas_call(
        _memkv_kernel, grid=(depth, rows // tm),
        in_specs=[pl.BlockSpec((tm, d), lambda i, j: (j, 0)), pl.BlockSpec((None, 1, d), lambda i, j: (i, 0, 0)),
                  wspec, wspec],
        out_specs=[ospec, ospec],
        out_shape=(jax.ShapeDtypeStruct((depth, rows, d), F32), jax.ShapeDtypeStruct((depth, rows, d), F32)),
        compiler_params=_params("parallel", "parallel"), name="memory_kv",
    )(mem2, norm_mem.reshape(depth, 1, d), wk, wv)
    return k.reshape(depth, b, m_len, d), v.reshape(depth, b, m_len, d)


def _xattn_kernel(x_ref, g_ref, wq_ref, mk_ref, mv_ref, wo_ref, o_ref):
    x = x_ref[...]
    xn = _rms(x, g_ref[...]).astype(BF16)
    q = _dot(xn, wq_ref[...])
    hsl = [slice(h * X_DH, (h + 1) * X_DH) for h in range(X_HEADS)]

    def scores(h):
        return _dot_nt(q[:, hsl[h]].astype(BF16), mk_ref[:, hsl[h]].astype(BF16))

    out = x
    nxt = scores(0)
    for h in range(X_HEADS):
        s = nxt * (X_DH ** -0.5)
        if h + 1 < X_HEADS:
            nxt = scores(h + 1)
        e = jnp.exp(s - jnp.max(s, axis=-1, keepdims=True))
        pv = _dot(e.astype(BF16), mv_ref[:, hsl[h]].astype(BF16))
        oh = pv * (1.0 / jnp.sum(e, axis=-1, keepdims=True))
        out = out + _dot(oh.astype(BF16), wo_ref[hsl[h], :])
    o_ref[...] = out


def _xattn(x, layer, g, wq, mk, mv, wo):
    b, L, d = x.shape
    tl = min(XATTN_TILE, L)
    m_len = mk.shape[2]
    tok = pl.BlockSpec((None, tl, d), lambda i, j: (i, j, 0))
    mem = pl.BlockSpec((None, None, m_len, d), lambda i, j: (layer, i, 0, 0))
    wsp = pl.BlockSpec((None, d, d), lambda i, j: (layer, 0, 0))
    gsp = pl.BlockSpec((None, 1, d), lambda i, j: (layer, 0, 0))
    return pl.pallas_call(
        _xattn_kernel, grid=(b, L // tl),
        in_specs=[tok, gsp, wsp, mem, mem, wsp],
        out_specs=tok, out_shape=jax.ShapeDtypeStruct(x.shape, F32),
        compiler_params=_params("parallel", "arbitrary"), name="cross_attn",
    )(x, g, wq, mk, mv, wo)


def _mlp_kernel(*refs, final):
    if final:
        x_ref, g_ref, wu_ref, wd_ref, gf_ref, o_ref, acc_ref, xn_ref = refs
    else:
        x_ref, g_ref, wu_ref, wd_ref, o_ref, acc_ref, xn_ref = refs
    j = pl.program_id(1)

    @pl.when(j == 0)
    def _():
        xn_ref[...] = _rms(x_ref[...], g_ref[...]).astype(BF16)
        acc_ref[...] = jnp.zeros(acc_ref.shape, F32)

    hdn = jnp.maximum(_dot(xn_ref[...], wu_ref[...]), 0.0)
    acc_ref[...] += _dot((hdn * hdn).astype(BF16), wd_ref[...])

    @pl.when(j == pl.num_programs(1) - 1)
    def _():
        out = x_ref[...] + acc_ref[...]
        if final:
            out = _rms(out, gf_ref[...])
        o_ref[...] = out


def _mlp(x, layer, g, wu, wd, gf=None):
    shape = x.shape
    d = shape[-1]
    x2 = x.reshape(-1, d)
    rows = x2.shape[0]
    ff = wu.shape[2]
    tm = min(MLP_ROW_TILE, rows)
    tf = min(MLP_FF_TILE, ff)
    final = gf is not None
    in_specs = [pl.BlockSpec((tm, d), lambda i, j: (i, 0)), pl.BlockSpec((None, 1, d), lambda i, j: (layer, 0, 0)),
                pl.BlockSpec((None, d, tf), lambda i, j: (layer, 0, j)),
                pl.BlockSpec((None, tf, d), lambda i, j: (layer, j, 0))]
    args = [x2, g, wu, wd]
    if final:
        in_specs.append(_full((1, d)))
        args.append(gf)
    out = pl.pallas_call(
        functools.partial(_mlp_kernel, final=final), grid=(rows // tm, ff // tf),
        in_specs=in_specs, out_specs=pl.BlockSpec((tm, d), lambda i, j: (i, 0)),
        out_shape=jax.ShapeDtypeStruct(x2.shape, F32),
        scratch_shapes=[pltpu.VMEM((tm, d), F32), pltpu.VMEM((tm, d), BF16)],
        compiler_params=_params("parallel", "arbitrary"), name="mlp",
    )(*args)
    return out.reshape(shape)


def _trunk(x, pos0, st_ret, st_re, st_im, ml_state, conv_state, mem_k, mem_v, w):
    b, L, d = x.shape
    tl = min(TOKEN_TILE, L)
    nst = S5_GROUPS * S5_STATE // (SUBLANES * LANES)
    st_h = jnp.concatenate([st_re.reshape(b, nst, SUBLANES, LANES), st_im.reshape(b, nst, SUBLANES, LANES)], axis=1)
    x, s_new, h_new = _rs_layer(x, st_ret, st_h, _ret_consts(L, tl, pos0), w)
    hr = h_new[:, :nst].reshape(b, S5_GROUPS, S5_STATE)
    hi = h_new[:, nst:].reshape(b, S5_GROUPS, S5_STATE)
    x = _xattn(x, 0, w['norm_cross'], w['x_wq'], mem_k, mem_v, w['x_wo'])
    x = _mlp(x, 0, w['norm_mlp'], w['mlp_w_up'], w['mlp_w_down'])
    q, kt, v, gates, sz, sx, new_conv = _ml_a(x, conv_state, w)
    x, cf, nf, mf = _ml_b(x, q, kt, v, gates, sz, sx, ml_state, w)
    x = _xattn(x, 1, w['norm_cross'], w['x_wq'], mem_k, mem_v, w['x_wo'])
    y = _mlp(x, 1, w['norm_mlp'], w['mlp_w_up'], w['mlp_w_down'], w['norm_final'])
    return y, s_new[None], hr[None], hi[None], cf[None], nf[None], mf[None], new_conv[None]


def kernel(x_prompt, x_sample, mem_prompt, state_ret, state_s5_re, state_s5_im, state_mlstm_c, state_mlstm_n, state_mlstm_m, cache_mlstm_conv, cache_mem_k, cache_mem_v, norm_mix, norm_cross, norm_mem, norm_mlp, norm_final, rs_w_in, rs_w_out, ret_gn, s5_a_re, s5_a_im, s5_log_dt, s5_b_re, s5_b_im, s5_c_re, s5_c_im, s5_d, s5_w_glu, s5_b_glu, ml_w_in, ml_conv_w, ml_conv_b, ml_wq, ml_wk, ml_wv, ml_w_gates, ml_b_gates, ml_gn, ml_skip, ml_w_out, x_wq, x_wk, x_wv, x_wo, mlp_w_up, mlp_w_down):
    d = x_prompt.shape[-1]
    bp = x_prompt.shape[0]
    bf = lambda t: t.astype(BF16)
    row = lambda t: t.reshape(1, -1).astype(F32)
    ab_re, ab_im, bb_re, bb_im = _s5_prep(s5_a_re[0], s5_a_im[0], s5_log_dt[0], s5_b_re[0], s5_b_im[0])
    s5_a, s5_b, s5_c = _s5_matrices(ab_re, ab_im, bb_re, bb_im, s5_c_re[0], s5_c_im[0])
    n_gate = ml_w_gates.shape[-1]
    w = {
        'norm_mix0': row(norm_mix[0]), 'norm_mix1': row(norm_mix[1]),
        'norm_cross': norm_cross.reshape(-1, 1, d), 'norm_mlp': norm_mlp.reshape(-1, 1, d),
        'norm_final': row(norm_final),
        'rs_w_in': bf(rs_w_in[0]), 'rs_w_out': bf(rs_w_out[0]), 'ret_gn': row(ret_gn[0]),
        's5_a': s5_a, 's5_b': s5_b, 's5_c': s5_c, 's5_d': row(s5_d[0]), 's5_w_glu': bf(s5_w_glu[0]),
        's5_b_glu': row(s5_b_glu[0]),
        'ml_w_in': bf(ml_w_in[0]), 'ml_conv_w': ml_conv_w[0], 'ml_conv_b': row(ml_conv_b[0]),
        'ml_wq': bf(ml_wq[0]), 'ml_wk': bf(ml_wk[0]), 'ml_wv': bf(ml_wv[0]),
        'ml_w_gates': bf(jnp.pad(ml_w_gates[0], ((0, 0), (0, LANES - n_gate)))),
        'ml_b_gates': jnp.pad(ml_b_gates[0], (0, LANES - n_gate)).reshape(1, LANES),
        'ml_gn': row(ml_gn[0]), 'ml_skip': row(ml_skip[0]), 'ml_w_out': bf(ml_w_out[0]),
        'x_wq': bf(x_wq), 'x_wo': bf(x_wo), 'mlp_w_up': bf(mlp_w_up), 'mlp_w_down': bf(mlp_w_down),
    }
    mem_len = mem_prompt.shape[1]
    mk_p, mv_p = _memkv(mem_prompt, norm_mem, bf(x_wk), bf(x_wv))
    zeros = lambda *s: jnp.zeros(s, F32)
    out_p = _trunk(x_prompt, 0, zeros(bp, RET_HEADS, RET_DK, RET_DK), zeros(bp, S5_GROUPS, S5_STATE),
                   zeros(bp, S5_GROUPS, S5_STATE), None, zeros(bp, ML_CONV - 1, ML_HEADS * ML_DH), mk_p, mv_p, w)
    bs = x_sample.shape[0]
    out_s = _trunk(x_sample, PAST_LEN, state_ret[0], state_s5_re[0], state_s5_im[0],
                   (state_mlstm_c[0], state_mlstm_n[0], state_mlstm_m[0]), cache_mlstm_conv[0],
                   cache_mem_k.reshape(-1, bs, mem_len, d), cache_mem_v.reshape(-1, bs, mem_len, d), w)
    mem_k_p = mk_p.reshape(-1, bp, mem_len, X_HEADS, X_DH)
    mem_v_p = mv_p.reshape(-1, bp, mem_len, X_HEADS, X_DH)
    return (out_p[0], out_s[0]) + tuple(out_p[1:]) + (mem_k_p, mem_v_p) + tuple(out_s[1:])
```

```python
import functools
import math

import jax
import jax.numpy as jnp
from jax import lax
from jax.experimental import pallas as pl
from jax.experimental.pallas import tpu as pltpu

F32 = jnp.float32
BF16 = jnp.bfloat16

EPS = 1e-6
ROPE_BASE = 10000.0
PAST_LEN = 4096
RET_HEADS = 4
RET_DK = 128
S5_GROUPS = 32
S5_GROUP = 16
S5_STATE = 64
ML_HEADS = 4
ML_DH = 512
ML_CONV = 4
X_HEADS = 4
X_DH = 256
LANES = 128
SUBLANES = 8
VMEM_LIMIT = 56 * 1024 * 1024
TOKEN_TILE = 256
XATTN_TILE = 512
MLP_ROW_TILE = 1024
MLP_FF_TILE = 1024
MEM_ROW_TILE = 1024


def _dot(a, b):
    return jnp.dot(a, b, preferred_element_type=F32)


def _dot_nt(a, b):
    return lax.dot_general(a, b, (((1,), (1,)), ((), ())), preferred_element_type=F32)


def _dot_tn(a, b):
    return lax.dot_general(a, b, (((0,), (0,)), ((), ())), preferred_element_type=F32)


def _rms(x, g):
    y = x * lax.rsqrt(jnp.mean(x * x, axis=-1, keepdims=True) + EPS)
    return y * g


def _layernorm(x):
    mu = jnp.mean(x, axis=-1, keepdims=True)
    xc = x - mu
    var = jnp.mean(xc * xc, axis=-1, keepdims=True)
    return xc * lax.rsqrt(var + EPS)


def _sigmoid(x):
    return 1.0 / (1.0 + jnp.exp(-x))


def _log_sigmoid(x):
    return -(jnp.maximum(-x, 0.0) + jnp.log(1.0 + jnp.exp(-jnp.abs(x))))


def _gelu_tanh(x):
    c = math.sqrt(2.0 / math.pi)
    return 0.5 * x * (1.0 + jnp.tanh(c * (x + 0.044715 * (x * x * x))))


def _params(*sem):
    return pltpu.CompilerParams(dimension_semantics=sem, vmem_limit_bytes=VMEM_LIMIT)


def _full(shape):
    n = len(shape)
    return pl.BlockSpec(shape, lambda *_: (0,) * n)


def _s5_prep_kernel(are_ref, aim_ref, ldt_ref, bre_ref, bim_ref, abre_ref, abim_ref, bbre_ref, bbim_ref):
    a_re = are_ref[...]
    a_im = aim_ref[...]
    dt = jnp.exp(ldt_ref[...])
    mag = jnp.exp(a_re * dt)
    ab_re = mag * jnp.cos(a_im * dt)
    ab_im = mag * jnp.sin(a_im * dt)
    den = a_re * a_re + a_im * a_im
    x_re = ab_re - 1.0
    f_re = (x_re * a_re + ab_im * a_im) / den
    f_im = (ab_im * a_re - x_re * a_im) / den
    b_re = bre_ref[...]
    b_im = bim_ref[...]
    abre_ref[...] = ab_re
    abim_ref[...] = ab_im
    bbre_ref[...] = f_re * b_re - f_im * b_im
    bbim_ref[...] = f_re * b_im + f_im * b_re


def _s5_prep(a_re, a_im, log_dt, b_re, b_im):
    g, p, c = b_re.shape
    out = pl.pallas_call(
        _s5_prep_kernel,
        out_shape=(jax.ShapeDtypeStruct((g, 1, p), F32), jax.ShapeDtypeStruct((g, 1, p), F32),
                   jax.ShapeDtypeStruct((g, c, p), F32), jax.ShapeDtypeStruct((g, c, p), F32)),
        name="s5_prep",
    )(a_re.reshape(g, 1, p), a_im.reshape(g, 1, p), log_dt.reshape(g, 1, 1),
      jnp.swapaxes(b_re, 1, 2), jnp.swapaxes(b_im, 1, 2))
    return out


def _s5_matrices(ab_re, ab_im, bb_re, bb_im, c_re, c_im):
    g, c, p = bb_re.shape
    nsl = g * c // LANES
    gs = LANES // c
    eye = jnp.eye(gs, dtype=F32)

    def bmat(bb):
        t = bb.reshape(nsl, gs, c, p)
        w = jnp.einsum('mgcp,hg->mhcgp', t, eye)
        return w.reshape(nsl, gs * c, gs * p)

    def cmat(cc):
        t = cc.reshape(nsl, gs, c, p)
        w = jnp.einsum('mgcp,hg->mhpgc', t, eye)
        return w.reshape(nsl, gs * p, gs * c)

    b_all = jnp.stack([bmat(bb_re), bmat(bb_im)], axis=1).reshape(nsl * 2, gs * c, gs * p)
    c_all = jnp.concatenate([cmat(c_re), -cmat(c_im)], axis=1)
    nst = g * p // (SUBLANES * LANES)
    a_all = jnp.concatenate([ab_re.reshape(nst, SUBLANES, LANES), ab_im.reshape(nst, SUBLANES, LANES)], axis=0)
    return a_all, b_all.astype(BF16), c_all.astype(BF16)


def _rs_kernel(x_ref, gn_ref, win_ref, cos_ref, sin_ref, decay_ref, wq_ref, wkv_ref, gc_ref,
               s0_ref, h0_ref, rgn_ref, a_ref, bm_ref, cm_ref, dsk_ref, wglu_ref, bglu_ref, wout_ref,
               xo_ref, s_ref, h_ref, bu_ref, *, tl, nl):
    l = pl.program_id(1)
    qk_w = RET_HEADS * RET_DK
    nsl = S5_GROUPS * S5_GROUP // LANES
    nst = S5_GROUPS * S5_STATE // (SUBLANES * LANES)
    rows_per_slab = SUBLANES // (nsl // nst)

    @pl.when(l == 0)
    def _():
        s_ref[...] = s0_ref[...]
        h_ref[...] = h0_ref[...]

    x = x_ref[...]
    xn = _rms(x, gn_ref[...]).astype(BF16)
    proj = _dot(xn, win_ref[...])
    cos = cos_ref[...]
    sin = sin_ref[...]

    u_off = 4 * qk_w
    for m in range(nsl):
        um = proj[:, u_off + m * LANES: u_off + (m + 1) * LANES].astype(BF16)
        for ri in range(2):
            r = _dot(um, bm_ref[2 * m + ri])
            slab = ri * nst + m // (nsl // nst)
            for jl in range(rows_per_slab):
                j = rows_per_slab * (m % (nsl // nst)) + jl
                bu_ref[slab, pl.ds(j, tl, stride=SUBLANES), :] = r[:, jl * LANES:(jl + 1) * LANES]

    a = [a_ref[i] for i in range(2 * nst)]

    def scan_steps(carry, t0, t1):
        for t in range(t0, t1):
            rows = pl.ds(t * SUBLANES, SUBLANES)
            new = []
            for s in range(nst):
                hr, hi = carry[s], carry[nst + s]
                ar, ai = a[s], a[nst + s]
                nr = ar * hr - ai * hi + bu_ref[s, rows, :]
                ni = ar * hi + ai * hr + bu_ref[nst + s, rows, :]
                bu_ref[s, rows, :] = nr
                bu_ref[nst + s, rows, :] = ni
                new.append((nr, ni))
            carry = tuple(n[0] for n in new) + tuple(n[1] for n in new)
        return carry

    def rope_head(h):
        q = proj[:, h * RET_DK:(h + 1) * RET_DK]
        k = proj[:, qk_w + h * RET_DK: qk_w + (h + 1) * RET_DK]
        qr = q * cos + pltpu.roll(q, RET_DK // 2, 1) * sin
        kr = (k * cos + pltpu.roll(k, RET_DK // 2, 1) * sin) * (RET_DK ** -0.5)
        qb = qr.astype(BF16)
        return qb, kr, _dot_nt(qb, kr.astype(BF16)), _dot(qb, s_ref[h].astype(BF16))

    carry = tuple(h_ref[i] for i in range(2 * nst))
    per_head = tl // RET_HEADS
    pieces = []
    nxt = rope_head(0)
    for h in range(RET_HEADS):
        hs = slice(h * RET_DK, (h + 1) * RET_DK)
        qb, kr, sqk, cross = nxt
        vb = proj[:, 2 * qk_w + h * RET_DK: 2 * qk_w + (h + 1) * RET_DK].astype(BF16)
        g = proj[:, 3 * qk_w + h * RET_DK: 3 * qk_w + (h + 1) * RET_DK]
        o = _dot((sqk * decay_ref[h]).astype(BF16), vb) + cross * wq_ref[:, hs]
        if h + 1 < RET_HEADS:
            nxt = rope_head(h + 1)
        carry = scan_steps(carry, h * per_head, (h + 1) * per_head)
        s_ref[h] = gc_ref[h] * s_ref[h] + _dot_tn((kr * wkv_ref[:, hs]).astype(BF16), vb)
        mu = jnp.mean(o, axis=-1, keepdims=True)
        var = jnp.mean(o * o, axis=-1, keepdims=True) - mu * mu
        y = (o - mu) * lax.rsqrt(var + EPS) * rgn_ref[:, hs]
        pieces.append((g * _sigmoid(g) * y).astype(BF16))
    carry = scan_steps(carry, RET_HEADS * per_head, tl)
    for i in range(2 * nst):
        h_ref[i] = carry[i]
    out = x + _dot(jnp.concatenate(pieces, axis=1), wout_ref[0:qk_w, :])

    ys = []
    for m in range(nsl):
        parts = []
        for ri in range(2):
            slab = ri * nst + m // (nsl // nst)
            for jl in range(rows_per_slab):
                j = rows_per_slab * (m % (nsl // nst)) + jl
                parts.append(bu_ref[slab, pl.ds(j, tl, stride=SUBLANES), :])
        hcat = jnp.concatenate(parts, axis=1).astype(BF16)
        um = proj[:, u_off + m * LANES: u_off + (m + 1) * LANES]
        ym = _dot(hcat, cm_ref[m]) + dsk_ref[:, m * LANES:(m + 1) * LANES] * um
        ys.append(_gelu_tanh(ym))
    yg = jnp.concatenate(ys, axis=1)
    gate = _sigmoid(_dot(yg.astype(BF16), wglu_ref[...]) + bglu_ref[...])
    xo_ref[...] = out + _dot((yg * gate).astype(BF16), wout_ref[qk_w:, :])


def _rs_layer(x, st_ret, st_h, consts, w):
    b, L, d = x.shape
    tl = min(TOKEN_TILE, L)
    nl = L // tl
    cos, sin, decay, wq, wkv, gc = consts
    rs_in = w['rs_w_in'].shape[1]
    nst2 = st_h.shape[1]
    kern = functools.partial(_rs_kernel, tl=tl, nl=nl)
    tok = pl.BlockSpec((None, tl, d), lambda i, j: (i, j, 0))
    in_specs = [
        tok, _full((1, d)), _full((d, rs_in)),
        pl.BlockSpec((tl, LANES), lambda i, j: (j, 0)), pl.BlockSpec((tl, LANES), lambda i, j: (j, 0)),
        _full(decay.shape), _full(wq.shape), _full(wkv.shape), _full(gc.shape),
        pl.BlockSpec((None,) + st_ret.shape[1:], lambda i, j: (i, 0, 0, 0)),
        pl.BlockSpec((None,) + st_h.shape[1:], lambda i, j: (i, 0, 0, 0)),
        _full((1, RET_HEADS * RET_DK)), _full(w['s5_a'].shape), _full(w['s5_b'].shape), _full(w['s5_c'].shape),
        _full((1, w['s5_d'].shape[1])), _full(w['s5_w_glu'].shape), _full((1, w['s5_b_glu'].shape[1])),
        _full(w['rs_w_out'].shape),
    ]
    out_specs = [
        tok,
        pl.BlockSpec((None,) + st_ret.shape[1:], lambda i, j: (i, 0, 0, 0)),
        pl.BlockSpec((None,) + st_h.shape[1:], lambda i, j: (i, 0, 0, 0)),
    ]
    return pl.pallas_call(
        kern, grid=(b, nl), in_specs=in_specs, out_specs=out_specs,
        out_shape=(jax.ShapeDtypeStruct(x.shape, F32), jax.ShapeDtypeStruct(st_ret.shape, F32),
                   jax.ShapeDtypeStruct(st_h.shape, F32)),
        scratch_shapes=[pltpu.VMEM((nst2, tl * SUBLANES, LANES), F32)],
        compiler_params=_params("parallel", "arbitrary"), name="rs_mixer",
    )(x, w['norm_mix0'], w['rs_w_in'], cos, sin, decay, wq, wkv, gc, st_ret, st_h, w['ret_gn'],
      w['s5_a'], w['s5_b'], w['s5_c'], w['s5_d'], w['s5_w_glu'], w['s5_b_glu'], w['rs_w_out'])


def _ret_consts(L, tl, pos0):
    half = RET_DK // 2
    pos = pos0 + jnp.arange(L, dtype=jnp.int32)
    inv = ROPE_BASE ** (-jnp.arange(half, dtype=F32) / half)
    ang = pos.astype(F32)[:, None] * inv[None, :]
    cos = jnp.cos(ang)
    sin = jnp.sin(ang)
    cos2 = jnp.concatenate([cos, cos], axis=1)
    sin2 = jnp.concatenate([-sin, sin], axis=1)
    log_g = jnp.log1p(-jnp.exp2(-5.0 - jnp.arange(RET_HEADS, dtype=F32)))
    idx = jnp.arange(tl, dtype=F32)
    diff = idx[:, None] - idx[None, :]
    decay = jnp.where(diff[None] >= 0, jnp.exp(jnp.maximum(diff, 0.0)[None] * log_g[:, None, None]), 0.0)
    w_q = jnp.exp((idx + 1.0)[:, None] * log_g[None, :])
    w_kv = jnp.exp((tl - 1.0 - idx)[:, None] * log_g[None, :])
    g_c = jnp.exp(tl * log_g)
    rep = lambda t: jnp.repeat(t, RET_DK, axis=1)
    gc = jnp.broadcast_to(g_c[:, None, None], (RET_HEADS, 1, RET_DK))
    return cos2, sin2, decay, rep(w_q), rep(w_kv), gc


def _ml_a_kernel(x_ref, gn_ref, win_ref, cs_ref, cw_ref, cb_ref, wq_ref, wk_ref, wv_ref, wg_ref, bg_ref, skip_ref,
                 q_ref, kt_ref, v_ref, gates_ref, sz_ref, sx_ref, nc_ref, xs_ref, *, tl, nl):
    l = pl.program_id(1)
    inner = ML_HEADS * ML_DH
    pad = SUBLANES
    hist = ML_CONV - 1

    @pl.when(l == 0)
    def _():
        xs_ref[0:pad, :] = jnp.zeros((pad, inner), F32)
        xs_ref[pl.ds(pad - hist, hist), :] = cs_ref[...]

    xn = _rms(x_ref[...], gn_ref[...]).astype(BF16)
    xm = _dot(xn, win_ref[:, 0:inner])
    sz_ref[...] = _sigmoid(_dot(xn, win_ref[:, inner:2 * inner])).astype(sz_ref.dtype)
    xs_ref[pl.ds(pad, tl), :] = xm
    acc = cw_ref[hist:hist + 1, :] * xm
    for j in range(hist):
        acc = acc + cw_ref[j:j + 1, :] * xs_ref[pl.ds(pad - hist + j, tl), :]
    xc = acc + cb_ref[...]
    xc = xc * _sigmoid(xc)
    sx_ref[...] = (skip_ref[...] * xc).astype(sx_ref.dtype)
    gates = jnp.zeros((tl, LANES), F32) + bg_ref[...]
    for h in range(ML_HEADS):
        hs = slice(h * ML_DH, (h + 1) * ML_DH)
        xch = xc[:, hs].astype(BF16)
        xmh = xm[:, hs].astype(BF16)
        qh = _dot(xch, wq_ref[h]).astype(BF16)
        kf = _dot(xch, wk_ref[h])
        kh = kf.astype(BF16)
        vh = _dot(xmh, wv_ref[h]).astype(BF16)
        q_ref[:, hs] = qh
        kt_ref[h] = kf.T.astype(BF16)
        v_ref[:, hs] = vh
        gates = gates + _dot(qh, wg_ref[h * ML_DH:(h + 1) * ML_DH, :])
        gates = gates + _dot(kh, wg_ref[inner + h * ML_DH: inner + (h + 1) * ML_DH, :])
        gates = gates + _dot(vh, wg_ref[2 * inner + h * ML_DH: 2 * inner + (h + 1) * ML_DH, :])
    gates_ref[...] = gates

    @pl.when(l == nl - 1)
    def _():
        nc_ref[...] = xs_ref[pl.ds(pad + tl - hist, hist), :]

    xs_ref[0:pad, :] = xs_ref[pl.ds(tl, pad), :]


def _ml_a(x, conv_state, w):
    b, L, d = x.shape
    tl = min(TOKEN_TILE, L)
    nl = L // tl
    inner = ML_HEADS * ML_DH
    kern = functools.partial(_ml_a_kernel, tl=tl, nl=nl)
    tok = lambda width: pl.BlockSpec((None, tl, width), lambda i, j: (i, j, 0))
    kt_spec = pl.BlockSpec((None, ML_HEADS, ML_DH, tl), lambda i, j: (i, 0, 0, j))
    cst = pl.BlockSpec((None, ML_CONV - 1, inner), lambda i, j: (i, 0, 0))
    in_specs = [tok(d), _full((1, d)), _full((d, 2 * inner)), cst, _full((ML_CONV, inner)), _full((1, inner)),
                _full(w['ml_wq'].shape), _full(w['ml_wk'].shape), _full(w['ml_wv'].shape),
                _full(w['ml_w_gates'].shape), _full((1, LANES)), _full((1, inner))]
    out_specs = [tok(inner), kt_spec, tok(inner), tok(LANES), tok(inner), tok(inner), cst]
    sds = jax.ShapeDtypeStruct
    out_shape = (sds((b, L, inner), BF16), sds((b, ML_HEADS, ML_DH, L), BF16), sds((b, L, inner), BF16),
                 sds((b, L, LANES), F32), sds((b, L, inner), BF16), sds((b, L, inner), BF16),
                 sds((b, ML_CONV - 1, inner), F32))
    return pl.pallas_call(
        kern, grid=(b, nl), in_specs=in_specs, out_specs=out_specs, out_shape=out_shape,
        scratch_shapes=[pltpu.VMEM((tl + SUBLANES, inner), F32)],
        compiler_params=_params("parallel", "arbitrary"), name="mlstm_proj",
    )(x, w['norm_mix1'], w['ml_w_in'], conv_state, w['ml_conv_w'], w['ml_conv_b'],
      w['ml_wq'], w['ml_wk'], w['ml_wv'], w['ml_w_gates'], w['ml_b_gates'], w['ml_skip'])


def _split3(x):
    hi = x.astype(BF16)
    r1 = x - hi.astype(F32)
    mid = r1.astype(BF16)
    lo = (r1 - mid.astype(F32)).astype(BF16)
    return hi, mid, lo


def _ml_b_kernel(*refs, tl, nl, has_state):
    if has_state:
        (x_ref, q_ref, kt_ref, v_ref, g_ref, sz_ref, sx_ref, c0_ref, n0_ref, m0_ref, gn_ref, wout_ref,
         xo_ref, c_ref, n_ref, m_ref, ncm_ref) = refs
    else:
        (x_ref, q_ref, kt_ref, v_ref, g_ref, sz_ref, sx_ref, gn_ref, wout_ref,
         xo_ref, c_ref, n_ref, m_ref, ncm_ref) = refs
    l = pl.program_id(1)
    scale = ML_DH ** -0.5

    @pl.when(l == 0)
    def _():
        if has_state:
            c_ref[...] = c0_ref[...]
            m_ref[...] = m0_ref[...]
            for h in range(ML_HEADS):
                ncm_ref[h] = jnp.broadcast_to(n0_ref[h:h + 1, :], (LANES, ML_DH)).T
        else:
            c_ref[...] = jnp.zeros(c_ref.shape, F32)
            m_ref[...] = jnp.zeros(m_ref.shape, F32)
            ncm_ref[...] = jnp.zeros(ncm_ref.shape, F32)

    gates = g_ref[...]
    gates_t = gates.T
    row = lax.broadcasted_iota(jnp.int32, (tl, tl), 0)
    col = lax.broadcasted_iota(jnp.int32, (tl, tl), 1)
    causal = row >= col
    lower = jnp.where(causal, 1.0, 0.0).astype(BF16)
    upper = jnp.where(row <= col, 1.0, 0.0).astype(BF16)
    ones = jnp.ones((tl, LANES), BF16)
    lf_c = _split3(_log_sigmoid(gates))
    b_f = _dot(lower, lf_c[0]) + _dot(lower, lf_c[1]) + _dot(lower, lf_c[2])
    lf_r = _split3(_log_sigmoid(gates_t[0:2 * ML_HEADS, :]))
    b_row_all = _dot(lf_r[0], upper) + _dot(lf_r[1], upper) + _dot(lf_r[2], upper)
    b_c = pltpu.roll(b_f, LANES - ML_HEADS, 1)
    lane = lax.broadcasted_iota(jnp.int32, (1, LANES), 1)
    m_prev_row = jnp.zeros((1, LANES), F32)
    for h in range(ML_HEADS):
        m_prev_row = jnp.where(lane == h, m_ref[h], m_prev_row)
    run = gates - b_c
    trow = lax.broadcasted_iota(jnp.int32, (tl, LANES), 0)
    shift = 1
    while shift < tl:
        run = jnp.maximum(run, jnp.where(trow >= shift, pltpu.roll(run, shift, 0), -jnp.inf))
        shift *= 2
    m_t_all = b_c + jnp.maximum(m_prev_row, run)
    s_inter_all = jnp.exp(b_c + m_prev_row - m_t_all)
    dmin_all = jnp.exp(-m_t_all)
    a_all = b_c - m_t_all + math.log(scale)
    m_new_row = m_t_all[tl - 1:tl, :]
    b_last_row = b_c[tl - 1:tl, :]
    dec_row = jnp.exp(b_last_row + m_prev_row - m_new_row)
    wsc_row = b_last_row - m_new_row

    hsl = [slice(h * ML_DH, (h + 1) * ML_DH) for h in range(ML_HEADS)]

    def state_dots(h):
        q = q_ref[:, hsl[h]]
        return (_dot(q, kt_ref[h]), _dot(q, c_ref[h].astype(BF16)), _dot(q, ncm_ref[h].astype(BF16))[:, 0:1])

    out = x_ref[...]
    nxt = state_dots(0)
    for h in range(ML_HEADS):
        hs = hsl[h]
        sqk, qc, qn = nxt
        v = v_ref[:, hs]
        r_row = gates_t[h:h + 1, :] - b_row_all[ML_HEADS + h: ML_HEADS + h + 1, :]
        s_inter = s_inter_all[:, h:h + 1]
        dec = dec_row[:, h:h + 1]
        wgt = jnp.exp(jnp.where(causal, a_all[:, h:h + 1] + r_row, -jnp.inf))
        qk = sqk * wgt
        num = s_inter * qc + _dot(qk.astype(BF16), v)
        den = s_inter * qn + jnp.sum(qk, axis=-1, keepdims=True)
        if h + 1 < ML_HEADS:
            nxt = state_dots(h + 1)
        dmax = jnp.maximum(jnp.abs(den), dmin_all[:, h:h + 1])
        xg = sz_ref[:, hs].astype(F32) * num
        mu = jnp.mean(xg, axis=-1, keepdims=True)
        var = jnp.mean(xg * xg, axis=-1, keepdims=True) - mu * mu
        y = (xg - mu) * lax.rsqrt(var + EPS * (dmax * dmax)) * gn_ref[:, hs] + sx_ref[:, hs].astype(F32)
        kwt = (kt_ref[h].astype(F32) * (scale * jnp.exp(r_row + wsc_row[:, h:h + 1]))).astype(BF16)
        c_ref[h] = dec * c_ref[h] + _dot(kwt, v)
        ncm_ref[h] = dec * ncm_ref[h] + _dot(kwt, ones)
        m_ref[h] = m_new_row[:, h:h + 1]
        out = out + _dot(y.astype(BF16), wout_ref[hs, :])
    xo_ref[...] = out

    @pl.when(l == nl - 1)
    def _():
        for h in range(ML_HEADS):
            n_ref[h:h + 1, :] = ncm_ref[h].T[0:1, :]


def _ml_b(x, q, kt, v, gates, sz, sx, state, w):
    b, L, d = x.shape
    tl = min(TOKEN_TILE, L)
    nl = L // tl
    inner = ML_HEADS * ML_DH
    has_state = state is not None
    kern = functools.partial(_ml_b_kernel, tl=tl, nl=nl, has_state=has_state)
    tok = lambda width: pl.BlockSpec((None, tl, width), lambda i, j: (i, j, 0))
    kt_spec = pl.BlockSpec((None, ML_HEADS, ML_DH, tl), lambda i, j: (i, 0, 0, j))
    c_spec = pl.BlockSpec((None, ML_HEADS, ML_DH, ML_DH), lambda i, j: (i, 0, 0, 0))
    n_spec = pl.BlockSpec((None, ML_HEADS, ML_DH), lambda i, j: (i, 0, 0))
    m_spec = pl.BlockSpec((None, ML_HEADS, 1, 1), lambda i, j: (i, 0, 0, 0))
    in_specs = [tok(d), tok(inner), kt_spec, tok(inner), tok(LANES), tok(inner), tok(inner)]
    args = [x, q, kt, v, gates, sz, sx]
    if has_state:
        in_specs += [c_spec, n_spec, m_spec]
        args += [state[0], state[1], state[2].reshape(b, ML_HEADS, 1, 1)]
    in_specs += [_full((1, inner)), _full((inner, d))]
    args += [w['ml_gn'], w['ml_w_out']]
    sds = jax.ShapeDtypeStruct
    out_shape = (sds(x.shape, F32), sds((b, ML_HEADS, ML_DH, ML_DH), F32), sds((b, ML_HEADS, ML_DH), F32),
                 sds((b, ML_HEADS, 1, 1), F32))
    xo, c, n, m = pl.pallas_call(
        kern, grid=(b, nl), in_specs=in_specs, out_specs=[tok(d), c_spec, n_spec, m_spec], out_shape=out_shape,
        scratch_shapes=[pltpu.VMEM((ML_HEADS, ML_DH, LANES), F32)],
        compiler_params=_params("parallel", "arbitrary"), name="mlstm_cell",
    )(*args)
    return xo, c, n, m.reshape(b, ML_HEADS)


def _memkv_kernel(m_ref, g_ref, wk_ref, wv_ref, k_ref, v_ref):
    mn = _rms(m_ref[...], g_ref[...]).astype(BF16)
    k_ref[...] = _dot(mn, wk_ref[...])
    v_ref[...] = _dot(mn, wv_ref[...])


def _memkv(mem, norm_mem, wk, wv):
    b, m_len, d = mem.shape
    depth = wk.shape[0]
    rows = b * m_len
    tm = min(MEM_ROW_TILE, rows)
    mem2 = mem.reshape(rows, d)
    wspec = pl.BlockSpec((None, d, d), lambda i, j: (i, 0, 0))
    ospec = pl.BlockSpec((None, tm, d), lambda i, j: (i, j, 0))
    k, v = pl.pallas_call(
        _memkv_kernel, grid=(depth, rows // tm),
        in_specs=[pl.BlockSpec((tm, d), lambda i, j: (j, 0)), pl.BlockSpec((None, 1, d), lambda i, j: (i, 0, 0)),
                  wspec, wspec],
        out_specs=[ospec, ospec],
        out_shape=(jax.ShapeDtypeStruct((depth, rows, d), F32), jax.ShapeDtypeStruct((depth, rows, d), F32)),
        compiler_params=_params("parallel", "parallel"), name="memory_kv",
    )(mem2, norm_mem.reshape(depth, 1, d), wk, wv)
    return k.reshape(depth, b, m_len, d), v.reshape(depth, b, m_len, d)


def _xattn_kernel(x_ref, g_ref, wq_ref, mk_ref, mv_ref, wo_ref, o_ref):
    x = x_ref[...]
    xn = _rms(x, g_ref[...]).astype(BF16)
    q = _dot(xn, wq_ref[...])
    hsl = [slice(h * X_DH, (h + 1) * X_DH) for h in range(X_HEADS)]

    def scores(h):
        return _dot_nt(q[:, hsl[h]].astype(BF16), mk_ref[:, hsl[h]].astype(BF16))

    out = x
    nxt = scores(0)
    for h in range(X_HEADS):
        s = nxt * (X_DH ** -0.5)
        if h + 1 < X_HEADS:
            nxt = scores(h + 1)
        e = jnp.exp(s - jnp.max(s, axis=-1, keepdims=True))
        pv = _dot(e.astype(BF16), mv_ref[:, hsl[h]].astype(BF16))
        oh = pv * (1.0 / jnp.sum(e, axis=-1, keepdims=True))
        out = out + _dot(oh.astype(BF16), wo_ref[hsl[h], :])
    o_ref[...] = out


def _xattn(x, layer, g, wq, mk, mv, wo):
    b, L, d = x.shape
    tl = min(XATTN_TILE, L)
    m_len = mk.shape[2]
    tok = pl.BlockSpec((None, tl, d), lambda i, j: (i, j, 0))
    mem = pl.BlockSpec((None, None, m_len, d), lambda i, j: (layer, i, 0, 0))
    wsp = pl.BlockSpec((None, d, d), lambda i, j: (layer, 0, 0))
    gsp = pl.BlockSpec((None, 1, d), lambda i, j: (layer, 0, 0))
    return pl.pallas_call(
        _xattn_kernel, grid=(b, L // tl),
        in_specs=[tok, gsp, wsp, mem, mem, wsp],
        out_specs=tok, out_shape=jax.ShapeDtypeStruct(x.shape, F32),
        compiler_params=_params("parallel", "arbitrary"), name="cross_attn",
    )(x, g, wq, mk, mv, wo)


def _mlp_kernel(*refs, final):
    if final:
        x_ref, g_ref, wu_ref, wd_ref, gf_ref, o_ref, xn_ref = refs
    else:
        x_ref, g_ref, wu_ref, wd_ref, o_ref, xn_ref = refs
    j = pl.program_id(1)

    @pl.when(j == 0)
    def _():
        x = x_ref[...]
        xn_ref[...] = _rms(x, g_ref[...]).astype(BF16)
        o_ref[...] = x

    hdn = jnp.maximum(_dot(xn_ref[...], wu_ref[...]), 0.0)
    o_ref[...] += _dot((hdn * hdn).astype(BF16), wd_ref[...])

    if final:
        @pl.when(j == pl.num_programs(1) - 1)
        def _():
            o_ref[...] = _rms(o_ref[...], gf_ref[...])


def _mlp(x, layer, g, wu, wd, gf=None):
    shape = x.shape
    d = shape[-1]
    x2 = x.reshape(-1, d)
    rows = x2.shape[0]
    ff = wu.shape[2]
    tm = min(MLP_ROW_TILE, rows)
    tf = min(MLP_FF_TILE, ff)
    final = gf is not None
    in_specs = [pl.BlockSpec((tm, d), lambda i, j: (i, 0)), pl.BlockSpec((None, 1, d), lambda i, j: (layer, 0, 0)),
                pl.BlockSpec((None, d, tf), lambda i, j: (layer, 0, j)),
                pl.BlockSpec((None, tf, d), lambda i, j: (layer, j, 0))]
    args = [x2, g, wu, wd]
    if final:
        in_specs.append(_full((1, d)))
        args.append(gf)
    out = pl.pallas_call(
        functools.partial(_mlp_kernel, final=final), grid=(rows // tm, ff // tf),
        in_specs=in_specs, out_specs=pl.BlockSpec((tm, d), lambda i, j: (i, 0)),
        out_shape=jax.ShapeDtypeStruct(x2.shape, F32),
        scratch_shapes=[pltpu.VMEM((tm, d), BF16)],
        compiler_params=_params("parallel", "arbitrary"), name="mlp",
    )(*args)
    return out.reshape(shape)


def _trunk(x, pos0, st_ret, st_re, st_im, ml_state, conv_state, mem_k, mem_v, w):
    b, L, d = x.shape
    tl = min(TOKEN_TILE, L)
    nst = S5_GROUPS * S5_STATE // (SUBLANES * LANES)
    st_h = jnp.concatenate([st_re.reshape(b, nst, SUBLANES, LANES), st_im.reshape(b, nst, SUBLANES, LANES)], axis=1)
    x, s_new, h_new = _rs_layer(x, st_ret, st_h, _ret_consts(L, tl, pos0), w)
    hr = h_new[:, :nst].reshape(b, S5_GROUPS, S5_STATE)
    hi = h_new[:, nst:].reshape(b, S5_GROUPS, S5_STATE)
    x = _xattn(x, 0, w['norm_cross'], w['x_wq'], mem_k, mem_v, w['x_wo'])
    x = _mlp(x, 0, w['norm_mlp'], w['mlp_w_up'], w['mlp_w_down'])
    q, kt, v, gates, sz, sx, new_conv = _ml_a(x, conv_state, w)
    x, cf, nf, mf = _ml_b(x, q, kt, v, gates, sz, sx, ml_state, w)
    x = _xattn(x, 1, w['norm_cross'], w['x_wq'], mem_k, mem_v, w['x_wo'])
    y = _mlp(x, 1, w['norm_mlp'], w['mlp_w_up'], w['mlp_w_down'], w['norm_final'])
    return y, s_new[None], hr[None], hi[None], cf[None], nf[None], mf[None], new_conv[None]


def kernel(x_prompt, x_sample, mem_prompt, state_ret, state_s5_re, state_s5_im, state_mlstm_c, state_mlstm_n, state_mlstm_m, cache_mlstm_conv, cache_mem_k, cache_mem_v, norm_mix, norm_cross, norm_mem, norm_mlp, norm_final, rs_w_in, rs_w_out, ret_gn, s5_a_re, s5_a_im, s5_log_dt, s5_b_re, s5_b_im, s5_c_re, s5_c_im, s5_d, s5_w_glu, s5_b_glu, ml_w_in, ml_conv_w, ml_conv_b, ml_wq, ml_wk, ml_wv, ml_w_gates, ml_b_gates, ml_gn, ml_skip, ml_w_out, x_wq, x_wk, x_wv, x_wo, mlp_w_up, mlp_w_down):
    d = x_prompt.shape[-1]
    bp = x_prompt.shape[0]
    bf = lambda t: t.astype(BF16)
    row = lambda t: t.reshape(1, -1).astype(F32)
    ab_re, ab_im, bb_re, bb_im = _s5_prep(s5_a_re[0], s5_a_im[0], s5_log_dt[0], s5_b_re[0], s5_b_im[0])
    s5_a, s5_b, s5_c = _s5_matrices(ab_re, ab_im, bb_re, bb_im, s5_c_re[0], s5_c_im[0])
    n_gate = ml_w_gates.shape[-1]
    w = {
        'norm_mix0': row(norm_mix[0]), 'norm_mix1': row(norm_mix[1]),
        'norm_cross': norm_cross.reshape(-1, 1, d), 'norm_mlp': norm_mlp.reshape(-1, 1, d),
        'norm_final': row(norm_final),
        'rs_w_in': bf(rs_w_in[0]), 'rs_w_out': bf(rs_w_out[0]), 'ret_gn': row(ret_gn[0]),
        's5_a': s5_a, 's5_b': s5_b, 's5_c': s5_c, 's5_d': row(s5_d[0]), 's5_w_glu': bf(s5_w_glu[0]),
        's5_b_glu': row(s5_b_glu[0]),
        'ml_w_in': bf(ml_w_in[0]), 'ml_conv_w': ml_conv_w[0], 'ml_conv_b': row(ml_conv_b[0]),
        'ml_wq': bf(ml_wq[0]), 'ml_wk': bf(ml_wk[0]), 'ml_wv': bf(ml_wv[0]),
        'ml_w_gates': bf(jnp.pad(ml_w_gates[0], ((0, 0), (0, LANES - n_gate)))),
        'ml_b_gates': jnp.pad(ml_b_gates[0], (0, LANES - n_gate)).reshape(1, LANES),
        'ml_gn': row(ml_gn[0]), 'ml_skip': row(ml_skip[0]), 'ml_w_out': bf(ml_w_out[0]),
        'x_wq': bf(x_wq), 'x_wo': bf(x_wo), 'mlp_w_up': bf(mlp_w_up), 'mlp_w_down': bf(mlp_w_down),
    }
    mem_len = mem_prompt.shape[1]
    mk_p, mv_p = _memkv(mem_prompt, norm_mem, bf(x_wk), bf(x_wv))
    zeros = lambda *s: jnp.zeros(s, F32)
    out_p = _trunk(x_prompt, 0, zeros(bp, RET_HEADS, RET_DK, RET_DK), zeros(bp, S5_GROUPS, S5_STATE),
                   zeros(bp, S5_GROUPS, S5_STATE), None, zeros(bp, ML_CONV - 1, ML_HEADS * ML_DH), mk_p, mv_p, w)
    bs = x_sample.shape[0]
    out_s = _trunk(x_sample, PAST_LEN, state_ret[0], state_s5_re[0], state_s5_im[0],
                   (state_mlstm_c[0], state_mlstm_n[0], state_mlstm_m[0]), cache_mlstm_conv[0],
                   cache_mem_k.reshape(-1, bs, mem_len, d), cache_mem_v.reshape(-1, bs, mem_len, d), w)
    mem_k_p = mk_p.reshape(-1, bp, mem_len, X_HEADS, X_DH)
    mem_v_p = mv_p.reshape(-1, bp, mem_len, X_HEADS, X_DH)
    return (out_p[0], out_s[0]) + tuple(out_p[1:]) + (mem_k_p, mem_v_p) + tuple(out_s[1:])
```

```python
import functools
import math

import jax
import jax.numpy as jnp
from jax import lax
from jax.experimental import pallas as pl
from jax.experimental.pallas import tpu as pltpu

F32 = jnp.float32
BF16 = jnp.bfloat16

EPS = 1e-6
ROPE_BASE = 10000.0
PAST_LEN = 4096
RET_HEADS = 4
RET_DK = 128
S5_GROUPS = 32
S5_GROUP = 16
S5_STATE = 64
ML_HEADS = 4
ML_DH = 512
ML_CONV = 4
X_HEADS = 4
X_DH = 256
LANES = 128
SUBLANES = 8
VMEM_LIMIT = 56 * 1024 * 1024
TOKEN_TILE = 256
XATTN_TILE = 512
MLP_ROW_TILE = 1024
MLP_FF_TILE = 1024
MEM_ROW_TILE = 1024


def _dot(a, b):
    return jnp.dot(a, b, preferred_element_type=F32)


def _dot_nt(a, b):
    return lax.dot_general(a, b, (((1,), (1,)), ((), ())), preferred_element_type=F32)


def _dot_tn(a, b):
    return lax.dot_general(a, b, (((0,), (0,)), ((), ())), preferred_element_type=F32)


def _rms(x, g):
    y = x * lax.rsqrt(jnp.mean(x * x, axis=-1, keepdims=True) + EPS)
    return y * g


def _layernorm(x):
    mu = jnp.mean(x, axis=-1, keepdims=True)
    xc = x - mu
    var = jnp.mean(xc * xc, axis=-1, keepdims=True)
    return xc * lax.rsqrt(var + EPS)


def _sigmoid(x):
    return 1.0 / (1.0 + jnp.exp(-x))


def _log_sigmoid(x):
    return -(jnp.maximum(-x, 0.0) + jnp.log(1.0 + jnp.exp(-jnp.abs(x))))


def _gelu_tanh(x):
    c = math.sqrt(2.0 / math.pi)
    return 0.5 * x * (1.0 + jnp.tanh(c * (x + 0.044715 * (x * x * x))))


def _params(*sem):
    return pltpu.CompilerParams(dimension_semantics=sem, vmem_limit_bytes=VMEM_LIMIT)


def _full(shape):
    n = len(shape)
    return pl.BlockSpec(shape, lambda *_: (0,) * n)


def _s5_prep_kernel(are_ref, aim_ref, ldt_ref, bre_ref, bim_ref, abre_ref, abim_ref, bbre_ref, bbim_ref):
    a_re = are_ref[...]
    a_im = aim_ref[...]
    dt = jnp.exp(ldt_ref[...])
    mag = jnp.exp(a_re * dt)
    ab_re = mag * jnp.cos(a_im * dt)
    ab_im = mag * jnp.sin(a_im * dt)
    den = a_re * a_re + a_im * a_im
    x_re = ab_re - 1.0
    f_re = (x_re * a_re + ab_im * a_im) / den
    f_im = (ab_im * a_re - x_re * a_im) / den
    b_re = bre_ref[...]
    b_im = bim_ref[...]
    abre_ref[...] = ab_re
    abim_ref[...] = ab_im
    bbre_ref[...] = f_re * b_re - f_im * b_im
    bbim_ref[...] = f_re * b_im + f_im * b_re


def _s5_prep(a_re, a_im, log_dt, b_re, b_im):
    g, p, c = b_re.shape
    out = pl.pallas_call(
        _s5_prep_kernel,
        out_shape=(jax.ShapeDtypeStruct((g, 1, p), F32), jax.ShapeDtypeStruct((g, 1, p), F32),
                   jax.ShapeDtypeStruct((g, c, p), F32), jax.ShapeDtypeStruct((g, c, p), F32)),
        name="s5_prep",
    )(a_re.reshape(g, 1, p), a_im.reshape(g, 1, p), log_dt.reshape(g, 1, 1),
      jnp.swapaxes(b_re, 1, 2), jnp.swapaxes(b_im, 1, 2))
    return out


def _s5_matrices(ab_re, ab_im, bb_re, bb_im, c_re, c_im):
    g, c, p = bb_re.shape
    nsl = g * c // LANES
    gs = LANES // c
    eye = jnp.eye(gs, dtype=F32)

    def bmat(bb):
        t = bb.reshape(nsl, gs, c, p)
        w = jnp.einsum('mgcp,hg->mhcgp', t, eye)
        return w.reshape(nsl, gs * c, gs * p)

    def cmat(cc):
        t = cc.reshape(nsl, gs, c, p)
        w = jnp.einsum('mgcp,hg->mhpgc', t, eye)
        return w.reshape(nsl, gs * p, gs * c)

    b_all = jnp.stack([bmat(bb_re), bmat(bb_im)], axis=1).reshape(nsl * 2, gs * c, gs * p)
    c_all = jnp.concatenate([cmat(c_re), -cmat(c_im)], axis=1)
    nst = g * p // (SUBLANES * LANES)
    a_all = jnp.concatenate([ab_re.reshape(nst, SUBLANES, LANES), ab_im.reshape(nst, SUBLANES, LANES)], axis=0)
    return a_all, b_all.astype(BF16), c_all.astype(BF16)


def _rs_kernel(x_ref, gn_ref, win_ref, cos_ref, sin_ref, decay_ref, wq_ref, wkv_ref, gc_ref,
               s0_ref, h0_ref, rgn_ref, a_ref, bm_ref, cm_ref, dsk_ref, wglu_ref, bglu_ref, wout_ref,
               xo_ref, s_ref, h_ref, bu_ref, *, tl, nl):
    l = pl.program_id(1)
    qk_w = RET_HEADS * RET_DK
    nsl = S5_GROUPS * S5_GROUP // LANES
    nst = S5_GROUPS * S5_STATE // (SUBLANES * LANES)
    rows_per_slab = SUBLANES // (nsl // nst)

    @pl.when(l == 0)
    def _():
        s_ref[...] = s0_ref[...]
        h_ref[...] = h0_ref[...]

    x = x_ref[...]
    xn = _rms(x, gn_ref[...]).astype(BF16)
    proj = _dot(xn, win_ref[...])
    cos = cos_ref[...]
    sin = sin_ref[...]

    u_off = 4 * qk_w
    for m in range(nsl):
        um = proj[:, u_off + m * LANES: u_off + (m + 1) * LANES].astype(BF16)
        for ri in range(2):
            r = _dot(um, bm_ref[2 * m + ri])
            slab = ri * nst + m // (nsl // nst)
            for jl in range(rows_per_slab):
                j = rows_per_slab * (m % (nsl // nst)) + jl
                bu_ref[slab, pl.ds(j, tl, stride=SUBLANES), :] = r[:, jl * LANES:(jl + 1) * LANES]

    a = [a_ref[i] for i in range(2 * nst)]

    def scan_steps(carry, t0, t1):
        for t in range(t0, t1):
            rows = pl.ds(t * SUBLANES, SUBLANES)
            new = []
            for s in range(nst):
                hr, hi = carry[s], carry[nst + s]
                ar, ai = a[s], a[nst + s]
                nr = ar * hr - ai * hi + bu_ref[s, rows, :]
                ni = ar * hi + ai * hr + bu_ref[nst + s, rows, :]
                bu_ref[s, rows, :] = nr
                bu_ref[nst + s, rows, :] = ni
                new.append((nr, ni))
            carry = tuple(n[0] for n in new) + tuple(n[1] for n in new)
        return carry

    def rope_head(h):
        q = proj[:, h * RET_DK:(h + 1) * RET_DK]
        k = proj[:, qk_w + h * RET_DK: qk_w + (h + 1) * RET_DK]
        qr = q * cos + pltpu.roll(q, RET_DK // 2, 1) * sin
        kr = (k * cos + pltpu.roll(k, RET_DK // 2, 1) * sin) * (RET_DK ** -0.5)
        qb = qr.astype(BF16)
        return qb, kr, _dot_nt(qb, kr.astype(BF16)), _dot(qb, s_ref[h].astype(BF16))

    carry = tuple(h_ref[i] for i in range(2 * nst))
    per_head = tl // RET_HEADS
    pieces = []
    nxt = rope_head(0)
    for h in range(RET_HEADS):
        hs = slice(h * RET_DK, (h + 1) * RET_DK)
        qb, kr, sqk, cross = nxt
        vb = proj[:, 2 * qk_w + h * RET_DK: 2 * qk_w + (h + 1) * RET_DK].astype(BF16)
        g = proj[:, 3 * qk_w + h * RET_DK: 3 * qk_w + (h + 1) * RET_DK]
        o = _dot((sqk * decay_ref[h]).astype(BF16), vb) + cross * wq_ref[:, hs]
        if h + 1 < RET_HEADS:
            nxt = rope_head(h + 1)
        carry = scan_steps(carry, h * per_head, (h + 1) * per_head)
        s_ref[h] = gc_ref[h] * s_ref[h] + _dot_tn((kr * wkv_ref[:, hs]).astype(BF16), vb)
        mu = jnp.mean(o, axis=-1, keepdims=True)
        var = jnp.mean(o * o, axis=-1, keepdims=True) - mu * mu
        y = (o - mu) * lax.rsqrt(var + EPS) * rgn_ref[:, hs]
        pieces.append((g * _sigmoid(g) * y).astype(BF16))
    carry = scan_steps(carry, RET_HEADS * per_head, tl)
    for i in range(2 * nst):
        h_ref[i] = carry[i]
    out = x + _dot(jnp.concatenate(pieces, axis=1), wout_ref[0:qk_w, :])

    ys = []
    for m in range(nsl):
        parts = []
        for ri in range(2):
            slab = ri * nst + m // (nsl // nst)
            for jl in range(rows_per_slab):
                j = rows_per_slab * (m % (nsl // nst)) + jl
                parts.append(bu_ref[slab, pl.ds(j, tl, stride=SUBLANES), :])
        hcat = jnp.concatenate(parts, axis=1).astype(BF16)
        um = proj[:, u_off + m * LANES: u_off + (m + 1) * LANES]
        ym = _dot(hcat, cm_ref[m]) + dsk_ref[:, m * LANES:(m + 1) * LANES] * um
        ys.append(_gelu_tanh(ym))
    yg = jnp.concatenate(ys, axis=1)
    gate = _sigmoid(_dot(yg.astype(BF16), wglu_ref[...]) + bglu_ref[...])
    xo_ref[...] = out + _dot((yg * gate).astype(BF16), wout_ref[qk_w:, :])


def _rs_layer(x, st_ret, st_h, consts, w):
    b, L, d = x.shape
    tl = min(TOKEN_TILE, L)
    nl = L // tl
    cos, sin, decay, wq, wkv, gc = consts
    rs_in = w['rs_w_in'].shape[1]
    nst2 = st_h.shape[1]
    kern = functools.partial(_rs_kernel, tl=tl, nl=nl)
    tok = pl.BlockSpec((None, tl, d), lambda i, j: (i, j, 0))
    in_specs = [
        tok, _full((1, d)), _full((d, rs_in)),
        pl.BlockSpec((tl, LANES), lambda i, j: (j, 0)), pl.BlockSpec((tl, LANES), lambda i, j: (j, 0)),
        _full(decay.shape), _full(wq.shape), _full(wkv.shape), _full(gc.shape),
        pl.BlockSpec((None,) + st_ret.shape[1:], lambda i, j: (i, 0, 0, 0)),
        pl.BlockSpec((None,) + st_h.shape[1:], lambda i, j: (i, 0, 0, 0)),
        _full((1, RET_HEADS * RET_DK)), _full(w['s5_a'].shape), _full(w['s5_b'].shape), _full(w['s5_c'].shape),
        _full((1, w['s5_d'].shape[1])), _full(w['s5_w_glu'].shape), _full((1, w['s5_b_glu'].shape[1])),
        _full(w['rs_w_out'].shape),
    ]
    out_specs = [
        tok,
        pl.BlockSpec((None,) + st_ret.shape[1:], lambda i, j: (i, 0, 0, 0)),
        pl.BlockSpec((None,) + st_h.shape[1:], lambda i, j: (i, 0, 0, 0)),
    ]
    return pl.pallas_call(
        kern, grid=(b, nl), in_specs=in_specs, out_specs=out_specs,
        out_shape=(jax.ShapeDtypeStruct(x.shape, F32), jax.ShapeDtypeStruct(st_ret.shape, F32),
                   jax.ShapeDtypeStruct(st_h.shape, F32)),
        scratch_shapes=[pltpu.VMEM((nst2, tl * SUBLANES, LANES), F32)],
        compiler_params=_params("parallel", "arbitrary"), name="rs_mixer",
    )(x, w['norm_mix0'], w['rs_w_in'], cos, sin, decay, wq, wkv, gc, st_ret, st_h, w['ret_gn'],
      w['s5_a'], w['s5_b'], w['s5_c'], w['s5_d'], w['s5_w_glu'], w['s5_b_glu'], w['rs_w_out'])


def _ret_consts(L, tl, pos0):
    half = RET_DK // 2
    pos = pos0 + jnp.arange(L, dtype=jnp.int32)
    inv = ROPE_BASE ** (-jnp.arange(half, dtype=F32) / half)
    ang = pos.astype(F32)[:, None] * inv[None, :]
    cos = jnp.cos(ang)
    sin = jnp.sin(ang)
    cos2 = jnp.concatenate([cos, cos], axis=1)
    sin2 = jnp.concatenate([-sin, sin], axis=1)
    log_g = jnp.log1p(-jnp.exp2(-5.0 - jnp.arange(RET_HEADS, dtype=F32)))
    idx = jnp.arange(tl, dtype=F32)
    diff = idx[:, None] - idx[None, :]
    decay = jnp.where(diff[None] >= 0, jnp.exp(jnp.maximum(diff, 0.0)[None] * log_g[:, None, None]), 0.0)
    w_q = jnp.exp((idx + 1.0)[:, None] * log_g[None, :])
    w_kv = jnp.exp((tl - 1.0 - idx)[:, None] * log_g[None, :])
    g_c = jnp.exp(tl * log_g)
    rep = lambda t: jnp.repeat(t, RET_DK, axis=1)
    gc = jnp.broadcast_to(g_c[:, None, None], (RET_HEADS, 1, RET_DK))
    return cos2, sin2, decay, rep(w_q), rep(w_kv), gc


def _ml_a_kernel(x_ref, gn_ref, win_ref, cs_ref, cw_ref, cb_ref, wq_ref, wk_ref, wv_ref, wg_ref, bg_ref, skip_ref,
                 q_ref, kt_ref, v_ref, gates_ref, sz_ref, sx_ref, nc_ref, xs_ref, *, tl, nl):
    l = pl.program_id(1)
    inner = ML_HEADS * ML_DH
    pad = SUBLANES
    hist = ML_CONV - 1

    @pl.when(l == 0)
    def _():
        xs_ref[0:pad, :] = jnp.zeros((pad, inner), F32)
        xs_ref[pl.ds(pad - hist, hist), :] = cs_ref[...]

    xn = _rms(x_ref[...], gn_ref[...]).astype(BF16)
    xm = _dot(xn, win_ref[:, 0:inner])
    sz_ref[...] = _sigmoid(_dot(xn, win_ref[:, inner:2 * inner])).astype(sz_ref.dtype)
    xs_ref[pl.ds(pad, tl), :] = xm
    acc = cw_ref[hist:hist + 1, :] * xm
    for j in range(hist):
        acc = acc + cw_ref[j:j + 1, :] * xs_ref[pl.ds(pad - hist + j, tl), :]
    xc = acc + cb_ref[...]
    xc = xc * _sigmoid(xc)
    sx_ref[...] = (skip_ref[...] * xc).astype(sx_ref.dtype)
    gates = jnp.zeros((tl, LANES), F32) + bg_ref[...]
    for h in range(ML_HEADS):
        hs = slice(h * ML_DH, (h + 1) * ML_DH)
        xch = xc[:, hs].astype(BF16)
        xmh = xm[:, hs].astype(BF16)
        qh = _dot(xch, wq_ref[h]).astype(BF16)
        kf = _dot(xch, wk_ref[h])
        kh = kf.astype(BF16)
        vh = _dot(xmh, wv_ref[h]).astype(BF16)
        q_ref[:, hs] = qh
        kt_ref[h] = kf.T.astype(BF16)
        v_ref[:, hs] = vh
        gates = gates + _dot(qh, wg_ref[h * ML_DH:(h + 1) * ML_DH, :])
        gates = gates + _dot(kh, wg_ref[inner + h * ML_DH: inner + (h + 1) * ML_DH, :])
        gates = gates + _dot(vh, wg_ref[2 * inner + h * ML_DH: 2 * inner + (h + 1) * ML_DH, :])
    gates_ref[...] = gates

    @pl.when(l == nl - 1)
    def _():
        nc_ref[...] = xs_ref[pl.ds(pad + tl - hist, hist), :]

    xs_ref[0:pad, :] = xs_ref[pl.ds(tl, pad), :]


def _ml_a(x, conv_state, w):
    b, L, d = x.shape
    tl = min(TOKEN_TILE, L)
    nl = L // tl
    inner = ML_HEADS * ML_DH
    kern = functools.partial(_ml_a_kernel, tl=tl, nl=nl)
    tok = lambda width: pl.BlockSpec((None, tl, width), lambda i, j: (i, j, 0))
    kt_spec = pl.BlockSpec((None, ML_HEADS, ML_DH, tl), lambda i, j: (i, 0, 0, j))
    cst = pl.BlockSpec((None, ML_CONV - 1, inner), lambda i, j: (i, 0, 0))
    in_specs = [tok(d), _full((1, d)), _full((d, 2 * inner)), cst, _full((ML_CONV, inner)), _full((1, inner)),
                _full(w['ml_wq'].shape), _full(w['ml_wk'].shape), _full(w['ml_wv'].shape),
                _full(w['ml_w_gates'].shape), _full((1, LANES)), _full((1, inner))]
    out_specs = [tok(inner), kt_spec, tok(inner), tok(LANES), tok(inner), tok(inner), cst]
    sds = jax.ShapeDtypeStruct
    out_shape = (sds((b, L, inner), BF16), sds((b, ML_HEADS, ML_DH, L), BF16), sds((b, L, inner), BF16),
                 sds((b, L, LANES), F32), sds((b, L, inner), BF16), sds((b, L, inner), BF16),
                 sds((b, ML_CONV - 1, inner), F32))
    return pl.pallas_call(
        kern, grid=(b, nl), in_specs=in_specs, out_specs=out_specs, out_shape=out_shape,
        scratch_shapes=[pltpu.VMEM((tl + SUBLANES, inner), F32)],
        compiler_params=_params("parallel", "arbitrary"), name="mlstm_proj",
    )(x, w['norm_mix1'], w['ml_w_in'], conv_state, w['ml_conv_w'], w['ml_conv_b'],
      w['ml_wq'], w['ml_wk'], w['ml_wv'], w['ml_w_gates'], w['ml_b_gates'], w['ml_skip'])


def _split3(x):
    hi = x.astype(BF16)
    r1 = x - hi.astype(F32)
    mid = r1.astype(BF16)
    lo = (r1 - mid.astype(F32)).astype(BF16)
    return hi, mid, lo


def _ml_b_kernel(*refs, tl, nl, has_state):
    if has_state:
        (x_ref, q_ref, kt_ref, v_ref, g_ref, sz_ref, sx_ref, c0_ref, n0_ref, m0_ref, gn_ref, wout_ref,
         xo_ref, c_ref, n_ref, m_ref, ncm_ref) = refs
    else:
        (x_ref, q_ref, kt_ref, v_ref, g_ref, sz_ref, sx_ref, gn_ref, wout_ref,
         xo_ref, c_ref, n_ref, m_ref, ncm_ref) = refs
    l = pl.program_id(1)
    scale = ML_DH ** -0.5

    @pl.when(l == 0)
    def _():
        if has_state:
            c_ref[...] = c0_ref[...]
            m_ref[...] = m0_ref[...]
            for h in range(ML_HEADS):
                ncm_ref[h] = jnp.broadcast_to(n0_ref[h:h + 1, :], (LANES, ML_DH)).T
        else:
            c_ref[...] = jnp.zeros(c_ref.shape, F32)
            m_ref[...] = jnp.zeros(m_ref.shape, F32)
            ncm_ref[...] = jnp.zeros(ncm_ref.shape, F32)

    gates = g_ref[...]
    gates_t = gates.T
    row = lax.broadcasted_iota(jnp.int32, (tl, tl), 0)
    col = lax.broadcasted_iota(jnp.int32, (tl, tl), 1)
    causal = row >= col
    lower = jnp.where(causal, 1.0, 0.0).astype(BF16)
    upper = jnp.where(row <= col, 1.0, 0.0).astype(BF16)
    ones = jnp.ones((tl, LANES), BF16)
    lf_c = _split3(_log_sigmoid(gates))
    b_f = _dot(lower, lf_c[0]) + _dot(lower, lf_c[1]) + _dot(lower, lf_c[2])
    lf_r = _split3(_log_sigmoid(gates_t[0:2 * ML_HEADS, :]))
    b_row_all = _dot(lf_r[0], upper) + _dot(lf_r[1], upper) + _dot(lf_r[2], upper)
    b_c = pltpu.roll(b_f, LANES - ML_HEADS, 1)
    lane = lax.broadcasted_iota(jnp.int32, (1, LANES), 1)
    m_prev_row = jnp.zeros((1, LANES), F32)
    for h in range(ML_HEADS):
        m_prev_row = jnp.where(lane == h, m_ref[h], m_prev_row)
    run = gates - b_c
    trow = lax.broadcasted_iota(jnp.int32, (tl, LANES), 0)
    shift = 1
    while shift < tl:
        run = jnp.maximum(run, jnp.where(trow >= shift, pltpu.roll(run, shift, 0), -jnp.inf))
        shift *= 2
    m_t_all = b_c + jnp.maximum(m_prev_row, run)
    s_inter_all = jnp.exp(b_c + m_prev_row - m_t_all)
    dmin_all = jnp.exp(-m_t_all)
    a_all = b_c - m_t_all + math.log(scale)
    m_new_row = m_t_all[tl - 1:tl, :]
    b_last_row = b_c[tl - 1:tl, :]
    dec_row = jnp.exp(b_last_row + m_prev_row - m_new_row)
    wsc_row = b_last_row - m_new_row

    hsl = [slice(h * ML_DH, (h + 1) * ML_DH) for h in range(ML_HEADS)]

    def state_dots(h):
        q = q_ref[:, hsl[h]]
        return (_dot(q, kt_ref[h]), _dot(q, c_ref[h].astype(BF16)), _dot(q, ncm_ref[h].astype(BF16))[:, 0:1])

    out = x_ref[...]
    nxt = state_dots(0)
    for h in range(ML_HEADS):
        hs = hsl[h]
        sqk, qc, qn = nxt
        v = v_ref[:, hs]
        r_row = gates_t[h:h + 1, :] - b_row_all[ML_HEADS + h: ML_HEADS + h + 1, :]
        s_inter = s_inter_all[:, h:h + 1]
        dec = dec_row[:, h:h + 1]
        wgt = jnp.exp(jnp.where(causal, a_all[:, h:h + 1] + r_row, -jnp.inf))
        qk = sqk * wgt
        num = s_inter * qc + _dot(qk.astype(BF16), v)
        den = s_inter * qn + jnp.sum(qk, axis=-1, keepdims=True)
        if h + 1 < ML_HEADS:
            nxt = state_dots(h + 1)
        dmax = jnp.maximum(jnp.abs(den), dmin_all[:, h:h + 1])
        xg = sz_ref[:, hs].astype(F32) * num
        mu = jnp.mean(xg, axis=-1, keepdims=True)
        var = jnp.mean(xg * xg, axis=-1, keepdims=True) - mu * mu
        y = (xg - mu) * lax.rsqrt(var + EPS * (dmax * dmax)) * gn_ref[:, hs] + sx_ref[:, hs].astype(F32)
        kwt = (kt_ref[h].astype(F32) * (scale * jnp.exp(r_row + wsc_row[:, h:h + 1]))).astype(BF16)
        c_ref[h] = dec * c_ref[h] + _dot(kwt, v)
        ncm_ref[h] = dec * ncm_ref[h] + _dot(kwt, ones)
        m_ref[h] = m_new_row[:, h:h + 1]
        out = out + _dot(y.astype(BF16), wout_ref[hs, :])
    xo_ref[...] = out

    @pl.when(l == nl - 1)
    def _():
        for h in range(ML_HEADS):
            n_ref[h:h + 1, :] = ncm_ref[h].T[0:1, :]


def _ml_b(x, q, kt, v, gates, sz, sx, state, w):
    b, L, d = x.shape
    tl = min(TOKEN_TILE, L)
    nl = L // tl
    inner = ML_HEADS * ML_DH
    has_state = state is not None
    kern = functools.partial(_ml_b_kernel, tl=tl, nl=nl, has_state=has_state)
    tok = lambda width: pl.BlockSpec((None, tl, width), lambda i, j: (i, j, 0))
    kt_spec = pl.BlockSpec((None, ML_HEADS, ML_DH, tl), lambda i, j: (i, 0, 0, j))
    c_spec = pl.BlockSpec((None, ML_HEADS, ML_DH, ML_DH), lambda i, j: (i, 0, 0, 0))
    n_spec = pl.BlockSpec((None, ML_HEADS, ML_DH), lambda i, j: (i, 0, 0))
    m_spec = pl.BlockSpec((None, ML_HEADS, 1, 1), lambda i, j: (i, 0, 0, 0))
    in_specs = [tok(d), tok(inner), kt_spec, tok(inner), tok(LANES), tok(inner), tok(inner)]
    args = [x, q, kt, v, gates, sz, sx]
    if has_state:
        in_specs += [c_spec, n_spec, m_spec]
        args += [state[0], state[1], state[2].reshape(b, ML_HEADS, 1, 1)]
    in_specs += [_full((1, inner)), _full((inner, d))]
    args += [w['ml_gn'], w['ml_w_out']]
    sds = jax.ShapeDtypeStruct
    out_shape = (sds(x.shape, F32), sds((b, ML_HEADS, ML_DH, ML_DH), F32), sds((b, ML_HEADS, ML_DH), F32),
                 sds((b, ML_HEADS, 1, 1), F32))
    xo, c, n, m = pl.pallas_call(
        kern, grid=(b, nl), in_specs=in_specs, out_specs=[tok(d), c_spec, n_spec, m_spec], out_shape=out_shape,
        scratch_shapes=[pltpu.VMEM((ML_HEADS, ML_DH, LANES), F32)],
        compiler_params=_params("parallel", "arbitrary"), name="mlstm_cell",
    )(*args)
    return xo, c, n, m.reshape(b, ML_HEADS)


def _memkv_kernel(m_ref, g_ref, wk_ref, wv_ref, k_ref, v_ref, kb_ref, vb_ref):
    nb, m_len = k_ref.shape[0], k_ref.shape[1]
    mn = _rms(m_ref[...], g_ref[...]).astype(BF16)
    k = _dot(mn, wk_ref[...])
    v = _dot(mn, wv_ref[...])
    for h in range(X_HEADS):
        kh = k[:, h * X_DH:(h + 1) * X_DH].reshape(nb, m_len, X_DH)
        vh = v[:, h * X_DH:(h + 1) * X_DH].reshape(nb, m_len, X_DH)
        k_ref[:, :, h, :] = kh
        v_ref[:, :, h, :] = vh
        kb_ref[:, h] = kh.astype(BF16)
        vb_ref[:, h] = vh.astype(BF16)


def _memkv(mem, norm_mem, wk, wv):
    b, m_len, d = mem.shape
    depth = wk.shape[0]
    rows = b * m_len
    tm = min(MEM_ROW_TILE, rows)
    nb = tm // m_len
    mem2 = mem.reshape(rows, d)
    wspec = pl.BlockSpec((None, d, d), lambda i, j: (i, 0, 0))
    ospec = pl.BlockSpec((None, nb, m_len, X_HEADS, X_DH), lambda i, j: (i, j, 0, 0, 0))
    oshape = jax.ShapeDtypeStruct((depth, b, m_len, X_HEADS, X_DH), F32)
    bspec = pl.BlockSpec((None, nb, X_HEADS, m_len, X_DH), lambda i, j: (i, j, 0, 0, 0))
    bshape = jax.ShapeDtypeStruct((depth, b, X_HEADS, m_len, X_DH), BF16)
    return pl.pallas_call(
        _memkv_kernel, grid=(depth, rows // tm),
        in_specs=[pl.BlockSpec((tm, d), lambda i, j: (j, 0)), pl.BlockSpec((None, 1, d), lambda i, j: (i, 0, 0)),
                  wspec, wspec],
        out_specs=[ospec, ospec, bspec, bspec],
        out_shape=(oshape, oshape, bshape, bshape),
        compiler_params=_params("parallel", "parallel"), name="memory_kv",
    )(mem2, norm_mem.reshape(depth, 1, d), wk, wv)


def _xattn_kernel(x_ref, g_ref, wq_ref, mk_ref, mv_ref, wo_ref, o_ref):
    x = x_ref[...]
    xn = _rms(x, g_ref[...]).astype(BF16)
    q = _dot(xn, wq_ref[...])
    hsl = [slice(h * X_DH, (h + 1) * X_DH) for h in range(X_HEADS)]

    def scores(h):
        return _dot_nt(q[:, hsl[h]].astype(BF16), mk_ref[h])

    out = x
    nxt = scores(0)
    for h in range(X_HEADS):
        s = nxt * (X_DH ** -0.5)
        if h + 1 < X_HEADS:
            nxt = scores(h + 1)
        e = jnp.exp(s - jnp.max(s, axis=-1, keepdims=True))
        pv = _dot(e.astype(BF16), mv_ref[h])
        oh = pv * (1.0 / jnp.sum(e, axis=-1, keepdims=True))
        out = out + _dot(oh.astype(BF16), wo_ref[hsl[h], :])
    o_ref[...] = out


def _xattn(x, layer, g, wq, mk, mv, wo):
    b, L, d = x.shape
    tl = min(XATTN_TILE, L)
    m_len = mk.shape[3]
    tok = pl.BlockSpec((None, tl, d), lambda i, j: (i, j, 0))
    mem = pl.BlockSpec((None, None, X_HEADS, m_len, X_DH), lambda i, j: (layer, i, 0, 0, 0))
    wsp = pl.BlockSpec((None, d, d), lambda i, j: (layer, 0, 0))
    gsp = pl.BlockSpec((None, 1, d), lambda i, j: (layer, 0, 0))
    return pl.pallas_call(
        _xattn_kernel, grid=(b, L // tl),
        in_specs=[tok, gsp, wsp, mem, mem, wsp],
        out_specs=tok, out_shape=jax.ShapeDtypeStruct(x.shape, F32),
        compiler_params=_params("parallel", "arbitrary"), name="cross_attn",
    )(x, g, wq, mk, mv, wo)


def _mlp_kernel(*refs, final):
    if final:
        x_ref, g_ref, wu_ref, wd_ref, gf_ref, o_ref, xn_ref = refs
    else:
        x_ref, g_ref, wu_ref, wd_ref, o_ref, xn_ref = refs
    j = pl.program_id(1)

    @pl.when(j == 0)
    def _():
        x = x_ref[...]
        xn_ref[...] = _rms(x, g_ref[...]).astype(BF16)
        o_ref[...] = x

    hdn = jnp.maximum(_dot(xn_ref[...], wu_ref[...]), 0.0)
    o_ref[...] += _dot((hdn * hdn).astype(BF16), wd_ref[...])

    if final:
        @pl.when(j == pl.num_programs(1) - 1)
        def _():
            o_ref[...] = _rms(o_ref[...], gf_ref[...])


def _mlp(x, layer, g, wu, wd, gf=None):
    shape = x.shape
    d = shape[-1]
    x2 = x.reshape(-1, d)
    rows = x2.shape[0]
    ff = wu.shape[2]
    tm = min(MLP_ROW_TILE, rows)
    tf = min(MLP_FF_TILE, ff)
    final = gf is not None
    in_specs = [pl.BlockSpec((tm, d), lambda i, j: (i, 0)), pl.BlockSpec((None, 1, d), lambda i, j: (layer, 0, 0)),
                pl.BlockSpec((None, d, tf), lambda i, j: (layer, 0, j)),
                pl.BlockSpec((None, tf, d), lambda i, j: (layer, j, 0))]
    args = [x2, g, wu, wd]
    if final:
        in_specs.append(_full((1, d)))
        args.append(gf)
    out = pl.pallas_call(
        functools.partial(_mlp_kernel, final=final), grid=(rows // tm, ff // tf),
        in_specs=in_specs, out_specs=pl.BlockSpec((tm, d), lambda i, j: (i, 0)),
        out_shape=jax.ShapeDtypeStruct(x2.shape, F32),
        scratch_shapes=[pltpu.VMEM((tm, d), BF16)],
        compiler_params=_params("parallel", "arbitrary"), name="mlp",
    )(*args)
    return out.reshape(shape)


def _trunk(x, pos0, st_ret, st_re, st_im, ml_state, conv_state, mem_k, mem_v, w):
    b, L, d = x.shape
    tl = min(TOKEN_TILE, L)
    nst = S5_GROUPS * S5_STATE // (SUBLANES * LANES)
    st_h = jnp.concatenate([st_re.reshape(b, nst, SUBLANES, LANES), st_im.reshape(b, nst, SUBLANES, LANES)], axis=1)
    x, s_new, h_new = _rs_layer(x, st_ret, st_h, _ret_consts(L, tl, pos0), w)
    hr = h_new[:, :nst].reshape(b, S5_GROUPS, S5_STATE)
    hi = h_new[:, nst:].reshape(b, S5_GROUPS, S5_STATE)
    x = _xattn(x, 0, w['norm_cross'], w['x_wq'], mem_k, mem_v, w['x_wo'])
    x = _mlp(x, 0, w['norm_mlp'], w['mlp_w_up'], w['mlp_w_down'])
    q, kt, v, gates, sz, sx, new_conv = _ml_a(x, conv_state, w)
    x, cf, nf, mf = _ml_b(x, q, kt, v, gates, sz, sx, ml_state, w)
    x = _xattn(x, 1, w['norm_cross'], w['x_wq'], mem_k, mem_v, w['x_wo'])
    y = _mlp(x, 1, w['norm_mlp'], w['mlp_w_up'], w['mlp_w_down'], w['norm_final'])
    return y, s_new[None], hr[None], hi[None], cf[None], nf[None], mf[None], new_conv[None]


def kernel(x_prompt, x_sample, mem_prompt, state_ret, state_s5_re, state_s5_im, state_mlstm_c, state_mlstm_n, state_mlstm_m, cache_mlstm_conv, cache_mem_k, cache_mem_v, norm_mix, norm_cross, norm_mem, norm_mlp, norm_final, rs_w_in, rs_w_out, ret_gn, s5_a_re, s5_a_im, s5_log_dt, s5_b_re, s5_b_im, s5_c_re, s5_c_im, s5_d, s5_w_glu, s5_b_glu, ml_w_in, ml_conv_w, ml_conv_b, ml_wq, ml_wk, ml_wv, ml_w_gates, ml_b_gates, ml_gn, ml_skip, ml_w_out, x_wq, x_wk, x_wv, x_wo, mlp_w_up, mlp_w_down):
    d = x_prompt.shape[-1]
    bp = x_prompt.shape[0]
    bf = lambda t: t.astype(BF16)
    row = lambda t: t.reshape(1, -1).astype(F32)
    ab_re, ab_im, bb_re, bb_im = _s5_prep(s5_a_re[0], s5_a_im[0], s5_log_dt[0], s5_b_re[0], s5_b_im[0])
    s5_a, s5_b, s5_c = _s5_matrices(ab_re, ab_im, bb_re, bb_im, s5_c_re[0], s5_c_im[0])
    n_gate = ml_w_gates.shape[-1]
    w = {
        'norm_mix0': row(norm_mix[0]), 'norm_mix1': row(norm_mix[1]),
        'norm_cross': norm_cross.reshape(-1, 1, d), 'norm_mlp': norm_mlp.reshape(-1, 1, d),
        'norm_final': row(norm_final),
        'rs_w_in': bf(rs_w_in[0]), 'rs_w_out': bf(rs_w_out[0]), 'ret_gn': row(ret_gn[0]),
        's5_a': s5_a, 's5_b': s5_b, 's5_c': s5_c, 's5_d': row(s5_d[0]), 's5_w_glu': bf(s5_w_glu[0]),
        's5_b_glu': row(s5_b_glu[0]),
        'ml_w_in': bf(ml_w_in[0]), 'ml_conv_w': ml_conv_w[0], 'ml_conv_b': row(ml_conv_b[0]),
        'ml_wq': bf(ml_wq[0]), 'ml_wk': bf(ml_wk[0]), 'ml_wv': bf(ml_wv[0]),
        'ml_w_gates': bf(jnp.pad(ml_w_gates[0], ((0, 0), (0, LANES - n_gate)))),
        'ml_b_gates': jnp.pad(ml_b_gates[0], (0, LANES - n_gate)).reshape(1, LANES),
        'ml_gn': row(ml_gn[0]), 'ml_skip': row(ml_skip[0]), 'ml_w_out': bf(ml_w_out[0]),
        'x_wq': bf(x_wq), 'x_wo': bf(x_wo), 'mlp_w_up': bf(mlp_w_up), 'mlp_w_down': bf(mlp_w_down),
    }
    mk_p, mv_p, mkb_p, mvb_p = _memkv(mem_prompt, norm_mem, bf(x_wk), bf(x_wv))
    zeros = lambda *s: jnp.zeros(s, F32)
    out_p = _trunk(x_prompt, 0, zeros(bp, RET_HEADS, RET_DK, RET_DK), zeros(bp, S5_GROUPS, S5_STATE),
                   zeros(bp, S5_GROUPS, S5_STATE), None, zeros(bp, ML_CONV - 1, ML_HEADS * ML_DH), mkb_p, mvb_p, w)
    out_s = _trunk(x_sample, PAST_LEN, state_ret[0], state_s5_re[0], state_s5_im[0],
                   (state_mlstm_c[0], state_mlstm_n[0], state_mlstm_m[0]), cache_mlstm_conv[0],
                   bf(jnp.swapaxes(cache_mem_k, 2, 3)), bf(jnp.swapaxes(cache_mem_v, 2, 3)), w)
    return (out_p[0], out_s[0]) + tuple(out_p[1:]) + (mk_p, mv_p) + tuple(out_s[1:])
```

```python
import functools
import math

import jax
import jax.numpy as jnp
from jax import lax
from jax.experimental import pallas as pl
from jax.experimental.pallas import tpu as pltpu

F32 = jnp.float32
BF16 = jnp.bfloat16

EPS = 1e-6
ROPE_BASE = 10000.0
PAST_LEN = 4096
RET_HEADS = 4
RET_DK = 128
S5_GROUPS = 32
S5_GROUP = 16
S5_STATE = 64
ML_HEADS = 4
ML_DH = 512
ML_CONV = 4
X_HEADS = 4
X_DH = 256
LANES = 128
SUBLANES = 8
VMEM_LIMIT = 56 * 1024 * 1024
TOKEN_TILE = 256
PROJ_TILE = 512
XATTN_TILE = 1024
MLP_ROW_TILE = 1024
MLP_FF_TILE = 2048
MEM_ROW_TILE = 1024


def _dot(a, b):
    return jnp.dot(a, b, preferred_element_type=F32)


def _dot_nt(a, b):
    return lax.dot_general(a, b, (((1,), (1,)), ((), ())), preferred_element_type=F32)


def _dot_tn(a, b):
    return lax.dot_general(a, b, (((0,), (0,)), ((), ())), preferred_element_type=F32)


def _rms(x, g):
    y = x * lax.rsqrt(jnp.mean(x * x, axis=-1, keepdims=True) + EPS)
    return y * g


def _layernorm(x):
    mu = jnp.mean(x, axis=-1, keepdims=True)
    xc = x - mu
    var = jnp.mean(xc * xc, axis=-1, keepdims=True)
    return xc * lax.rsqrt(var + EPS)


def _sigmoid(x):
    return 1.0 / (1.0 + jnp.exp(-x))


def _log_sigmoid(x):
    return -(jnp.maximum(-x, 0.0) + jnp.log(1.0 + jnp.exp(-jnp.abs(x))))


def _gelu_tanh(x):
    c = math.sqrt(2.0 / math.pi)
    return 0.5 * x * (1.0 + jnp.tanh(c * (x + 0.044715 * (x * x * x))))


def _params(*sem):
    return pltpu.CompilerParams(dimension_semantics=sem, vmem_limit_bytes=VMEM_LIMIT)


def _full(shape):
    n = len(shape)
    return pl.BlockSpec(shape, lambda *_: (0,) * n, pipeline_mode=pl.Buffered(1))


def _s5_prep_kernel(are_ref, aim_ref, ldt_ref, bre_ref, bim_ref, abre_ref, abim_ref, bbre_ref, bbim_ref):
    a_re = are_ref[...]
    a_im = aim_ref[...]
    dt = jnp.exp(ldt_ref[...])
    mag = jnp.exp(a_re * dt)
    ab_re = mag * jnp.cos(a_im * dt)
    ab_im = mag * jnp.sin(a_im * dt)
    den = a_re * a_re + a_im * a_im
    x_re = ab_re - 1.0
    f_re = (x_re * a_re + ab_im * a_im) / den
    f_im = (ab_im * a_re - x_re * a_im) / den
    b_re = bre_ref[...]
    b_im = bim_ref[...]
    abre_ref[...] = ab_re
    abim_ref[...] = ab_im
    bbre_ref[...] = f_re * b_re - f_im * b_im
    bbim_ref[...] = f_re * b_im + f_im * b_re


def _s5_prep(a_re, a_im, log_dt, b_re, b_im):
    g, p, c = b_re.shape
    out = pl.pallas_call(
        _s5_prep_kernel,
        out_shape=(jax.ShapeDtypeStruct((g, 1, p), F32), jax.ShapeDtypeStruct((g, 1, p), F32),
                   jax.ShapeDtypeStruct((g, c, p), F32), jax.ShapeDtypeStruct((g, c, p), F32)),
        name="s5_prep",
    )(a_re.reshape(g, 1, p), a_im.reshape(g, 1, p), log_dt.reshape(g, 1, 1),
      jnp.swapaxes(b_re, 1, 2), jnp.swapaxes(b_im, 1, 2))
    return out


def _s5_matrices(ab_re, ab_im, bb_re, bb_im, c_re, c_im):
    g, c, p = bb_re.shape
    nsl = g * c // LANES
    gs = LANES // c
    eye = jnp.eye(gs, dtype=F32)

    def bmat(bb):
        t = bb.reshape(nsl, gs, c, p)
        w = jnp.einsum('mgcp,hg->mhcgp', t, eye)
        return w.reshape(nsl, gs * c, gs * p)

    def cmat(cc):
        t = cc.reshape(nsl, gs, c, p)
        w = jnp.einsum('mgcp,hg->mhpgc', t, eye)
        return w.reshape(nsl, gs * p, gs * c)

    b_all = jnp.stack([bmat(bb_re), bmat(bb_im)], axis=1).reshape(nsl * 2, gs * c, gs * p)
    c_all = jnp.concatenate([cmat(c_re), -cmat(c_im)], axis=1)
    nst = g * p // (SUBLANES * LANES)
    a_all = jnp.concatenate([ab_re.reshape(nst, SUBLANES, LANES), ab_im.reshape(nst, SUBLANES, LANES)], axis=0)
    return a_all, b_all.astype(BF16), c_all.astype(BF16)


def _rs_kernel(x_ref, gn_ref, win_ref, cos_ref, sin_ref, decay_ref, wq_ref, wkv_ref, gc_ref,
               s0_ref, h0_ref, rgn_ref, a_ref, bm_ref, cm_ref, dsk_ref, wglu_ref, bglu_ref, wout_ref,
               xo_ref, s_ref, h_ref, bu_ref, *, tl, nl):
    l = pl.program_id(1)
    qk_w = RET_HEADS * RET_DK
    nsl = S5_GROUPS * S5_GROUP // LANES
    nst = S5_GROUPS * S5_STATE // (SUBLANES * LANES)
    rows_per_slab = SUBLANES // (nsl // nst)

    @pl.when(l == 0)
    def _():
        s_ref[...] = s0_ref[...]
        h_ref[...] = h0_ref[...]

    x = x_ref[...]
    xn = _rms(x, gn_ref[...]).astype(BF16)
    proj = _dot(xn, win_ref[...])
    cos = cos_ref[...]
    sin = sin_ref[...]

    u_off = 4 * qk_w
    for m in range(nsl):
        um = proj[:, u_off + m * LANES: u_off + (m + 1) * LANES].astype(BF16)
        for ri in range(2):
            r = _dot(um, bm_ref[2 * m + ri])
            slab = ri * nst + m // (nsl // nst)
            for jl in range(rows_per_slab):
                j = rows_per_slab * (m % (nsl // nst)) + jl
                bu_ref[slab, pl.ds(j, tl, stride=SUBLANES), :] = r[:, jl * LANES:(jl + 1) * LANES]

    a = [a_ref[i] for i in range(2 * nst)]

    def scan_steps(carry, t0, t1):
        for t in range(t0, t1):
            rows = pl.ds(t * SUBLANES, SUBLANES)
            new = []
            for s in range(nst):
                hr, hi = carry[s], carry[nst + s]
                ar, ai = a[s], a[nst + s]
                nr = ar * hr - ai * hi + bu_ref[s, rows, :]
                ni = ar * hi + ai * hr + bu_ref[nst + s, rows, :]
                bu_ref[s, rows, :] = nr
                bu_ref[nst + s, rows, :] = ni
                new.append((nr, ni))
            carry = tuple(n[0] for n in new) + tuple(n[1] for n in new)
        return carry

    def rope_head(h):
        q = proj[:, h * RET_DK:(h + 1) * RET_DK]
        k = proj[:, qk_w + h * RET_DK: qk_w + (h + 1) * RET_DK]
        qr = q * cos + pltpu.roll(q, RET_DK // 2, 1) * sin
        kr = (k * cos + pltpu.roll(k, RET_DK // 2, 1) * sin) * (RET_DK ** -0.5)
        qb = qr.astype(BF16)
        return qb, kr, _dot_nt(qb, kr.astype(BF16)), _dot(qb, s_ref[h].astype(BF16))

    carry = tuple(h_ref[i] for i in range(2 * nst))
    per_head = tl // RET_HEADS
    pieces = []
    nxt = rope_head(0)
    for h in range(RET_HEADS):
        hs = slice(h * RET_DK, (h + 1) * RET_DK)
        qb, kr, sqk, cross = nxt
        vb = proj[:, 2 * qk_w + h * RET_DK: 2 * qk_w + (h + 1) * RET_DK].astype(BF16)
        g = proj[:, 3 * qk_w + h * RET_DK: 3 * qk_w + (h + 1) * RET_DK]
        o = _dot((sqk * decay_ref[h]).astype(BF16), vb) + cross * wq_ref[:, hs]
        if h + 1 < RET_HEADS:
            nxt = rope_head(h + 1)
        carry = scan_steps(carry, h * per_head, (h + 1) * per_head)
        s_ref[h] = gc_ref[h] * s_ref[h] + _dot_tn((kr * wkv_ref[:, hs]).astype(BF16), vb)
        mu = jnp.mean(o, axis=-1, keepdims=True)
        var = jnp.mean(o * o, axis=-1, keepdims=True) - mu * mu
        y = (o - mu) * lax.rsqrt(var + EPS) * rgn_ref[:, hs]
        pieces.append((g * _sigmoid(g) * y).astype(BF16))
    carry = scan_steps(carry, RET_HEADS * per_head, tl)
    for i in range(2 * nst):
        h_ref[i] = carry[i]
    out = x + _dot(jnp.concatenate(pieces, axis=1), wout_ref[0:qk_w, :])

    ys = []
    for m in range(nsl):
        parts = []
        for ri in range(2):
            slab = ri * nst + m // (nsl // nst)
            for jl in range(rows_per_slab):
                j = rows_per_slab * (m % (nsl // nst)) + jl
                parts.append(bu_ref[slab, pl.ds(j, tl, stride=SUBLANES), :])
        hcat = jnp.concatenate(parts, axis=1).astype(BF16)
        um = proj[:, u_off + m * LANES: u_off + (m + 1) * LANES]
        ym = _dot(hcat, cm_ref[m]) + dsk_ref[:, m * LANES:(m + 1) * LANES] * um
        ys.append(_gelu_tanh(ym))
    yg = jnp.concatenate(ys, axis=1)
    gate = _sigmoid(_dot(yg.astype(BF16), wglu_ref[...]) + bglu_ref[...])
    xo_ref[...] = out + _dot((yg * gate).astype(BF16), wout_ref[qk_w:, :])


def _rs_layer(x, st_ret, st_h, consts, w):
    b, L, d = x.shape
    tl = min(TOKEN_TILE, L)
    nl = L // tl
    cos, sin, decay, wq, wkv, gc = consts
    rs_in = w['rs_w_in'].shape[1]
    nst2 = st_h.shape[1]
    kern = functools.partial(_rs_kernel, tl=tl, nl=nl)
    tok = pl.BlockSpec((None, tl, d), lambda i, j: (i, j, 0))
    in_specs = [
        tok, _full((1, d)), _full((d, rs_in)),
        pl.BlockSpec((tl, LANES), lambda i, j: (j, 0)), pl.BlockSpec((tl, LANES), lambda i, j: (j, 0)),
        _full(decay.shape), _full(wq.shape), _full(wkv.shape), _full(gc.shape),
        pl.BlockSpec((None,) + st_ret.shape[1:], lambda i, j: (i, 0, 0, 0)),
        pl.BlockSpec((None,) + st_h.shape[1:], lambda i, j: (i, 0, 0, 0)),
        _full((1, RET_HEADS * RET_DK)), _full(w['s5_a'].shape), _full(w['s5_b'].shape), _full(w['s5_c'].shape),
        _full((1, w['s5_d'].shape[1])), _full(w['s5_w_glu'].shape), _full((1, w['s5_b_glu'].shape[1])),
        _full(w['rs_w_out'].shape),
    ]
    out_specs = [
        tok,
        pl.BlockSpec((None,) + st_ret.shape[1:], lambda i, j: (i, 0, 0, 0)),
        pl.BlockSpec((None,) + st_h.shape[1:], lambda i, j: (i, 0, 0, 0)),
    ]
    return pl.pallas_call(
        kern, grid=(b, nl), in_specs=in_specs, out_specs=out_specs,
        out_shape=(jax.ShapeDtypeStruct(x.shape, F32), jax.ShapeDtypeStruct(st_ret.shape, F32),
                   jax.ShapeDtypeStruct(st_h.shape, F32)),
        scratch_shapes=[pltpu.VMEM((nst2, tl * SUBLANES, LANES), F32)],
        compiler_params=_params("parallel", "arbitrary"), name="rs_mixer",
    )(x, w['norm_mix0'], w['rs_w_in'], cos, sin, decay, wq, wkv, gc, st_ret, st_h, w['ret_gn'],
      w['s5_a'], w['s5_b'], w['s5_c'], w['s5_d'], w['s5_w_glu'], w['s5_b_glu'], w['rs_w_out'])


def _ret_consts(L, tl, pos0):
    half = RET_DK // 2
    pos = pos0 + jnp.arange(L, dtype=jnp.int32)
    inv = ROPE_BASE ** (-jnp.arange(half, dtype=F32) / half)
    ang = pos.astype(F32)[:, None] * inv[None, :]
    cos = jnp.cos(ang)
    sin = jnp.sin(ang)
    cos2 = jnp.concatenate([cos, cos], axis=1)
    sin2 = jnp.concatenate([-sin, sin], axis=1)
    log_g = jnp.log1p(-jnp.exp2(-5.0 - jnp.arange(RET_HEADS, dtype=F32)))
    idx = jnp.arange(tl, dtype=F32)
    diff = idx[:, None] - idx[None, :]
    decay = jnp.where(diff[None] >= 0, jnp.exp(jnp.maximum(diff, 0.0)[None] * log_g[:, None, None]), 0.0)
    w_q = jnp.exp((idx + 1.0)[:, None] * log_g[None, :])
    w_kv = jnp.exp((tl - 1.0 - idx)[:, None] * log_g[None, :])
    g_c = jnp.exp(tl * log_g)
    rep = lambda t: jnp.repeat(t, RET_DK, axis=1)
    gc = jnp.broadcast_to(g_c[:, None, None], (RET_HEADS, 1, RET_DK))
    return cos2, sin2, decay, rep(w_q), rep(w_kv), gc


def _ml_a_kernel(x_ref, gn_ref, win_ref, cs_ref, cw_ref, cb_ref, wq_ref, wk_ref, wv_ref, wg_ref, bg_ref, skip_ref,
                 q_ref, kt_ref, v_ref, gates_ref, sz_ref, sx_ref, nc_ref, xs_ref, *, tl, nl):
    l = pl.program_id(1)
    inner = ML_HEADS * ML_DH
    pad = SUBLANES
    hist = ML_CONV - 1

    @pl.when(l == 0)
    def _():
        xs_ref[0:pad, :] = jnp.zeros((pad, inner), F32)
        xs_ref[pl.ds(pad - hist, hist), :] = cs_ref[...]

    xn = _rms(x_ref[...], gn_ref[...]).astype(BF16)
    xm = _dot(xn, win_ref[:, 0:inner])
    sz_ref[...] = _sigmoid(_dot(xn, win_ref[:, inner:2 * inner])).astype(sz_ref.dtype)
    xs_ref[pl.ds(pad, tl), :] = xm
    acc = cw_ref[hist:hist + 1, :] * xm
    for j in range(hist):
        acc = acc + cw_ref[j:j + 1, :] * xs_ref[pl.ds(pad - hist + j, tl), :]
    xc = acc + cb_ref[...]
    xc = xc * _sigmoid(xc)
    sx_ref[...] = (skip_ref[...] * xc).astype(sx_ref.dtype)
    gates = jnp.zeros((tl, LANES), F32) + bg_ref[...]
    for h in range(ML_HEADS):
        hs = slice(h * ML_DH, (h + 1) * ML_DH)
        xch = xc[:, hs].astype(BF16)
        xmh = xm[:, hs].astype(BF16)
        qh = _dot(xch, wq_ref[h]).astype(BF16)
        kf = _dot(xch, wk_ref[h])
        kh = kf.astype(BF16)
        vh = _dot(xmh, wv_ref[h]).astype(BF16)
        q_ref[:, hs] = qh
        kt_ref[h] = kf.T.astype(BF16)
        v_ref[:, hs] = vh
        gates = gates + _dot(qh, wg_ref[h * ML_DH:(h + 1) * ML_DH, :])
        gates = gates + _dot(kh, wg_ref[inner + h * ML_DH: inner + (h + 1) * ML_DH, :])
        gates = gates + _dot(vh, wg_ref[2 * inner + h * ML_DH: 2 * inner + (h + 1) * ML_DH, :])
    gates_ref[...] = gates

    @pl.when(l == nl - 1)
    def _():
        nc_ref[...] = xs_ref[pl.ds(pad + tl - hist, hist), :]

    xs_ref[0:pad, :] = xs_ref[pl.ds(tl, pad), :]


def _ml_a(x, conv_state, w):
    b, L, d = x.shape
    tl = min(PROJ_TILE, L)
    nl = L // tl
    inner = ML_HEADS * ML_DH
    kern = functools.partial(_ml_a_kernel, tl=tl, nl=nl)
    tok = lambda width: pl.BlockSpec((None, tl, width), lambda i, j: (i, j, 0))
    kt_spec = pl.BlockSpec((None, ML_HEADS, ML_DH, tl), lambda i, j: (i, 0, 0, j))
    cst = pl.BlockSpec((None, ML_CONV - 1, inner), lambda i, j: (i, 0, 0))
    in_specs = [tok(d), _full((1, d)), _full((d, 2 * inner)), cst, _full((ML_CONV, inner)), _full((1, inner)),
                _full(w['ml_wq'].shape), _full(w['ml_wk'].shape), _full(w['ml_wv'].shape),
                _full(w['ml_w_gates'].shape), _full((1, LANES)), _full((1, inner))]
    out_specs = [tok(inner), kt_spec, tok(inner), tok(LANES), tok(inner), tok(inner), cst]
    sds = jax.ShapeDtypeStruct
    out_shape = (sds((b, L, inner), BF16), sds((b, ML_HEADS, ML_DH, L), BF16), sds((b, L, inner), BF16),
                 sds((b, L, LANES), F32), sds((b, L, inner), BF16), sds((b, L, inner), BF16),
                 sds((b, ML_CONV - 1, inner), F32))
    return pl.pallas_call(
        kern, grid=(b, nl), in_specs=in_specs, out_specs=out_specs, out_shape=out_shape,
        scratch_shapes=[pltpu.VMEM((tl + SUBLANES, inner), F32)],
        compiler_params=_params("parallel", "arbitrary"), name="mlstm_proj",
    )(x, w['norm_mix1'], w['ml_w_in'], conv_state, w['ml_conv_w'], w['ml_conv_b'],
      w['ml_wq'], w['ml_wk'], w['ml_wv'], w['ml_w_gates'], w['ml_b_gates'], w['ml_skip'])


def _split3(x):
    hi = x.astype(BF16)
    r1 = x - hi.astype(F32)
    mid = r1.astype(BF16)
    lo = (r1 - mid.astype(F32)).astype(BF16)
    return hi, mid, lo


def _ml_b_kernel(*refs, tl, nl, has_state):
    if has_state:
        (x_ref, q_ref, kt_ref, v_ref, g_ref, sz_ref, sx_ref, c0_ref, n0_ref, m0_ref, gn_ref, wout_ref,
         xo_ref, c_ref, n_ref, m_ref, ncm_ref) = refs
    else:
        (x_ref, q_ref, kt_ref, v_ref, g_ref, sz_ref, sx_ref, gn_ref, wout_ref,
         xo_ref, c_ref, n_ref, m_ref, ncm_ref) = refs
    l = pl.program_id(1)
    scale = ML_DH ** -0.5

    @pl.when(l == 0)
    def _():
        if has_state:
            c_ref[...] = c0_ref[...]
            m_ref[...] = m0_ref[...]
            for h in range(ML_HEADS):
                ncm_ref[h] = jnp.broadcast_to(n0_ref[h:h + 1, :], (LANES, ML_DH)).T
        else:
            c_ref[...] = jnp.zeros(c_ref.shape, F32)
            m_ref[...] = jnp.zeros(m_ref.shape, F32)
            ncm_ref[...] = jnp.zeros(ncm_ref.shape, F32)

    gates = g_ref[...]
    gates_t = gates.T
    row = lax.broadcasted_iota(jnp.int32, (tl, tl), 0)
    col = lax.broadcasted_iota(jnp.int32, (tl, tl), 1)
    causal = row >= col
    lower = jnp.where(causal, 1.0, 0.0).astype(BF16)
    upper = jnp.where(row <= col, 1.0, 0.0).astype(BF16)
    ones = jnp.ones((tl, LANES), BF16)
    lf_c = _split3(_log_sigmoid(gates))
    b_f = _dot(lower, lf_c[0]) + _dot(lower, lf_c[1]) + _dot(lower, lf_c[2])
    lf_r = _split3(_log_sigmoid(gates_t[0:2 * ML_HEADS, :]))
    b_row_all = _dot(lf_r[0], upper) + _dot(lf_r[1], upper) + _dot(lf_r[2], upper)
    b_c = pltpu.roll(b_f, LANES - ML_HEADS, 1)
    lane = lax.broadcasted_iota(jnp.int32, (1, LANES), 1)
    m_prev_row = jnp.zeros((1, LANES), F32)
    for h in range(ML_HEADS):
        m_prev_row = jnp.where(lane == h, m_ref[h], m_prev_row)
    run = gates - b_c
    trow = lax.broadcasted_iota(jnp.int32, (tl, LANES), 0)
    shift = 1
    while shift < tl:
        run = jnp.maximum(run, jnp.where(trow >= shift, pltpu.roll(run, shift, 0), -jnp.inf))
        shift *= 2
    m_t_all = b_c + jnp.maximum(m_prev_row, run)
    s_inter_all = jnp.exp(b_c + m_prev_row - m_t_all)
    dmin_all = jnp.exp(-m_t_all)
    a_all = b_c - m_t_all + math.log(scale)
    m_new_row = m_t_all[tl - 1:tl, :]
    b_last_row = b_c[tl - 1:tl, :]
    dec_row = jnp.exp(b_last_row + m_prev_row - m_new_row)
    wsc_row = b_last_row - m_new_row

    hsl = [slice(h * ML_DH, (h + 1) * ML_DH) for h in range(ML_HEADS)]

    def state_dots(h):
        q = q_ref[:, hsl[h]]
        return (_dot(q, kt_ref[h]), _dot(q, c_ref[h].astype(BF16)), _dot(q, ncm_ref[h].astype(BF16))[:, 0:1])

    out = x_ref[...]
    nxt = state_dots(0)
    for h in range(ML_HEADS):
        hs = hsl[h]
        sqk, qc, qn = nxt
        v = v_ref[:, hs]
        r_row = gates_t[h:h + 1, :] - b_row_all[ML_HEADS + h: ML_HEADS + h + 1, :]
        s_inter = s_inter_all[:, h:h + 1]
        dec = dec_row[:, h:h + 1]
        wgt = jnp.exp(jnp.where(causal, a_all[:, h:h + 1] + r_row, -jnp.inf))
        qk = sqk * wgt
        num = s_inter * qc + _dot(qk.astype(BF16), v)
        den = s_inter * qn + jnp.sum(qk, axis=-1, keepdims=True)
        if h + 1 < ML_HEADS:
            nxt = state_dots(h + 1)
        dmax = jnp.maximum(jnp.abs(den), dmin_all[:, h:h + 1])
        xg = sz_ref[:, hs].astype(F32) * num
        mu = jnp.mean(xg, axis=-1, keepdims=True)
        var = jnp.mean(xg * xg, axis=-1, keepdims=True) - mu * mu
        y = (xg - mu) * lax.rsqrt(var + EPS * (dmax * dmax)) * gn_ref[:, hs] + sx_ref[:, hs].astype(F32)
        kwt = (kt_ref[h].astype(F32) * (scale * jnp.exp(r_row + wsc_row[:, h:h + 1]))).astype(BF16)
        c_ref[h] = dec * c_ref[h] + _dot(kwt, v)
        ncm_ref[h] = dec * ncm_ref[h] + _dot(kwt, ones)
        m_ref[h] = m_new_row[:, h:h + 1]
        out = out + _dot(y.astype(BF16), wout_ref[hs, :])
    xo_ref[...] = out

    @pl.when(l == nl - 1)
    def _():
        for h in range(ML_HEADS):
            n_ref[h:h + 1, :] = ncm_ref[h].T[0:1, :]


def _ml_b(x, q, kt, v, gates, sz, sx, state, w):
    b, L, d = x.shape
    tl = min(TOKEN_TILE, L)
    nl = L // tl
    inner = ML_HEADS * ML_DH
    has_state = state is not None
    kern = functools.partial(_ml_b_kernel, tl=tl, nl=nl, has_state=has_state)
    tok = lambda width: pl.BlockSpec((None, tl, width), lambda i, j: (i, j, 0))
    kt_spec = pl.BlockSpec((None, ML_HEADS, ML_DH, tl), lambda i, j: (i, 0, 0, j))
    c_spec = pl.BlockSpec((None, ML_HEADS, ML_DH, ML_DH), lambda i, j: (i, 0, 0, 0))
    n_spec = pl.BlockSpec((None, ML_HEADS, ML_DH), lambda i, j: (i, 0, 0))
    m_spec = pl.BlockSpec((None, ML_HEADS, 1, 1), lambda i, j: (i, 0, 0, 0))
    in_specs = [tok(d), tok(inner), kt_spec, tok(inner), tok(LANES), tok(inner), tok(inner)]
    args = [x, q, kt, v, gates, sz, sx]
    if has_state:
        in_specs += [c_spec, n_spec, m_spec]
        args += [state[0], state[1], state[2].reshape(b, ML_HEADS, 1, 1)]
    in_specs += [_full((1, inner)), _full((inner, d))]
    args += [w['ml_gn'], w['ml_w_out']]
    sds = jax.ShapeDtypeStruct
    out_shape = (sds(x.shape, F32), sds((b, ML_HEADS, ML_DH, ML_DH), F32), sds((b, ML_HEADS, ML_DH), F32),
                 sds((b, ML_HEADS, 1, 1), F32))
    xo, c, n, m = pl.pallas_call(
        kern, grid=(b, nl), in_specs=in_specs, out_specs=[tok(d), c_spec, n_spec, m_spec], out_shape=out_shape,
        scratch_shapes=[pltpu.VMEM((ML_HEADS, ML_DH, LANES), F32)],
        compiler_params=_params("parallel", "arbitrary"), name="mlstm_cell",
    )(*args)
    return xo, c, n, m.reshape(b, ML_HEADS)


def _memkv_kernel(m_ref, g_ref, wk_ref, wv_ref, k_ref, v_ref, kb_ref, vb_ref):
    nb, m_len = k_ref.shape[0], k_ref.shape[1]
    mn = _rms(m_ref[...], g_ref[...]).astype(BF16)
    k = _dot(mn, wk_ref[...])
    v = _dot(mn, wv_ref[...])
    for h in range(X_HEADS):
        kh = k[:, h * X_DH:(h + 1) * X_DH].reshape(nb, m_len, X_DH)
        vh = v[:, h * X_DH:(h + 1) * X_DH].reshape(nb, m_len, X_DH)
        k_ref[:, :, h, :] = kh
        v_ref[:, :, h, :] = vh
        kb_ref[:, h] = kh.astype(BF16)
        vb_ref[:, h] = vh.astype(BF16)


def _memkv(mem, norm_mem, wk, wv):
    b, m_len, d = mem.shape
    depth = wk.shape[0]
    rows = b * m_len
    tm = min(MEM_ROW_TILE, rows)
    nb = tm // m_len
    mem2 = mem.reshape(rows, d)
    wspec = pl.BlockSpec((None, d, d), lambda i, j: (i, 0, 0))
    ospec = pl.BlockSpec((None, nb, m_len, X_HEADS, X_DH), lambda i, j: (i, j, 0, 0, 0))
    oshape = jax.ShapeDtypeStruct((depth, b, m_len, X_HEADS, X_DH), F32)
    bspec = pl.BlockSpec((None, nb, X_HEADS, m_len, X_DH), lambda i, j: (i, j, 0, 0, 0))
    bshape = jax.ShapeDtypeStruct((depth, b, X_HEADS, m_len, X_DH), BF16)
    return pl.pallas_call(
        _memkv_kernel, grid=(depth, rows // tm),
        in_specs=[pl.BlockSpec((tm, d), lambda i, j: (j, 0)), pl.BlockSpec((None, 1, d), lambda i, j: (i, 0, 0)),
                  wspec, wspec],
        out_specs=[ospec, ospec, bspec, bspec],
        out_shape=(oshape, oshape, bshape, bshape),
        compiler_params=_params("parallel", "parallel"), name="memory_kv",
    )(mem2, norm_mem.reshape(depth, 1, d), wk, wv)


def _xattn_kernel(x_ref, g_ref, wq_ref, mk_ref, mv_ref, wo_ref, o_ref):
    x = x_ref[...]
    xn = _rms(x, g_ref[...]).astype(BF16)
    q = _dot(xn, wq_ref[...])
    hsl = [slice(h * X_DH, (h + 1) * X_DH) for h in range(X_HEADS)]

    def scores(h):
        return _dot_nt(q[:, hsl[h]].astype(BF16), mk_ref[h])

    out = x
    nxt = scores(0)
    for h in range(X_HEADS):
        s = nxt * (X_DH ** -0.5)
        if h + 1 < X_HEADS:
            nxt = scores(h + 1)
        e = jnp.exp(s - jnp.max(s, axis=-1, keepdims=True))
        pv = _dot(e.astype(BF16), mv_ref[h])
        oh = pv * (1.0 / jnp.sum(e, axis=-1, keepdims=True))
        out = out + _dot(oh.astype(BF16), wo_ref[hsl[h], :])
    o_ref[...] = out


def _xattn(x, layer, g, wq, mk, mv, wo):
    b, L, d = x.shape
    tl = min(XATTN_TILE, L)
    m_len = mk.shape[3]
    tok = pl.BlockSpec((None, tl, d), lambda i, j: (i, j, 0))
    mem = pl.BlockSpec((None, None, X_HEADS, m_len, X_DH), lambda i, j: (layer, i, 0, 0, 0))
    wsp = pl.BlockSpec((None, d, d), lambda i, j: (layer, 0, 0))
    gsp = pl.BlockSpec((None, 1, d), lambda i, j: (layer, 0, 0))
    return pl.pallas_call(
        _xattn_kernel, grid=(b, L // tl),
        in_specs=[tok, gsp, wsp, mem, mem, wsp],
        out_specs=tok, out_shape=jax.ShapeDtypeStruct(x.shape, F32),
        compiler_params=_params("parallel", "arbitrary"), name="cross_attn",
    )(x, g, wq, mk, mv, wo)


def _mlp_kernel(*refs, final):
    if final:
        x_ref, g_ref, wu_ref, wd_ref, gf_ref, o_ref, xn_ref = refs
    else:
        x_ref, g_ref, wu_ref, wd_ref, o_ref, xn_ref = refs
    j = pl.program_id(1)

    @pl.when(j == 0)
    def _():
        x = x_ref[...]
        xn_ref[...] = _rms(x, g_ref[...]).astype(BF16)
        o_ref[...] = x

    hdn = jnp.maximum(_dot(xn_ref[...], wu_ref[...]), 0.0)
    o_ref[...] += _dot((hdn * hdn).astype(BF16), wd_ref[...])

    if final:
        @pl.when(j == pl.num_programs(1) - 1)
        def _():
            o_ref[...] = _rms(o_ref[...], gf_ref[...])


def _mlp(x, layer, g, wu, wd, gf=None):
    shape = x.shape
    d = shape[-1]
    x2 = x.reshape(-1, d)
    rows = x2.shape[0]
    ff = wu.shape[2]
    tm = min(MLP_ROW_TILE, rows)
    tf = min(MLP_FF_TILE, ff)
    final = gf is not None
    in_specs = [pl.BlockSpec((tm, d), lambda i, j: (i, 0)), pl.BlockSpec((None, 1, d), lambda i, j: (layer, 0, 0)),
                pl.BlockSpec((None, d, tf), lambda i, j: (layer, 0, j)),
                pl.BlockSpec((None, tf, d), lambda i, j: (layer, j, 0))]
    args = [x2, g, wu, wd]
    if final:
        in_specs.append(_full((1, d)))
        args.append(gf)
    out = pl.pallas_call(
        functools.partial(_mlp_kernel, final=final), grid=(rows // tm, ff // tf),
        in_specs=in_specs, out_specs=pl.BlockSpec((tm, d), lambda i, j: (i, 0)),
        out_shape=jax.ShapeDtypeStruct(x2.shape, F32),
        scratch_shapes=[pltpu.VMEM((tm, d), BF16)],
        compiler_params=_params("parallel", "arbitrary"), name="mlp",
    )(*args)
    return out.reshape(shape)


def _trunk(x, pos0, st_ret, st_re, st_im, ml_state, conv_state, mem_k, mem_v, w):
    b, L, d = x.shape
    tl = min(TOKEN_TILE, L)
    nst = S5_GROUPS * S5_STATE // (SUBLANES * LANES)
    st_h = jnp.concatenate([st_re.reshape(b, nst, SUBLANES, LANES), st_im.reshape(b, nst, SUBLANES, LANES)], axis=1)
    x, s_new, h_new = _rs_layer(x, st_ret, st_h, _ret_consts(L, tl, pos0), w)
    hr = h_new[:, :nst].reshape(b, S5_GROUPS, S5_STATE)
    hi = h_new[:, nst:].reshape(b, S5_GROUPS, S5_STATE)
    x = _xattn(x, 0, w['norm_cross'], w['x_wq'], mem_k, mem_v, w['x_wo'])
    x = _mlp(x, 0, w['norm_mlp'], w['mlp_w_up'], w['mlp_w_down'])
    q, kt, v, gates, sz, sx, new_conv = _ml_a(x, conv_state, w)
    x, cf, nf, mf = _ml_b(x, q, kt, v, gates, sz, sx, ml_state, w)
    x = _xattn(x, 1, w['norm_cross'], w['x_wq'], mem_k, mem_v, w['x_wo'])
    y = _mlp(x, 1, w['norm_mlp'], w['mlp_w_up'], w['mlp_w_down'], w['norm_final'])
    return y, s_new[None], hr[None], hi[None], cf[None], nf[None], mf[None], new_conv[None]


def kernel(x_prompt, x_sample, mem_prompt, state_ret, state_s5_re, state_s5_im, state_mlstm_c, state_mlstm_n, state_mlstm_m, cache_mlstm_conv, cache_mem_k, cache_mem_v, norm_mix, norm_cross, norm_mem, norm_mlp, norm_final, rs_w_in, rs_w_out, ret_gn, s5_a_re, s5_a_im, s5_log_dt, s5_b_re, s5_b_im, s5_c_re, s5_c_im, s5_d, s5_w_glu, s5_b_glu, ml_w_in, ml_conv_w, ml_conv_b, ml_wq, ml_wk, ml_wv, ml_w_gates, ml_b_gates, ml_gn, ml_skip, ml_w_out, x_wq, x_wk, x_wv, x_wo, mlp_w_up, mlp_w_down):
    d = x_prompt.shape[-1]
    bp = x_prompt.shape[0]
    bf = lambda t: t.astype(BF16)
    row = lambda t: t.reshape(1, -1).astype(F32)
    ab_re, ab_im, bb_re, bb_im = _s5_prep(s5_a_re[0], s5_a_im[0], s5_log_dt[0], s5_b_re[0], s5_b_im[0])
    s5_a, s5_b, s5_c = _s5_matrices(ab_re, ab_im, bb_re, bb_im, s5_c_re[0], s5_c_im[0])
    n_gate = ml_w_gates.shape[-1]
    w = {
        'norm_mix0': row(norm_mix[0]), 'norm_mix1': row(norm_mix[1]),
        'norm_cross': norm_cross.reshape(-1, 1, d), 'norm_mlp': norm_mlp.reshape(-1, 1, d),
        'norm_final': row(norm_final),
        'rs_w_in': bf(rs_w_in[0]), 'rs_w_out': bf(rs_w_out[0]), 'ret_gn': row(ret_gn[0]),
        's5_a': s5_a, 's5_b': s5_b, 's5_c': s5_c, 's5_d': row(s5_d[0]), 's5_w_glu': bf(s5_w_glu[0]),
        's5_b_glu': row(s5_b_glu[0]),
        'ml_w_in': bf(ml_w_in[0]), 'ml_conv_w': ml_conv_w[0], 'ml_conv_b': row(ml_conv_b[0]),
        'ml_wq': bf(ml_wq[0]), 'ml_wk': bf(ml_wk[0]), 'ml_wv': bf(ml_wv[0]),
        'ml_w_gates': bf(jnp.pad(ml_w_gates[0], ((0, 0), (0, LANES - n_gate)))),
        'ml_b_gates': jnp.pad(ml_b_gates[0], (0, LANES - n_gate)).reshape(1, LANES),
        'ml_gn': row(ml_gn[0]), 'ml_skip': row(ml_skip[0]), 'ml_w_out': bf(ml_w_out[0]),
        'x_wq': bf(x_wq), 'x_wo': bf(x_wo), 'mlp_w_up': bf(mlp_w_up), 'mlp_w_down': bf(mlp_w_down),
    }
    mk_p, mv_p, mkb_p, mvb_p = _memkv(mem_prompt, norm_mem, bf(x_wk), bf(x_wv))
    zeros = lambda *s: jnp.zeros(s, F32)
    out_p = _trunk(x_prompt, 0, zeros(bp, RET_HEADS, RET_DK, RET_DK), zeros(bp, S5_GROUPS, S5_STATE),
                   zeros(bp, S5_GROUPS, S5_STATE), None, zeros(bp, ML_CONV - 1, ML_HEADS * ML_DH), mkb_p, mvb_p, w)
    out_s = _trunk(x_sample, PAST_LEN, state_ret[0], state_s5_re[0], state_s5_im[0],
                   (state_mlstm_c[0], state_mlstm_n[0], state_mlstm_m[0]), cache_mlstm_conv[0],
                   bf(jnp.swapaxes(cache_mem_k, 2, 3)), bf(jnp.swapaxes(cache_mem_v, 2, 3)), w)
    return (out_p[0], out_s[0]) + tuple(out_p[1:]) + (mk_p, mv_p) + tuple(out_s[1:])
```

```python
import functools
import math

import jax
import jax.numpy as jnp
from jax import lax
from jax.experimental import pallas as pl
from jax.experimental.pallas import tpu as pltpu

F32 = jnp.float32
BF16 = jnp.bfloat16

EPS = 1e-6
ROPE_BASE = 10000.0
PAST_LEN = 4096
RET_HEADS = 4
RET_DK = 128
S5_GROUPS = 32
S5_GROUP = 16
S5_STATE = 64
ML_HEADS = 4
ML_DH = 512
ML_CONV = 4
X_HEADS = 4
X_DH = 256
LANES = 128
SUBLANES = 8
VMEM_LIMIT = 56 * 1024 * 1024
CHUNK = 256
MIXER_TILE = 512
CELL_TILE = 512
PROJ_TILE = 512
XATTN_TILE = 1024
MLP_ROW_TILE = 1024
MLP_FF_TILE = 2048
MEM_ROW_TILE = 1024


def _dot(a, b):
    return jnp.dot(a, b, preferred_element_type=F32)


def _dot_nt(a, b):
    return lax.dot_general(a, b, (((1,), (1,)), ((), ())), preferred_element_type=F32)


def _dot_tn(a, b):
    return lax.dot_general(a, b, (((0,), (0,)), ((), ())), preferred_element_type=F32)


def _rms(x, g):
    y = x * lax.rsqrt(jnp.mean(x * x, axis=-1, keepdims=True) + EPS)
    return y * g


def _layernorm(x):
    mu = jnp.mean(x, axis=-1, keepdims=True)
    xc = x - mu
    var = jnp.mean(xc * xc, axis=-1, keepdims=True)
    return xc * lax.rsqrt(var + EPS)


def _sigmoid(x):
    return 1.0 / (1.0 + jnp.exp(-x))


def _log_sigmoid(x):
    return -(jnp.maximum(-x, 0.0) + jnp.log(1.0 + jnp.exp(-jnp.abs(x))))


def _gelu_tanh(x):
    c = math.sqrt(2.0 / math.pi)
    return 0.5 * x * (1.0 + jnp.tanh(c * (x + 0.044715 * (x * x * x))))


def _params(*sem):
    return pltpu.CompilerParams(dimension_semantics=sem, vmem_limit_bytes=VMEM_LIMIT)


def _full(shape):
    n = len(shape)
    return pl.BlockSpec(shape, lambda *_: (0,) * n, pipeline_mode=pl.Buffered(1))


def _s5_prep_kernel(are_ref, aim_ref, ldt_ref, bre_ref, bim_ref, abre_ref, abim_ref, bbre_ref, bbim_ref):
    a_re = are_ref[...]
    a_im = aim_ref[...]
    dt = jnp.exp(ldt_ref[...])
    mag = jnp.exp(a_re * dt)
    ab_re = mag * jnp.cos(a_im * dt)
    ab_im = mag * jnp.sin(a_im * dt)
    den = a_re * a_re + a_im * a_im
    x_re = ab_re - 1.0
    f_re = (x_re * a_re + ab_im * a_im) / den
    f_im = (ab_im * a_re - x_re * a_im) / den
    b_re = bre_ref[...]
    b_im = bim_ref[...]
    abre_ref[...] = ab_re
    abim_ref[...] = ab_im
    bbre_ref[...] = f_re * b_re - f_im * b_im
    bbim_ref[...] = f_re * b_im + f_im * b_re


def _s5_prep(a_re, a_im, log_dt, b_re, b_im):
    g, p, c = b_re.shape
    out = pl.pallas_call(
        _s5_prep_kernel,
        out_shape=(jax.ShapeDtypeStruct((g, 1, p), F32), jax.ShapeDtypeStruct((g, 1, p), F32),
                   jax.ShapeDtypeStruct((g, c, p), F32), jax.ShapeDtypeStruct((g, c, p), F32)),
        name="s5_prep",
    )(a_re.reshape(g, 1, p), a_im.reshape(g, 1, p), log_dt.reshape(g, 1, 1),
      jnp.swapaxes(b_re, 1, 2), jnp.swapaxes(b_im, 1, 2))
    return out


def _s5_matrices(ab_re, ab_im, bb_re, bb_im, c_re, c_im):
    g, c, p = bb_re.shape
    nsl = g * c // LANES
    gs = LANES // c
    eye = jnp.eye(gs, dtype=F32)

    def bmat(bb):
        t = bb.reshape(nsl, gs, c, p)
        w = jnp.einsum('mgcp,hg->mhcgp', t, eye)
        return w.reshape(nsl, gs * c, gs * p)

    def cmat(cc):
        t = cc.reshape(nsl, gs, c, p)
        w = jnp.einsum('mgcp,hg->mhpgc', t, eye)
        return w.reshape(nsl, gs * p, gs * c)

    b_all = jnp.stack([bmat(bb_re), bmat(bb_im)], axis=1).reshape(nsl * 2, gs * c, gs * p)
    c_all = jnp.concatenate([cmat(c_re), -cmat(c_im)], axis=1)
    nst = g * p // (SUBLANES * LANES)
    a_all = jnp.concatenate([ab_re.reshape(nst, SUBLANES, LANES), ab_im.reshape(nst, SUBLANES, LANES)], axis=0)
    return a_all, b_all.astype(BF16), c_all.astype(BF16)


def _rs_kernel(*refs, tl, chunk):
    s0_ref, h0_ref = refs[9], refs[10]
    s_ref, h_ref, bu_ref = refs[20], refs[21], refs[22]

    @pl.when(pl.program_id(1) == 0)
    def _():
        s_ref[...] = s0_ref[...]
        h_ref[...] = h0_ref[...]

    for sub in range(tl // chunk):
        _rs_chunk(*refs[:22], bu_ref.at[sub], rows=slice(sub * chunk, (sub + 1) * chunk), chunk=chunk)


def _rs_chunk(x_ref, gn_ref, win_ref, cos_ref, sin_ref, decay_ref, wq_ref, wkv_ref, gc_ref,
              s0_ref, h0_ref, rgn_ref, a_ref, bm_ref, cm_ref, dsk_ref, wglu_ref, bglu_ref, wout_ref,
              xo_ref, s_ref, h_ref, bu, *, rows, chunk):
    qk_w = RET_HEADS * RET_DK
    nsl = S5_GROUPS * S5_GROUP // LANES
    nst = S5_GROUPS * S5_STATE // (SUBLANES * LANES)
    rows_per_slab = SUBLANES // (nsl // nst)

    x = x_ref[rows, :]
    xn = _rms(x, gn_ref[...]).astype(BF16)
    proj = _dot(xn, win_ref[...])
    cos = cos_ref[rows, :]
    sin = sin_ref[rows, :]

    u_off = 4 * qk_w
    for m in range(nsl):
        um = proj[:, u_off + m * LANES: u_off + (m + 1) * LANES].astype(BF16)
        for ri in range(2):
            r = _dot(um, bm_ref[2 * m + ri])
            slab = ri * nst + m // (nsl // nst)
            for jl in range(rows_per_slab):
                j = rows_per_slab * (m % (nsl // nst)) + jl
                bu[slab, pl.ds(j, chunk, stride=SUBLANES), :] = r[:, jl * LANES:(jl + 1) * LANES]

    a = [a_ref[i] for i in range(2 * nst)]

    def scan_steps(carry, t0, t1):
        for t in range(t0, t1):
            trow = pl.ds(t * SUBLANES, SUBLANES)
            new = []
            for s in range(nst):
                hr, hi = carry[s], carry[nst + s]
                ar, ai = a[s], a[nst + s]
                nr = ar * hr - ai * hi + bu[s, trow, :]
                ni = ar * hi + ai * hr + bu[nst + s, trow, :]
                bu[s, trow, :] = nr
                bu[nst + s, trow, :] = ni
                new.append((nr, ni))
            carry = tuple(n[0] for n in new) + tuple(n[1] for n in new)
        return carry

    def rope_head(h):
        q = proj[:, h * RET_DK:(h + 1) * RET_DK]
        k = proj[:, qk_w + h * RET_DK: qk_w + (h + 1) * RET_DK]
        qr = q * cos + pltpu.roll(q, RET_DK // 2, 1) * sin
        kr = (k * cos + pltpu.roll(k, RET_DK // 2, 1) * sin) * (RET_DK ** -0.5)
        qb = qr.astype(BF16)
        return qb, kr, _dot_nt(qb, kr.astype(BF16)), _dot(qb, s_ref[h].astype(BF16))

    carry = tuple(h_ref[i] for i in range(2 * nst))
    per_head = chunk // RET_HEADS
    pieces = []
    nxt = rope_head(0)
    for h in range(RET_HEADS):
        hs = slice(h * RET_DK, (h + 1) * RET_DK)
        qb, kr, sqk, cross = nxt
        vb = proj[:, 2 * qk_w + h * RET_DK: 2 * qk_w + (h + 1) * RET_DK].astype(BF16)
        g = proj[:, 3 * qk_w + h * RET_DK: 3 * qk_w + (h + 1) * RET_DK]
        o = _dot((sqk * decay_ref[h]).astype(BF16), vb) + cross * wq_ref[:, hs]
        if h + 1 < RET_HEADS:
            nxt = rope_head(h + 1)
        carry = scan_steps(carry, h * per_head, (h + 1) * per_head)
        s_ref[h] = gc_ref[h] * s_ref[h] + _dot_tn((kr * wkv_ref[:, hs]).astype(BF16), vb)
        mu = jnp.mean(o, axis=-1, keepdims=True)
        var = jnp.mean(o * o, axis=-1, keepdims=True) - mu * mu
        y = (o - mu) * lax.rsqrt(var + EPS) * rgn_ref[:, hs]
        pieces.append((g * _sigmoid(g) * y).astype(BF16))
    carry = scan_steps(carry, RET_HEADS * per_head, chunk)
    for i in range(2 * nst):
        h_ref[i] = carry[i]
    out = x + _dot(jnp.concatenate(pieces, axis=1), wout_ref[0:qk_w, :])

    ys = []
    for m in range(nsl):
        parts = []
        for ri in range(2):
            slab = ri * nst + m // (nsl // nst)
            for jl in range(rows_per_slab):
                j = rows_per_slab * (m % (nsl // nst)) + jl
                parts.append(bu[slab, pl.ds(j, chunk, stride=SUBLANES), :])
        hcat = jnp.concatenate(parts, axis=1).astype(BF16)
        um = proj[:, u_off + m * LANES: u_off + (m + 1) * LANES]
        ym = _dot(hcat, cm_ref[m]) + dsk_ref[:, m * LANES:(m + 1) * LANES] * um
        ys.append(_gelu_tanh(ym))
    yg = jnp.concatenate(ys, axis=1)
    gate = _sigmoid(_dot(yg.astype(BF16), wglu_ref[...]) + bglu_ref[...])
    xo_ref[rows, :] = out + _dot((yg * gate).astype(BF16), wout_ref[qk_w:, :])


def _rs_layer(x, st_ret, st_h, consts, w):
    b, L, d = x.shape
    chunk = min(CHUNK, L)
    tl = min(MIXER_TILE, L)
    nl = L // tl
    cos, sin, decay, wq, wkv, gc = consts
    rs_in = w['rs_w_in'].shape[1]
    nst2 = st_h.shape[1]
    kern = functools.partial(_rs_kernel, tl=tl, chunk=chunk)
    tok = pl.BlockSpec((None, tl, d), lambda i, j: (i, j, 0))
    in_specs = [
        tok, _full((1, d)), _full((d, rs_in)),
        pl.BlockSpec((tl, LANES), lambda i, j: (j, 0)), pl.BlockSpec((tl, LANES), lambda i, j: (j, 0)),
        _full(decay.shape), _full(wq.shape), _full(wkv.shape), _full(gc.shape),
        pl.BlockSpec((None,) + st_ret.shape[1:], lambda i, j: (i, 0, 0, 0)),
        pl.BlockSpec((None,) + st_h.shape[1:], lambda i, j: (i, 0, 0, 0)),
        _full((1, RET_HEADS * RET_DK)), _full(w['s5_a'].shape), _full(w['s5_b'].shape), _full(w['s5_c'].shape),
        _full((1, w['s5_d'].shape[1])), _full(w['s5_w_glu'].shape), _full((1, w['s5_b_glu'].shape[1])),
        _full(w['rs_w_out'].shape),
    ]
    out_specs = [
        tok,
        pl.BlockSpec((None,) + st_ret.shape[1:], lambda i, j: (i, 0, 0, 0)),
        pl.BlockSpec((None,) + st_h.shape[1:], lambda i, j: (i, 0, 0, 0)),
    ]
    return pl.pallas_call(
        kern, grid=(b, nl), in_specs=in_specs, out_specs=out_specs,
        out_shape=(jax.ShapeDtypeStruct(x.shape, F32), jax.ShapeDtypeStruct(st_ret.shape, F32),
                   jax.ShapeDtypeStruct(st_h.shape, F32)),
        scratch_shapes=[pltpu.VMEM((tl // chunk, nst2, chunk * SUBLANES, LANES), F32)],
        compiler_params=_params("parallel", "arbitrary"), name="rs_mixer",
    )(x, w['norm_mix0'], w['rs_w_in'], cos, sin, decay, wq, wkv, gc, st_ret, st_h, w['ret_gn'],
      w['s5_a'], w['s5_b'], w['s5_c'], w['s5_d'], w['s5_w_glu'], w['s5_b_glu'], w['rs_w_out'])


def _ret_consts(L, tl, pos0):
    half = RET_DK // 2
    pos = pos0 + jnp.arange(L, dtype=jnp.int32)
    inv = ROPE_BASE ** (-jnp.arange(half, dtype=F32) / half)
    ang = pos.astype(F32)[:, None] * inv[None, :]
    cos = jnp.cos(ang)
    sin = jnp.sin(ang)
    cos2 = jnp.concatenate([cos, cos], axis=1)
    sin2 = jnp.concatenate([-sin, sin], axis=1)
    log_g = jnp.log1p(-jnp.exp2(-5.0 - jnp.arange(RET_HEADS, dtype=F32)))
    idx = jnp.arange(tl, dtype=F32)
    diff = idx[:, None] - idx[None, :]
    decay = jnp.where(diff[None] >= 0, jnp.exp(jnp.maximum(diff, 0.0)[None] * log_g[:, None, None]), 0.0)
    w_q = jnp.exp((idx + 1.0)[:, None] * log_g[None, :])
    w_kv = jnp.exp((tl - 1.0 - idx)[:, None] * log_g[None, :])
    g_c = jnp.exp(tl * log_g)
    rep = lambda t: jnp.repeat(t, RET_DK, axis=1)
    gc = jnp.broadcast_to(g_c[:, None, None], (RET_HEADS, 1, RET_DK))
    return cos2, sin2, decay, rep(w_q), rep(w_kv), gc


def _ml_a_kernel(x_ref, gn_ref, win_ref, cs_ref, cw_ref, cb_ref, wq_ref, wk_ref, wv_ref, wg_ref, bg_ref, skip_ref,
                 q_ref, kt_ref, v_ref, gates_ref, sz_ref, sx_ref, nc_ref, xs_ref, *, tl, nl):
    l = pl.program_id(1)
    inner = ML_HEADS * ML_DH
    pad = SUBLANES
    hist = ML_CONV - 1

    @pl.when(l == 0)
    def _():
        xs_ref[0:pad, :] = jnp.zeros((pad, inner), F32)
        xs_ref[pl.ds(pad - hist, hist), :] = cs_ref[...]

    xn = _rms(x_ref[...], gn_ref[...]).astype(BF16)
    xm = _dot(xn, win_ref[:, 0:inner])
    sz_ref[...] = _sigmoid(_dot(xn, win_ref[:, inner:2 * inner])).astype(sz_ref.dtype)
    xs_ref[pl.ds(pad, tl), :] = xm
    acc = cw_ref[hist:hist + 1, :] * xm
    for j in range(hist):
        acc = acc + cw_ref[j:j + 1, :] * xs_ref[pl.ds(pad - hist + j, tl), :]
    xc = acc + cb_ref[...]
    xc = xc * _sigmoid(xc)
    sx_ref[...] = (skip_ref[...] * xc).astype(sx_ref.dtype)
    gates = jnp.zeros((tl, LANES), F32) + bg_ref[...]
    for h in range(ML_HEADS):
        hs = slice(h * ML_DH, (h + 1) * ML_DH)
        xch = xc[:, hs].astype(BF16)
        xmh = xm[:, hs].astype(BF16)
        qh = _dot(xch, wq_ref[h]).astype(BF16)
        kf = _dot(xch, wk_ref[h])
        kh = kf.astype(BF16)
        vh = _dot(xmh, wv_ref[h]).astype(BF16)
        q_ref[:, hs] = qh
        kt_ref[h] = kf.T.astype(BF16)
        v_ref[:, hs] = vh
        gates = gates + _dot(qh, wg_ref[h * ML_DH:(h + 1) * ML_DH, :])
        gates = gates + _dot(kh, wg_ref[inner + h * ML_DH: inner + (h + 1) * ML_DH, :])
        gates = gates + _dot(vh, wg_ref[2 * inner + h * ML_DH: 2 * inner + (h + 1) * ML_DH, :])
    gates_ref[...] = gates

    @pl.when(l == nl - 1)
    def _():
        nc_ref[...] = xs_ref[pl.ds(pad + tl - hist, hist), :]

    xs_ref[0:pad, :] = xs_ref[pl.ds(tl, pad), :]


def _ml_a(x, conv_state, w):
    b, L, d = x.shape
    tl = min(PROJ_TILE, L)
    nl = L // tl
    inner = ML_HEADS * ML_DH
    kern = functools.partial(_ml_a_kernel, tl=tl, nl=nl)
    tok = lambda width: pl.BlockSpec((None, tl, width), lambda i, j: (i, j, 0))
    kt_spec = pl.BlockSpec((None, ML_HEADS, ML_DH, tl), lambda i, j: (i, 0, 0, j))
    cst = pl.BlockSpec((None, ML_CONV - 1, inner), lambda i, j: (i, 0, 0))
    in_specs = [tok(d), _full((1, d)), _full((d, 2 * inner)), cst, _full((ML_CONV, inner)), _full((1, inner)),
                _full(w['ml_wq'].shape), _full(w['ml_wk'].shape), _full(w['ml_wv'].shape),
                _full(w['ml_w_gates'].shape), _full((1, LANES)), _full((1, inner))]
    out_specs = [tok(inner), kt_spec, tok(inner), tok(LANES), tok(inner), tok(inner), cst]
    sds = jax.ShapeDtypeStruct
    out_shape = (sds((b, L, inner), BF16), sds((b, ML_HEADS, ML_DH, L), BF16), sds((b, L, inner), BF16),
                 sds((b, L, LANES), F32), sds((b, L, inner), BF16), sds((b, L, inner), BF16),
                 sds((b, ML_CONV - 1, inner), F32))
    return pl.pallas_call(
        kern, grid=(b, nl), in_specs=in_specs, out_specs=out_specs, out_shape=out_shape,
        scratch_shapes=[pltpu.VMEM((tl + SUBLANES, inner), F32)],
        compiler_params=_params("parallel", "arbitrary"), name="mlstm_proj",
    )(x, w['norm_mix1'], w['ml_w_in'], conv_state, w['ml_conv_w'], w['ml_conv_b'],
      w['ml_wq'], w['ml_wk'], w['ml_wv'], w['ml_w_gates'], w['ml_b_gates'], w['ml_skip'])


def _split3(x):
    hi = x.astype(BF16)
    r1 = x - hi.astype(F32)
    mid = r1.astype(BF16)
    lo = (r1 - mid.astype(F32)).astype(BF16)
    return hi, mid, lo


def _ml_b_kernel(*refs, tl, chunk, nl, has_state):
    if has_state:
        (x_ref, q_ref, kt_ref, v_ref, g_ref, sz_ref, sx_ref, c0_ref, n0_ref, m0_ref, gn_ref, wout_ref,
         xo_ref, c_ref, n_ref, m_ref, ncm_ref) = refs
    else:
        (x_ref, q_ref, kt_ref, v_ref, g_ref, sz_ref, sx_ref, gn_ref, wout_ref,
         xo_ref, c_ref, n_ref, m_ref, ncm_ref) = refs
    l = pl.program_id(1)

    @pl.when(l == 0)
    def _():
        if has_state:
            c_ref[...] = c0_ref[...]
            m_ref[...] = m0_ref[...]
            for h in range(ML_HEADS):
                ncm_ref[h] = jnp.broadcast_to(n0_ref[h:h + 1, :], (LANES, ML_DH)).T
        else:
            c_ref[...] = jnp.zeros(c_ref.shape, F32)
            m_ref[...] = jnp.zeros(m_ref.shape, F32)
            ncm_ref[...] = jnp.zeros(ncm_ref.shape, F32)

    for sub in range(tl // chunk):
        rows = slice(sub * chunk, (sub + 1) * chunk)
        _ml_b_chunk(x_ref.at[rows], q_ref.at[rows], kt_ref.at[:, :, rows], v_ref.at[rows], g_ref.at[rows],
                    sz_ref.at[rows], sx_ref.at[rows], gn_ref, wout_ref, xo_ref.at[rows], c_ref, m_ref, ncm_ref,
                    tl=chunk)

    @pl.when(l == nl - 1)
    def _():
        for h in range(ML_HEADS):
            n_ref[h:h + 1, :] = ncm_ref[h].T[0:1, :]


def _ml_b_chunk(x_ref, q_ref, kt_ref, v_ref, g_ref, sz_ref, sx_ref, gn_ref, wout_ref, xo_ref, c_ref, m_ref, ncm_ref,
                *, tl):
    scale = ML_DH ** -0.5
    gates = g_ref[...]
    gates_t = gates.T
    row = lax.broadcasted_iota(jnp.int32, (tl, tl), 0)
    col = lax.broadcasted_iota(jnp.int32, (tl, tl), 1)
    causal = row >= col
    lower = jnp.where(causal, 1.0, 0.0).astype(BF16)
    upper = jnp.where(row <= col, 1.0, 0.0).astype(BF16)
    ones = jnp.ones((tl, LANES), BF16)
    lf_c = _split3(_log_sigmoid(gates))
    b_f = _dot(lower, lf_c[0]) + _dot(lower, lf_c[1]) + _dot(lower, lf_c[2])
    lf_r = _split3(_log_sigmoid(gates_t[0:2 * ML_HEADS, :]))
    b_row_all = _dot(lf_r[0], upper) + _dot(lf_r[1], upper) + _dot(lf_r[2], upper)
    b_c = pltpu.roll(b_f, LANES - ML_HEADS, 1)
    lane = lax.broadcasted_iota(jnp.int32, (1, LANES), 1)
    m_prev_row = jnp.zeros((1, LANES), F32)
    for h in range(ML_HEADS):
        m_prev_row = jnp.where(lane == h, m_ref[h], m_prev_row)
    run = gates - b_c
    trow = lax.broadcasted_iota(jnp.int32, (tl, LANES), 0)
    shift = 1
    while shift < tl:
        run = jnp.maximum(run, jnp.where(trow >= shift, pltpu.roll(run, shift, 0), -jnp.inf))
        shift *= 2
    m_t_all = b_c + jnp.maximum(m_prev_row, run)
    s_inter_all = jnp.exp(b_c + m_prev_row - m_t_all)
    dmin_all = jnp.exp(-m_t_all)
    a_all = b_c - m_t_all + math.log(scale)
    m_new_row = m_t_all[tl - 1:tl, :]
    b_last_row = b_c[tl - 1:tl, :]
    dec_row = jnp.exp(b_last_row + m_prev_row - m_new_row)
    wsc_row = b_last_row - m_new_row

    hsl = [slice(h * ML_DH, (h + 1) * ML_DH) for h in range(ML_HEADS)]

    def state_dots(h):
        q = q_ref[:, hsl[h]]
        return (_dot(q, kt_ref[h]), _dot(q, c_ref[h].astype(BF16)), _dot(q, ncm_ref[h].astype(BF16))[:, 0:1])

    out = x_ref[...]
    nxt = state_dots(0)
    for h in range(ML_HEADS):
        hs = hsl[h]
        sqk, qc, qn = nxt
        v = v_ref[:, hs]
        r_row = gates_t[h:h + 1, :] - b_row_all[ML_HEADS + h: ML_HEADS + h + 1, :]
        s_inter = s_inter_all[:, h:h + 1]
        dec = dec_row[:, h:h + 1]
        kwt = (kt_ref[h].astype(F32) * (scale * jnp.exp(r_row + wsc_row[:, h:h + 1]))).astype(BF16)
        c_ref[h] = dec * c_ref[h] + _dot(kwt, v)
        ncm_ref[h] = dec * ncm_ref[h] + _dot(kwt, ones)
        m_ref[h] = m_new_row[:, h:h + 1]
        wgt = jnp.exp(jnp.where(causal, a_all[:, h:h + 1] + r_row, -jnp.inf))
        qk = sqk * wgt
        num = s_inter * qc + _dot(qk.astype(BF16), v)
        den = s_inter * qn + jnp.sum(qk, axis=-1, keepdims=True)
        if h + 1 < ML_HEADS:
            nxt = state_dots(h + 1)
        dmax = jnp.maximum(jnp.abs(den), dmin_all[:, h:h + 1])
        xg = sz_ref[:, hs].astype(F32) * num
        mu = jnp.mean(xg, axis=-1, keepdims=True)
        var = jnp.mean(xg * xg, axis=-1, keepdims=True) - mu * mu
        y = (xg - mu) * lax.rsqrt(var + EPS * (dmax * dmax)) * gn_ref[:, hs] + sx_ref[:, hs].astype(F32)
        out = out + _dot(y.astype(BF16), wout_ref[hs, :])
    xo_ref[...] = out


def _ml_b(x, q, kt, v, gates, sz, sx, state, w):
    b, L, d = x.shape
    chunk = min(CHUNK, L)
    tl = min(CELL_TILE, L)
    nl = L // tl
    inner = ML_HEADS * ML_DH
    has_state = state is not None
    kern = functools.partial(_ml_b_kernel, tl=tl, chunk=chunk, nl=nl, has_state=has_state)
    tok = lambda width: pl.BlockSpec((None, tl, width), lambda i, j: (i, j, 0))
    kt_spec = pl.BlockSpec((None, ML_HEADS, ML_DH, tl), lambda i, j: (i, 0, 0, j))
    c_spec = pl.BlockSpec((None, ML_HEADS, ML_DH, ML_DH), lambda i, j: (i, 0, 0, 0))
    n_spec = pl.BlockSpec((None, ML_HEADS, ML_DH), lambda i, j: (i, 0, 0))
    m_spec = pl.BlockSpec((None, ML_HEADS, 1, 1), lambda i, j: (i, 0, 0, 0))
    in_specs = [tok(d), tok(inner), kt_spec, tok(inner), tok(LANES), tok(inner), tok(inner)]
    args = [x, q, kt, v, gates, sz, sx]
    if has_state:
        in_specs += [c_spec, n_spec, m_spec]
        args += [state[0], state[1], state[2].reshape(b, ML_HEADS, 1, 1)]
    in_specs += [_full((1, inner)), _full((inner, d))]
    args += [w['ml_gn'], w['ml_w_out']]
    sds = jax.ShapeDtypeStruct
    out_shape = (sds(x.shape, F32), sds((b, ML_HEADS, ML_DH, ML_DH), F32), sds((b, ML_HEADS, ML_DH), F32),
                 sds((b, ML_HEADS, 1, 1), F32))
    xo, c, n, m = pl.pallas_call(
        kern, grid=(b, nl), in_specs=in_specs, out_specs=[tok(d), c_spec, n_spec, m_spec], out_shape=out_shape,
        scratch_shapes=[pltpu.VMEM((ML_HEADS, ML_DH, LANES), F32)],
        compiler_params=_params("parallel", "arbitrary"), name="mlstm_cell",
    )(*args)
    return xo, c, n, m.reshape(b, ML_HEADS)


def _memkv_kernel(m_ref, g_ref, wk_ref, wv_ref, k_ref, v_ref, kb_ref, vb_ref):
    nb, m_len = k_ref.shape[0], k_ref.shape[1]
    mn = _rms(m_ref[...], g_ref[...]).astype(BF16)
    k = _dot(mn, wk_ref[...])
    v = _dot(mn, wv_ref[...])
    for h in range(X_HEADS):
        kh = k[:, h * X_DH:(h + 1) * X_DH].reshape(nb, m_len, X_DH)
        vh = v[:, h * X_DH:(h + 1) * X_DH].reshape(nb, m_len, X_DH)
        k_ref[:, :, h, :] = kh
        v_ref[:, :, h, :] = vh
        kb_ref[:, h] = kh.astype(BF16)
        vb_ref[:, h] = vh.astype(BF16)


def _memkv(mem, norm_mem, wk, wv):
    b, m_len, d = mem.shape
    depth = wk.shape[0]
    rows = b * m_len
    tm = min(MEM_ROW_TILE, rows)
    nb = tm // m_len
    mem2 = mem.reshape(rows, d)
    wspec = pl.BlockSpec((None, d, d), lambda i, j: (i, 0, 0))
    ospec = pl.BlockSpec((None, nb, m_len, X_HEADS, X_DH), lambda i, j: (i, j, 0, 0, 0))
    oshape = jax.ShapeDtypeStruct((depth, b, m_len, X_HEADS, X_DH), F32)
    bspec = pl.BlockSpec((None, nb, X_HEADS, m_len, X_DH), lambda i, j: (i, j, 0, 0, 0))
    bshape = jax.ShapeDtypeStruct((depth, b, X_HEADS, m_len, X_DH), BF16)
    return pl.pallas_call(
        _memkv_kernel, grid=(depth, rows // tm),
        in_specs=[pl.BlockSpec((tm, d), lambda i, j: (j, 0)), pl.BlockSpec((None, 1, d), lambda i, j: (i, 0, 0)),
                  wspec, wspec],
        out_specs=[ospec, ospec, bspec, bspec],
        out_shape=(oshape, oshape, bshape, bshape),
        compiler_params=_params("parallel", "parallel"), name="memory_kv",
    )(mem2, norm_mem.reshape(depth, 1, d), wk, wv)


def _xattn_kernel(x_ref, g_ref, wq_ref, mk_ref, mv_ref, wo_ref, o_ref):
    x = x_ref[...]
    xn = _rms(x, g_ref[...]).astype(BF16)
    q = _dot(xn, wq_ref[...])
    hsl = [slice(h * X_DH, (h + 1) * X_DH) for h in range(X_HEADS)]

    def scores(h):
        return _dot_nt(q[:, hsl[h]].astype(BF16), mk_ref[h])

    out = x
    nxt = scores(0)
    for h in range(X_HEADS):
        s = nxt * (X_DH ** -0.5)
        if h + 1 < X_HEADS:
            nxt = scores(h + 1)
        e = jnp.exp(s - jnp.max(s, axis=-1, keepdims=True))
        pv = _dot(e.astype(BF16), mv_ref[h])
        oh = pv * (1.0 / jnp.sum(e, axis=-1, keepdims=True))
        out = out + _dot(oh.astype(BF16), wo_ref[hsl[h], :])
    o_ref[...] = out


def _xattn(x, layer, g, wq, mk, mv, wo):
    b, L, d = x.shape
    tl = min(XATTN_TILE, L)
    m_len = mk.shape[3]
    tok = pl.BlockSpec((None, tl, d), lambda i, j: (i, j, 0))
    mem = pl.BlockSpec((None, None, X_HEADS, m_len, X_DH), lambda i, j: (layer, i, 0, 0, 0))
    wsp = pl.BlockSpec((None, d, d), lambda i, j: (layer, 0, 0))
    gsp = pl.BlockSpec((None, 1, d), lambda i, j: (layer, 0, 0))
    return pl.pallas_call(
        _xattn_kernel, grid=(b, L // tl),
        in_specs=[tok, gsp, wsp, mem, mem, wsp],
        out_specs=tok, out_shape=jax.ShapeDtypeStruct(x.shape, F32),
        compiler_params=_params("parallel", "arbitrary"), name="cross_attn",
    )(x, g, wq, mk, mv, wo)


def _mlp_kernel(*refs, final):
    if final:
        x_ref, g_ref, wu_ref, wd_ref, gf_ref, o_ref, xn_ref = refs
    else:
        x_ref, g_ref, wu_ref, wd_ref, o_ref, xn_ref = refs
    j = pl.program_id(1)

    @pl.when(j == 0)
    def _():
        x = x_ref[...]
        xn_ref[...] = _rms(x, g_ref[...]).astype(BF16)
        o_ref[...] = x

    hdn = jnp.maximum(_dot(xn_ref[...], wu_ref[...]), 0.0)
    o_ref[...] += _dot((hdn * hdn).astype(BF16), wd_ref[...])

    if final:
        @pl.when(j == pl.num_programs(1) - 1)
        def _():
            o_ref[...] = _rms(o_ref[...], gf_ref[...])


def _mlp(x, layer, g, wu, wd, gf=None):
    shape = x.shape
    d = shape[-1]
    x2 = x.reshape(-1, d)
    rows = x2.shape[0]
    ff = wu.shape[2]
    tm = min(MLP_ROW_TILE, rows)
    tf = min(MLP_FF_TILE, ff)
    final = gf is not None
    in_specs = [pl.BlockSpec((tm, d), lambda i, j: (i, 0)), pl.BlockSpec((None, 1, d), lambda i, j: (layer, 0, 0)),
                pl.BlockSpec((None, d, tf), lambda i, j: (layer, 0, j)),
                pl.BlockSpec((None, tf, d), lambda i, j: (layer, j, 0))]
    args = [x2, g, wu, wd]
    if final:
        in_specs.append(_full((1, d)))
        args.append(gf)
    out = pl.pallas_call(
        functools.partial(_mlp_kernel, final=final), grid=(rows // tm, ff // tf),
        in_specs=in_specs, out_specs=pl.BlockSpec((tm, d), lambda i, j: (i, 0)),
        out_shape=jax.ShapeDtypeStruct(x2.shape, F32),
        scratch_shapes=[pltpu.VMEM((tm, d), BF16)],
        compiler_params=_params("parallel", "arbitrary"), name="mlp",
    )(*args)
    return out.reshape(shape)


def _trunk(x, pos0, st_ret, st_re, st_im, ml_state, conv_state, mem_k, mem_v, w):
    b, L, d = x.shape
    chunk = min(CHUNK, L)
    nst = S5_GROUPS * S5_STATE // (SUBLANES * LANES)
    st_h = jnp.concatenate([st_re.reshape(b, nst, SUBLANES, LANES), st_im.reshape(b, nst, SUBLANES, LANES)], axis=1)
    x, s_new, h_new = _rs_layer(x, st_ret, st_h, _ret_consts(L, chunk, pos0), w)
    hr = h_new[:, :nst].reshape(b, S5_GROUPS, S5_STATE)
    hi = h_new[:, nst:].reshape(b, S5_GROUPS, S5_STATE)
    x = _xattn(x, 0, w['norm_cross'], w['x_wq'], mem_k, mem_v, w['x_wo'])
    x = _mlp(x, 0, w['norm_mlp'], w['mlp_w_up'], w['mlp_w_down'])
    q, kt, v, gates, sz, sx, new_conv = _ml_a(x, conv_state, w)
    x, cf, nf, mf = _ml_b(x, q, kt, v, gates, sz, sx, ml_state, w)
    x = _xattn(x, 1, w['norm_cross'], w['x_wq'], mem_k, mem_v, w['x_wo'])
    y = _mlp(x, 1, w['norm_mlp'], w['mlp_w_up'], w['mlp_w_down'], w['norm_final'])
    return y, s_new[None], hr[None], hi[None], cf[None], nf[None], mf[None], new_conv[None]


def kernel(x_prompt, x_sample, mem_prompt, state_ret, state_s5_re, state_s5_im, state_mlstm_c, state_mlstm_n, state_mlstm_m, cache_mlstm_conv, cache_mem_k, cache_mem_v, norm_mix, norm_cross, norm_mem, norm_mlp, norm_final, rs_w_in, rs_w_out, ret_gn, s5_a_re, s5_a_im, s5_log_dt, s5_b_re, s5_b_im, s5_c_re, s5_c_im, s5_d, s5_w_glu, s5_b_glu, ml_w_in, ml_conv_w, ml_conv_b, ml_wq, ml_wk, ml_wv, ml_w_gates, ml_b_gates, ml_gn, ml_skip, ml_w_out, x_wq, x_wk, x_wv, x_wo, mlp_w_up, mlp_w_down):
    d = x_prompt.shape[-1]
    bp = x_prompt.shape[0]
    bf = lambda t: t.astype(BF16)
    row = lambda t: t.reshape(1, -1).astype(F32)
    ab_re, ab_im, bb_re, bb_im = _s5_prep(s5_a_re[0], s5_a_im[0], s5_log_dt[0], s5_b_re[0], s5_b_im[0])
    s5_a, s5_b, s5_c = _s5_matrices(ab_re, ab_im, bb_re, bb_im, s5_c_re[0], s5_c_im[0])
    n_gate = ml_w_gates.shape[-1]
    w = {
        'norm_mix0': row(norm_mix[0]), 'norm_mix1': row(norm_mix[1]),
        'norm_cross': norm_cross.reshape(-1, 1, d), 'norm_mlp': norm_mlp.reshape(-1, 1, d),
        'norm_final': row(norm_final),
        'rs_w_in': bf(rs_w_in[0]), 'rs_w_out': bf(rs_w_out[0]), 'ret_gn': row(ret_gn[0]),
        's5_a': s5_a, 's5_b': s5_b, 's5_c': s5_c, 's5_d': row(s5_d[0]), 's5_w_glu': bf(s5_w_glu[0]),
        's5_b_glu': row(s5_b_glu[0]),
        'ml_w_in': bf(ml_w_in[0]), 'ml_conv_w': ml_conv_w[0], 'ml_conv_b': row(ml_conv_b[0]),
        'ml_wq': bf(ml_wq[0]), 'ml_wk': bf(ml_wk[0]), 'ml_wv': bf(ml_wv[0]),
        'ml_w_gates': bf(jnp.pad(ml_w_gates[0], ((0, 0), (0, LANES - n_gate)))),
        'ml_b_gates': jnp.pad(ml_b_gates[0], (0, LANES - n_gate)).reshape(1, LANES),
        'ml_gn': row(ml_gn[0]), 'ml_skip': row(ml_skip[0]), 'ml_w_out': bf(ml_w_out[0]),
        'x_wq': bf(x_wq), 'x_wo': bf(x_wo), 'mlp_w_up': bf(mlp_w_up), 'mlp_w_down': bf(mlp_w_down),
    }
    mk_p, mv_p, mkb_p, mvb_p = _memkv(mem_prompt, norm_mem, bf(x_wk), bf(x_wv))
    zeros = lambda *s: jnp.zeros(s, F32)
    out_p = _trunk(x_prompt, 0, zeros(bp, RET_HEADS, RET_DK, RET_DK), zeros(bp, S5_GROUPS, S5_STATE),
                   zeros(bp, S5_GROUPS, S5_STATE), None, zeros(bp, ML_CONV - 1, ML_HEADS * ML_DH), mkb_p, mvb_p, w)
    out_s = _trunk(x_sample, PAST_LEN, state_ret[0], state_s5_re[0], state_s5_im[0],
                   (state_mlstm_c[0], state_mlstm_n[0], state_mlstm_m[0]), cache_mlstm_conv[0],
                   bf(jnp.swapaxes(cache_mem_k, 2, 3)), bf(jnp.swapaxes(cache_mem_v, 2, 3)), w)
    return (out_p[0], out_s[0]) + tuple(out_p[1:]) + (mk_p, mv_p) + tuple(out_s[1:])
```

```python
import functools
import math

import jax
import jax.numpy as jnp
from jax import lax
from jax.experimental import pallas as pl
from jax.experimental.pallas import tpu as pltpu

F32 = jnp.float32
BF16 = jnp.bfloat16

EPS = 1e-6
ROPE_BASE = 10000.0
PAST_LEN = 4096
RET_HEADS = 4
RET_DK = 128
S5_GROUPS = 32
S5_GROUP = 16
S5_STATE = 64
ML_HEADS = 4
ML_DH = 512
ML_CONV = 4
X_HEADS = 4
X_DH = 256
LANES = 128
SUBLANES = 8
VMEM_LIMIT = 56 * 1024 * 1024
CHUNK = 256
MIXER_TILE = 256
CELL_TILE = 256
PROJ_TILE = 512
XATTN_TILE = 1024
MLP_ROW_TILE = 1024
MLP_FF_TILE = 2048
MEM_ROW_TILE = 1024


def _dot(a, b):
    return jnp.dot(a, b, preferred_element_type=F32)


def _dot_nt(a, b):
    return lax.dot_general(a, b, (((1,), (1,)), ((), ())), preferred_element_type=F32)


def _dot_tn(a, b):
    return lax.dot_general(a, b, (((0,), (0,)), ((), ())), preferred_element_type=F32)


def _rms(x, g):
    y = x * lax.rsqrt(jnp.mean(x * x, axis=-1, keepdims=True) + EPS)
    return y * g


def _layernorm(x):
    mu = jnp.mean(x, axis=-1, keepdims=True)
    xc = x - mu
    var = jnp.mean(xc * xc, axis=-1, keepdims=True)
    return xc * lax.rsqrt(var + EPS)


def _sigmoid(x):
    return 1.0 / (1.0 + jnp.exp(-x))


def _log_sigmoid(x):
    return -(jnp.maximum(-x, 0.0) + jnp.log(1.0 + jnp.exp(-jnp.abs(x))))


def _gelu_tanh(x):
    c = math.sqrt(2.0 / math.pi)
    return 0.5 * x * (1.0 + jnp.tanh(c * (x + 0.044715 * (x * x * x))))


def _params(*sem):
    return pltpu.CompilerParams(dimension_semantics=sem, vmem_limit_bytes=VMEM_LIMIT)


def _full(shape):
    n = len(shape)
    return pl.BlockSpec(shape, lambda *_: (0,) * n, pipeline_mode=pl.Buffered(1))


def _s5_prep_kernel(are_ref, aim_ref, ldt_ref, bre_ref, bim_ref, abre_ref, abim_ref, bbre_ref, bbim_ref):
    a_re = are_ref[...]
    a_im = aim_ref[...]
    dt = jnp.exp(ldt_ref[...])
    mag = jnp.exp(a_re * dt)
    ab_re = mag * jnp.cos(a_im * dt)
    ab_im = mag * jnp.sin(a_im * dt)
    den = a_re * a_re + a_im * a_im
    x_re = ab_re - 1.0
    f_re = (x_re * a_re + ab_im * a_im) / den
    f_im = (ab_im * a_re - x_re * a_im) / den
    b_re = bre_ref[...]
    b_im = bim_ref[...]
    abre_ref[...] = ab_re
    abim_ref[...] = ab_im
    bbre_ref[...] = f_re * b_re - f_im * b_im
    bbim_ref[...] = f_re * b_im + f_im * b_re


def _s5_prep(a_re, a_im, log_dt, b_re, b_im):
    g, p, c = b_re.shape
    out = pl.pallas_call(
        _s5_prep_kernel,
        out_shape=(jax.ShapeDtypeStruct((g, 1, p), F32), jax.ShapeDtypeStruct((g, 1, p), F32),
                   jax.ShapeDtypeStruct((g, c, p), F32), jax.ShapeDtypeStruct((g, c, p), F32)),
        name="s5_prep",
    )(a_re.reshape(g, 1, p), a_im.reshape(g, 1, p), log_dt.reshape(g, 1, 1),
      jnp.swapaxes(b_re, 1, 2), jnp.swapaxes(b_im, 1, 2))
    return out


def _s5_matrices(ab_re, ab_im, bb_re, bb_im, c_re, c_im):
    g, c, p = bb_re.shape
    nsl = g * c // LANES
    gs = LANES // c
    eye = jnp.eye(gs, dtype=F32)

    def bmat(bb):
        t = bb.reshape(nsl, gs, c, p)
        w = jnp.einsum('mgcp,hg->mhcgp', t, eye)
        return w.reshape(nsl, gs * c, gs * p)

    def cmat(cc):
        t = cc.reshape(nsl, gs, c, p)
        w = jnp.einsum('mgcp,hg->mhpgc', t, eye)
        return w.reshape(nsl, gs * p, gs * c)

    b_all = jnp.stack([bmat(bb_re), bmat(bb_im)], axis=1).reshape(nsl * 2, gs * c, gs * p)
    c_all = jnp.concatenate([cmat(c_re), -cmat(c_im)], axis=1)
    nst = g * p // (SUBLANES * LANES)
    a_all = jnp.concatenate([ab_re.reshape(nst, SUBLANES, LANES), ab_im.reshape(nst, SUBLANES, LANES)], axis=0)
    return a_all, b_all.astype(BF16), c_all.astype(BF16)


def _rs_kernel(*refs, tl, chunk):
    s0_ref, h0_ref = refs[9], refs[10]
    s_ref, h_ref, bu_ref = refs[20], refs[21], refs[22]

    @pl.when(pl.program_id(1) == 0)
    def _():
        s_ref[...] = s0_ref[...]
        h_ref[...] = h0_ref[...]

    for sub in range(tl // chunk):
        _rs_chunk(*refs[:22], bu_ref.at[sub], rows=slice(sub * chunk, (sub + 1) * chunk), chunk=chunk)


def _rs_chunk(x_ref, gn_ref, win_ref, cos_ref, sin_ref, decay_ref, wq_ref, wkv_ref, gc_ref,
              s0_ref, h0_ref, rgn_ref, a_ref, bm_ref, cm_ref, dsk_ref, wglu_ref, bglu_ref, wout_ref,
              xo_ref, s_ref, h_ref, bu, *, rows, chunk):
    qk_w = RET_HEADS * RET_DK
    nsl = S5_GROUPS * S5_GROUP // LANES
    nst = S5_GROUPS * S5_STATE // (SUBLANES * LANES)
    rows_per_slab = SUBLANES // (nsl // nst)

    x = x_ref[rows, :]
    xn = _rms(x, gn_ref[...]).astype(BF16)
    proj = _dot(xn, win_ref[...])
    cos = cos_ref[rows, :]
    sin = sin_ref[rows, :]

    u_off = 4 * qk_w
    for m in range(nsl):
        um = proj[:, u_off + m * LANES: u_off + (m + 1) * LANES].astype(BF16)
        for ri in range(2):
            r = _dot(um, bm_ref[2 * m + ri])
            slab = ri * nst + m // (nsl // nst)
            for jl in range(rows_per_slab):
                j = rows_per_slab * (m % (nsl // nst)) + jl
                bu[slab, pl.ds(j, chunk, stride=SUBLANES), :] = r[:, jl * LANES:(jl + 1) * LANES]

    a = [a_ref[i] for i in range(2 * nst)]

    def scan_steps(carry, t0, t1):
        for t in range(t0, t1):
            trow = pl.ds(t * SUBLANES, SUBLANES)
            new = []
            for s in range(nst):
                hr, hi = carry[s], carry[nst + s]
                ar, ai = a[s], a[nst + s]
                nr = ar * hr - ai * hi + bu[s, trow, :]
                ni = ar * hi + ai * hr + bu[nst + s, trow, :]
                bu[s, trow, :] = nr
                bu[nst + s, trow, :] = ni
                new.append((nr, ni))
            carry = tuple(n[0] for n in new) + tuple(n[1] for n in new)
        return carry

    def rope_head(h):
        q = proj[:, h * RET_DK:(h + 1) * RET_DK]
        k = proj[:, qk_w + h * RET_DK: qk_w + (h + 1) * RET_DK]
        qr = q * cos + pltpu.roll(q, RET_DK // 2, 1) * sin
        kr = (k * cos + pltpu.roll(k, RET_DK // 2, 1) * sin) * (RET_DK ** -0.5)
        qb = qr.astype(BF16)
        return qb, kr, _dot_nt(qb, kr.astype(BF16)), _dot(qb, s_ref[h].astype(BF16))

    carry = tuple(h_ref[i] for i in range(2 * nst))
    per_head = chunk // RET_HEADS
    pieces = []
    nxt = rope_head(0)
    for h in range(RET_HEADS):
        hs = slice(h * RET_DK, (h + 1) * RET_DK)
        qb, kr, sqk, cross = nxt
        vb = proj[:, 2 * qk_w + h * RET_DK: 2 * qk_w + (h + 1) * RET_DK].astype(BF16)
        g = proj[:, 3 * qk_w + h * RET_DK: 3 * qk_w + (h + 1) * RET_DK]
        o = _dot((sqk * decay_ref[h]).astype(BF16), vb) + cross * wq_ref[:, hs]
        if h + 1 < RET_HEADS:
            nxt = rope_head(h + 1)
        carry = scan_steps(carry, h * per_head, (h + 1) * per_head)
        s_ref[h] = gc_ref[h] * s_ref[h] + _dot_tn((kr * wkv_ref[:, hs]).astype(BF16), vb)
        mu = jnp.mean(o, axis=-1, keepdims=True)
        var = jnp.mean(o * o, axis=-1, keepdims=True) - mu * mu
        y = (o - mu) * lax.rsqrt(var + EPS) * rgn_ref[:, hs]
        pieces.append((g * _sigmoid(g) * y).astype(BF16))
    carry = scan_steps(carry, RET_HEADS * per_head, chunk)
    for i in range(2 * nst):
        h_ref[i] = carry[i]
    out = x + _dot(jnp.concatenate(pieces, axis=1), wout_ref[0:qk_w, :])

    ys = []
    for m in range(nsl):
        parts = []
        for ri in range(2):
            slab = ri * nst + m // (nsl // nst)
            for jl in range(rows_per_slab):
                j = rows_per_slab * (m % (nsl // nst)) + jl
                parts.append(bu[slab, pl.ds(j, chunk, stride=SUBLANES), :])
        hcat = jnp.concatenate(parts, axis=1).astype(BF16)
        um = proj[:, u_off + m * LANES: u_off + (m + 1) * LANES]
        ym = _dot(hcat, cm_ref[m]) + dsk_ref[:, m * LANES:(m + 1) * LANES] * um
        ys.append(_gelu_tanh(ym))
    yg = jnp.concatenate(ys, axis=1)
    gate = _sigmoid(_dot(yg.astype(BF16), wglu_ref[...]) + bglu_ref[...])
    xo_ref[rows, :] = out + _dot((yg * gate).astype(BF16), wout_ref[qk_w:, :])


def _rs_layer(x, st_ret, st_h, consts, w):
    b, L, d = x.shape
    chunk = min(CHUNK, L)
    tl = min(MIXER_TILE, L)
    nl = L // tl
    cos, sin, decay, wq, wkv, gc = consts
    rs_in = w['rs_w_in'].shape[1]
    nst2 = st_h.shape[1]
    kern = functools.partial(_rs_kernel, tl=tl, chunk=chunk)
    tok = pl.BlockSpec((None, tl, d), lambda i, j: (i, j, 0))
    in_specs = [
        tok, _full((1, d)), _full((d, rs_in)),
        pl.BlockSpec((tl, LANES), lambda i, j: (j, 0)), pl.BlockSpec((tl, LANES), lambda i, j: (j, 0)),
        _full(decay.shape), _full(wq.shape), _full(wkv.shape), _full(gc.shape),
        pl.BlockSpec((None,) + st_ret.shape[1:], lambda i, j: (i, 0, 0, 0)),
        pl.BlockSpec((None,) + st_h.shape[1:], lambda i, j: (i, 0, 0, 0)),
        _full((1, RET_HEADS * RET_DK)), _full(w['s5_a'].shape), _full(w['s5_b'].shape), _full(w['s5_c'].shape),
        _full((1, w['s5_d'].shape[1])), _full(w['s5_w_glu'].shape), _full((1, w['s5_b_glu'].shape[1])),
        _full(w['rs_w_out'].shape),
    ]
    out_specs = [
        tok,
        pl.BlockSpec((None,) + st_ret.shape[1:], lambda i, j: (i, 0, 0, 0)),
        pl.BlockSpec((None,) + st_h.shape[1:], lambda i, j: (i, 0, 0, 0)),
    ]
    return pl.pallas_call(
        kern, grid=(b, nl), in_specs=in_specs, out_specs=out_specs,
        out_shape=(jax.ShapeDtypeStruct(x.shape, F32), jax.ShapeDtypeStruct(st_ret.shape, F32),
                   jax.ShapeDtypeStruct(st_h.shape, F32)),
        scratch_shapes=[pltpu.VMEM((tl // chunk, nst2, chunk * SUBLANES, LANES), F32)],
        compiler_params=_params("parallel", "arbitrary"), name="rs_mixer",
    )(x, w['norm_mix0'], w['rs_w_in'], cos, sin, decay, wq, wkv, gc, st_ret, st_h, w['ret_gn'],
      w['s5_a'], w['s5_b'], w['s5_c'], w['s5_d'], w['s5_w_glu'], w['s5_b_glu'], w['rs_w_out'])


def _ret_consts(L, tl, pos0):
    half = RET_DK // 2
    pos = pos0 + jnp.arange(L, dtype=jnp.int32)
    inv = ROPE_BASE ** (-jnp.arange(half, dtype=F32) / half)
    ang = pos.astype(F32)[:, None] * inv[None, :]
    cos = jnp.cos(ang)
    sin = jnp.sin(ang)
    cos2 = jnp.concatenate([cos, cos], axis=1)
    sin2 = jnp.concatenate([-sin, sin], axis=1)
    log_g = jnp.log1p(-jnp.exp2(-5.0 - jnp.arange(RET_HEADS, dtype=F32)))
    idx = jnp.arange(tl, dtype=F32)
    diff = idx[:, None] - idx[None, :]
    decay = jnp.where(diff[None] >= 0, jnp.exp(jnp.maximum(diff, 0.0)[None] * log_g[:, None, None]), 0.0)
    w_q = jnp.exp((idx + 1.0)[:, None] * log_g[None, :])
    w_kv = jnp.exp((tl - 1.0 - idx)[:, None] * log_g[None, :])
    g_c = jnp.exp(tl * log_g)
    rep = lambda t: jnp.repeat(t, RET_DK, axis=1)
    gc = jnp.broadcast_to(g_c[:, None, None], (RET_HEADS, 1, RET_DK))
    return cos2, sin2, decay, rep(w_q), rep(w_kv), gc


def _gate_fold_kernel(wq_ref, wk_ref, wv_ref, wg_ref, o_ref):
    inner = ML_HEADS * ML_DH
    hp = lax.Precision.HIGHEST
    for h in range(ML_HEADS):
        rows = slice(h * ML_DH, (h + 1) * ML_DH)
        gq = wg_ref[h * ML_DH:(h + 1) * ML_DH, :]
        gk = wg_ref[inner + h * ML_DH: inner + (h + 1) * ML_DH, :]
        gv = wg_ref[2 * inner + h * ML_DH: 2 * inner + (h + 1) * ML_DH, :]
        o_ref[0, rows, :] = (jnp.dot(wq_ref[h], gq, precision=hp, preferred_element_type=F32)
                             + jnp.dot(wk_ref[h], gk, precision=hp, preferred_element_type=F32))
        o_ref[1, rows, :] = jnp.dot(wv_ref[h], gv, precision=hp, preferred_element_type=F32)


def _gate_fold(wq, wk, wv, wg_padded):
    inner = ML_HEADS * ML_DH
    return pl.pallas_call(
        _gate_fold_kernel, out_shape=jax.ShapeDtypeStruct((2, inner, LANES), F32), name="gate_fold",
    )(wq, wk, wv, wg_padded)


def _ml_a_kernel(x_ref, gn_ref, win_ref, cs_ref, cw_ref, cb_ref, wq_ref, wkt_ref, wv_ref, wg_ref, bg_ref, skip_ref,
                 q_ref, kt_ref, v_ref, gates_ref, sz_ref, sx_ref, nc_ref, xs_ref, *, tl, nl):
    l = pl.program_id(1)
    inner = ML_HEADS * ML_DH
    pad = SUBLANES
    hist = ML_CONV - 1

    @pl.when(l == 0)
    def _():
        xs_ref[0:pad, :] = jnp.zeros((pad, inner), F32)
        xs_ref[pl.ds(pad - hist, hist), :] = cs_ref[...]

    xn = _rms(x_ref[...], gn_ref[...]).astype(BF16)
    xm = _dot(xn, win_ref[:, 0:inner])
    sz_ref[...] = _sigmoid(_dot(xn, win_ref[:, inner:2 * inner])).astype(sz_ref.dtype)
    xs_ref[pl.ds(pad, tl), :] = xm
    acc = cw_ref[hist:hist + 1, :] * xm
    for j in range(hist):
        acc = acc + cw_ref[j:j + 1, :] * xs_ref[pl.ds(pad - hist + j, tl), :]
    xc = acc + cb_ref[...]
    xc = xc * _sigmoid(xc)
    sx_ref[...] = (skip_ref[...] * xc).astype(sx_ref.dtype)
    xcb = xc.astype(BF16)
    xmb = xm.astype(BF16)
    gates_ref[...] = _dot(xcb, wg_ref[0]) + _dot(xmb, wg_ref[1]) + bg_ref[...]
    for h in range(ML_HEADS):
        hs = slice(h * ML_DH, (h + 1) * ML_DH)
        q_ref[:, hs] = _dot(xcb[:, hs], wq_ref[h]).astype(BF16)
        kt_ref[h] = _dot_nt(wkt_ref[h], xcb[:, hs]).astype(BF16)
        v_ref[:, hs] = _dot(xmb[:, hs], wv_ref[h]).astype(BF16)

    @pl.when(l == nl - 1)
    def _():
        nc_ref[...] = xs_ref[pl.ds(pad + tl - hist, hist), :]

    xs_ref[0:pad, :] = xs_ref[pl.ds(tl, pad), :]


def _ml_a(x, conv_state, w):
    b, L, d = x.shape
    tl = min(PROJ_TILE, L)
    nl = L // tl
    inner = ML_HEADS * ML_DH
    kern = functools.partial(_ml_a_kernel, tl=tl, nl=nl)
    tok = lambda width: pl.BlockSpec((None, tl, width), lambda i, j: (i, j, 0))
    kt_spec = pl.BlockSpec((None, ML_HEADS, ML_DH, tl), lambda i, j: (i, 0, 0, j))
    cst = pl.BlockSpec((None, ML_CONV - 1, inner), lambda i, j: (i, 0, 0))
    in_specs = [tok(d), _full((1, d)), _full((d, 2 * inner)), cst, _full((ML_CONV, inner)), _full((1, inner)),
                _full(w['ml_wq'].shape), _full(w['ml_wkt'].shape), _full(w['ml_wv'].shape),
                _full(w['ml_w_gates'].shape), _full((1, LANES)), _full((1, inner))]
    out_specs = [tok(inner), kt_spec, tok(inner), tok(LANES), tok(inner), tok(inner), cst]
    sds = jax.ShapeDtypeStruct
    out_shape = (sds((b, L, inner), BF16), sds((b, ML_HEADS, ML_DH, L), BF16), sds((b, L, inner), BF16),
                 sds((b, L, LANES), F32), sds((b, L, inner), BF16), sds((b, L, inner), BF16),
                 sds((b, ML_CONV - 1, inner), F32))
    return pl.pallas_call(
        kern, grid=(b, nl), in_specs=in_specs, out_specs=out_specs, out_shape=out_shape,
        scratch_shapes=[pltpu.VMEM((tl + SUBLANES, inner), F32)],
        compiler_params=_params("parallel", "arbitrary"), name="mlstm_proj",
    )(x, w['norm_mix1'], w['ml_w_in'], conv_state, w['ml_conv_w'], w['ml_conv_b'],
      w['ml_wq'], w['ml_wkt'], w['ml_wv'], w['ml_w_gates'], w['ml_b_gates'], w['ml_skip'])


def _split3(x):
    hi = x.astype(BF16)
    r1 = x - hi.astype(F32)
    mid = r1.astype(BF16)
    lo = (r1 - mid.astype(F32)).astype(BF16)
    return hi, mid, lo


def _ml_b_kernel(*refs, tl, chunk, nl, has_state):
    if has_state:
        (x_ref, q_ref, kt_ref, v_ref, g_ref, sz_ref, sx_ref, c0_ref, n0_ref, m0_ref, gn_ref, wout_ref,
         xo_ref, c_ref, n_ref, m_ref, ncm_ref) = refs
    else:
        (x_ref, q_ref, kt_ref, v_ref, g_ref, sz_ref, sx_ref, gn_ref, wout_ref,
         xo_ref, c_ref, n_ref, m_ref, ncm_ref) = refs
    l = pl.program_id(1)

    @pl.when(l == 0)
    def _():
        if has_state:
            c_ref[...] = c0_ref[...]
            m_ref[...] = m0_ref[...]
            for h in range(ML_HEADS):
                ncm_ref[h] = jnp.broadcast_to(n0_ref[h:h + 1, :], (LANES, ML_DH)).T
        else:
            c_ref[...] = jnp.zeros(c_ref.shape, F32)
            m_ref[...] = jnp.zeros(m_ref.shape, F32)
            ncm_ref[...] = jnp.zeros(ncm_ref.shape, F32)

    for sub in range(tl // chunk):
        rows = slice(sub * chunk, (sub + 1) * chunk)
        _ml_b_chunk(x_ref.at[rows], q_ref.at[rows], kt_ref.at[:, :, rows], v_ref.at[rows], g_ref.at[rows],
                    sz_ref.at[rows], sx_ref.at[rows], gn_ref, wout_ref, xo_ref.at[rows], c_ref, m_ref, ncm_ref,
                    tl=chunk)

    @pl.when(l == nl - 1)
    def _():
        for h in range(ML_HEADS):
            n_ref[h:h + 1, :] = ncm_ref[h].T[0:1, :]


def _ml_b_chunk(x_ref, q_ref, kt_ref, v_ref, g_ref, sz_ref, sx_ref, gn_ref, wout_ref, xo_ref, c_ref, m_ref, ncm_ref,
                *, tl):
    scale = ML_DH ** -0.5
    gates = g_ref[...]
    gates_t = gates.T
    row = lax.broadcasted_iota(jnp.int32, (tl, tl), 0)
    col = lax.broadcasted_iota(jnp.int32, (tl, tl), 1)
    causal = row >= col
    lower = jnp.where(causal, 1.0, 0.0).astype(BF16)
    upper = jnp.where(row <= col, 1.0, 0.0).astype(BF16)
    ones = jnp.ones((tl, LANES), BF16)
    lf_c = _split3(_log_sigmoid(gates))
    b_f = _dot(lower, lf_c[0]) + _dot(lower, lf_c[1]) + _dot(lower, lf_c[2])
    lf_r = _split3(_log_sigmoid(gates_t[0:2 * ML_HEADS, :]))
    b_row_all = _dot(lf_r[0], upper) + _dot(lf_r[1], upper) + _dot(lf_r[2], upper)
    b_c = pltpu.roll(b_f, LANES - ML_HEADS, 1)
    lane = lax.broadcasted_iota(jnp.int32, (1, LANES), 1)
    m_prev_row = jnp.zeros((1, LANES), F32)
    for h in range(ML_HEADS):
        m_prev_row = jnp.where(lane == h, m_ref[h], m_prev_row)
    run = gates - b_c
    trow = lax.broadcasted_iota(jnp.int32, (tl, LANES), 0)
    shift = 1
    while shift < tl:
        run = jnp.maximum(run, jnp.where(trow >= shift, pltpu.roll(run, shift, 0), -jnp.inf))
        shift *= 2
    m_t_all = b_c + jnp.maximum(m_prev_row, run)
    s_inter_all = jnp.exp(b_c + m_prev_row - m_t_all)
    dmin_all = jnp.exp(-m_t_all)
    a_all = b_c - m_t_all + math.log(scale)
    m_new_row = m_t_all[tl - 1:tl, :]
    b_last_row = b_c[tl - 1:tl, :]
    dec_row = jnp.exp(b_last_row + m_prev_row - m_new_row)
    wsc_row = b_last_row - m_new_row

    hsl = [slice(h * ML_DH, (h + 1) * ML_DH) for h in range(ML_HEADS)]

    def state_dots(h):
        q = q_ref[:, hsl[h]]
        return (_dot(q, kt_ref[h]), _dot(q, c_ref[h].astype(BF16)), _dot(q, ncm_ref[h].astype(BF16))[:, 0:1])

    out = x_ref[...]
    nxt = state_dots(0)
    for h in range(ML_HEADS):
        hs = hsl[h]
        sqk, qc, qn = nxt
        v = v_ref[:, hs]
        r_row = gates_t[h:h + 1, :] - b_row_all[ML_HEADS + h: ML_HEADS + h + 1, :]
        s_inter = s_inter_all[:, h:h + 1]
        dec = dec_row[:, h:h + 1]
        kwt = (kt_ref[h].astype(F32) * (scale * jnp.exp(r_row + wsc_row[:, h:h + 1]))).astype(BF16)
        c_ref[h] = dec * c_ref[h] + _dot(kwt, v)
        ncm_ref[h] = dec * ncm_ref[h] + _dot(kwt, ones)
        m_ref[h] = m_new_row[:, h:h + 1]
        wgt = jnp.exp(jnp.where(causal, a_all[:, h:h + 1] + r_row, -jnp.inf))
        qk = sqk * wgt
        num = s_inter * qc + _dot(qk.astype(BF16), v)
        den = s_inter * qn + jnp.sum(qk, axis=-1, keepdims=True)
        if h + 1 < ML_HEADS:
            nxt = state_dots(h + 1)
        dmax = jnp.maximum(jnp.abs(den), dmin_all[:, h:h + 1])
        xg = sz_ref[:, hs].astype(F32) * num
        mu = jnp.mean(xg, axis=-1, keepdims=True)
        var = jnp.mean(xg * xg, axis=-1, keepdims=True) - mu * mu
        y = (xg - mu) * lax.rsqrt(var + EPS * (dmax * dmax)) * gn_ref[:, hs] + sx_ref[:, hs].astype(F32)
        out = out + _dot(y.astype(BF16), wout_ref[hs, :])
    xo_ref[...] = out


def _ml_b(x, q, kt, v, gates, sz, sx, state, w):
    b, L, d = x.shape
    chunk = min(CHUNK, L)
    tl = min(CELL_TILE, L)
    nl = L // tl
    inner = ML_HEADS * ML_DH
    has_state = state is not None
    kern = functools.partial(_ml_b_kernel, tl=tl, chunk=chunk, nl=nl, has_state=has_state)
    tok = lambda width: pl.BlockSpec((None, tl, width), lambda i, j: (i, j, 0))
    kt_spec = pl.BlockSpec((None, ML_HEADS, ML_DH, tl), lambda i, j: (i, 0, 0, j))
    c_spec = pl.BlockSpec((None, ML_HEADS, ML_DH, ML_DH), lambda i, j: (i, 0, 0, 0))
    n_spec = pl.BlockSpec((None, ML_HEADS, ML_DH), lambda i, j: (i, 0, 0))
    m_spec = pl.BlockSpec((None, ML_HEADS, 1, 1), lambda i, j: (i, 0, 0, 0))
    in_specs = [tok(d), tok(inner), kt_spec, tok(inner), tok(LANES), tok(inner), tok(inner)]
    args = [x, q, kt, v, gates, sz, sx]
    if has_state:
        in_specs += [c_spec, n_spec, m_spec]
        args += [state[0], state[1], state[2].reshape(b, ML_HEADS, 1, 1)]
    in_specs += [_full((1, inner)), _full((inner, d))]
    args += [w['ml_gn'], w['ml_w_out']]
    sds = jax.ShapeDtypeStruct
    out_shape = (sds(x.shape, F32), sds((b, ML_HEADS, ML_DH, ML_DH), F32), sds((b, ML_HEADS, ML_DH), F32),
                 sds((b, ML_HEADS, 1, 1), F32))
    xo, c, n, m = pl.pallas_call(
        kern, grid=(b, nl), in_specs=in_specs, out_specs=[tok(d), c_spec, n_spec, m_spec], out_shape=out_shape,
        scratch_shapes=[pltpu.VMEM((ML_HEADS, ML_DH, LANES), F32)],
        compiler_params=_params("parallel", "arbitrary"), name="mlstm_cell",
    )(*args)
    return xo, c, n, m.reshape(b, ML_HEADS)


def _memkv_kernel(m_ref, g_ref, wk_ref, wv_ref, k_ref, v_ref, kb_ref, vb_ref):
    nb, m_len = k_ref.shape[0], k_ref.shape[1]
    mn = _rms(m_ref[...], g_ref[...]).astype(BF16)
    k = _dot(mn, wk_ref[...])
    v = _dot(mn, wv_ref[...])
    for h in range(X_HEADS):
        kh = k[:, h * X_DH:(h + 1) * X_DH].reshape(nb, m_len, X_DH)
        vh = v[:, h * X_DH:(h + 1) * X_DH].reshape(nb, m_len, X_DH)
        k_ref[:, :, h, :] = kh
        v_ref[:, :, h, :] = vh
        kb_ref[:, h] = kh.astype(BF16)
        vb_ref[:, h] = vh.astype(BF16)


def _memkv(mem, norm_mem, wk, wv):
    b, m_len, d = mem.shape
    depth = wk.shape[0]
    rows = b * m_len
    tm = min(MEM_ROW_TILE, rows)
    nb = tm // m_len
    mem2 = mem.reshape(rows, d)
    wspec = pl.BlockSpec((None, d, d), lambda i, j: (i, 0, 0))
    ospec = pl.BlockSpec((None, nb, m_len, X_HEADS, X_DH), lambda i, j: (i, j, 0, 0, 0))
    oshape = jax.ShapeDtypeStruct((depth, b, m_len, X_HEADS, X_DH), F32)
    bspec = pl.BlockSpec((None, nb, X_HEADS, m_len, X_DH), lambda i, j: (i, j, 0, 0, 0))
    bshape = jax.ShapeDtypeStruct((depth, b, X_HEADS, m_len, X_DH), BF16)
    return pl.pallas_call(
        _memkv_kernel, grid=(depth, rows // tm),
        in_specs=[pl.BlockSpec((tm, d), lambda i, j: (j, 0)), pl.BlockSpec((None, 1, d), lambda i, j: (i, 0, 0)),
                  wspec, wspec],
        out_specs=[ospec, ospec, bspec, bspec],
        out_shape=(oshape, oshape, bshape, bshape),
        compiler_params=_params("parallel", "parallel"), name="memory_kv",
    )(mem2, norm_mem.reshape(depth, 1, d), wk, wv)


def _xattn_kernel(x_ref, g_ref, wq_ref, mk_ref, mv_ref, wo_ref, o_ref):
    x = x_ref[...]
    xn = _rms(x, g_ref[...]).astype(BF16)
    q = _dot(xn, wq_ref[...])
    hsl = [slice(h * X_DH, (h + 1) * X_DH) for h in range(X_HEADS)]

    def scores(h):
        return _dot_nt(q[:, hsl[h]].astype(BF16), mk_ref[h])

    out = x
    nxt = scores(0)
    for h in range(X_HEADS):
        s = nxt * (X_DH ** -0.5)
        if h + 1 < X_HEADS:
            nxt = scores(h + 1)
        e = jnp.exp(s - jnp.max(s, axis=-1, keepdims=True))
        pv = _dot(e.astype(BF16), mv_ref[h])
        oh = pv * (1.0 / jnp.sum(e, axis=-1, keepdims=True))
        out = out + _dot(oh.astype(BF16), wo_ref[hsl[h], :])
    o_ref[...] = out


def _xattn(x, layer, g, wq, mk, mv, wo):
    b, L, d = x.shape
    tl = min(XATTN_TILE, L)
    m_len = mk.shape[3]
    tok = pl.BlockSpec((None, tl, d), lambda i, j: (i, j, 0))
    mem = pl.BlockSpec((None, None, X_HEADS, m_len, X_DH), lambda i, j: (layer, i, 0, 0, 0))
    wsp = pl.BlockSpec((None, d, d), lambda i, j: (layer, 0, 0))
    gsp = pl.BlockSpec((None, 1, d), lambda i, j: (layer, 0, 0))
    return pl.pallas_call(
        _xattn_kernel, grid=(b, L // tl),
        in_specs=[tok, gsp, wsp, mem, mem, wsp],
        out_specs=tok, out_shape=jax.ShapeDtypeStruct(x.shape, F32),
        compiler_params=_params("parallel", "arbitrary"), name="cross_attn",
    )(x, g, wq, mk, mv, wo)


def _mlp_kernel(*refs, final):
    if final:
        x_ref, g_ref, wu_ref, wd_ref, gf_ref, o_ref, xn_ref = refs
    else:
        x_ref, g_ref, wu_ref, wd_ref, o_ref, xn_ref = refs
    j = pl.program_id(1)

    @pl.when(j == 0)
    def _():
        x = x_ref[...]
        xn_ref[...] = _rms(x, g_ref[...]).astype(BF16)
        o_ref[...] = x

    hdn = jnp.maximum(_dot(xn_ref[...], wu_ref[...]), 0.0)
    o_ref[...] += _dot((hdn * hdn).astype(BF16), wd_ref[...])

    if final:
        @pl.when(j == pl.num_programs(1) - 1)
        def _():
            o_ref[...] = _rms(o_ref[...], gf_ref[...])


def _mlp(x, layer, g, wu, wd, gf=None):
    shape = x.shape
    d = shape[-1]
    x2 = x.reshape(-1, d)
    rows = x2.shape[0]
    ff = wu.shape[2]
    tm = min(MLP_ROW_TILE, rows)
    tf = min(MLP_FF_TILE, ff)
    final = gf is not None
    in_specs = [pl.BlockSpec((tm, d), lambda i, j: (i, 0)), pl.BlockSpec((None, 1, d), lambda i, j: (layer, 0, 0)),
                pl.BlockSpec((None, d, tf), lambda i, j: (layer, 0, j)),
                pl.BlockSpec((None, tf, d), lambda i, j: (layer, j, 0))]
    args = [x2, g, wu, wd]
    if final:
        in_specs.append(_full((1, d)))
        args.append(gf)
    out = pl.pallas_call(
        functools.partial(_mlp_kernel, final=final), grid=(rows // tm, ff // tf),
        in_specs=in_specs, out_specs=pl.BlockSpec((tm, d), lambda i, j: (i, 0)),
        out_shape=jax.ShapeDtypeStruct(x2.shape, F32),
        scratch_shapes=[pltpu.VMEM((tm, d), BF16)],
        compiler_params=_params("parallel", "arbitrary"), name="mlp",
    )(*args)
    return out.reshape(shape)


def _trunk(x, pos0, st_ret, st_re, st_im, ml_state, conv_state, mem_k, mem_v, w):
    b, L, d = x.shape
    chunk = min(CHUNK, L)
    nst = S5_GROUPS * S5_STATE // (SUBLANES * LANES)
    st_h = jnp.concatenate([st_re.reshape(b, nst, SUBLANES, LANES), st_im.reshape(b, nst, SUBLANES, LANES)], axis=1)
    x, s_new, h_new = _rs_layer(x, st_ret, st_h, _ret_consts(L, chunk, pos0), w)
    hr = h_new[:, :nst].reshape(b, S5_GROUPS, S5_STATE)
    hi = h_new[:, nst:].reshape(b, S5_GROUPS, S5_STATE)
    x = _xattn(x, 0, w['norm_cross'], w['x_wq'], mem_k, mem_v, w['x_wo'])
    x = _mlp(x, 0, w['norm_mlp'], w['mlp_w_up'], w['mlp_w_down'])
    q, kt, v, gates, sz, sx, new_conv = _ml_a(x, conv_state, w)
    x, cf, nf, mf = _ml_b(x, q, kt, v, gates, sz, sx, ml_state, w)
    x = _xattn(x, 1, w['norm_cross'], w['x_wq'], mem_k, mem_v, w['x_wo'])
    y = _mlp(x, 1, w['norm_mlp'], w['mlp_w_up'], w['mlp_w_down'], w['norm_final'])
    return y, s_new[None], hr[None], hi[None], cf[None], nf[None], mf[None], new_conv[None]


def kernel(x_prompt, x_sample, mem_prompt, state_ret, state_s5_re, state_s5_im, state_mlstm_c, state_mlstm_n, state_mlstm_m, cache_mlstm_conv, cache_mem_k, cache_mem_v, norm_mix, norm_cross, norm_mem, norm_mlp, norm_final, rs_w_in, rs_w_out, ret_gn, s5_a_re, s5_a_im, s5_log_dt, s5_b_re, s5_b_im, s5_c_re, s5_c_im, s5_d, s5_w_glu, s5_b_glu, ml_w_in, ml_conv_w, ml_conv_b, ml_wq, ml_wk, ml_wv, ml_w_gates, ml_b_gates, ml_gn, ml_skip, ml_w_out, x_wq, x_wk, x_wv, x_wo, mlp_w_up, mlp_w_down):
    d = x_prompt.shape[-1]
    bp = x_prompt.shape[0]
    bf = lambda t: t.astype(BF16)
    row = lambda t: t.reshape(1, -1).astype(F32)
    ab_re, ab_im, bb_re, bb_im = _s5_prep(s5_a_re[0], s5_a_im[0], s5_log_dt[0], s5_b_re[0], s5_b_im[0])
    s5_a, s5_b, s5_c = _s5_matrices(ab_re, ab_im, bb_re, bb_im, s5_c_re[0], s5_c_im[0])
    n_gate = ml_w_gates.shape[-1]
    w = {
        'norm_mix0': row(norm_mix[0]), 'norm_mix1': row(norm_mix[1]),
        'norm_cross': norm_cross.reshape(-1, 1, d), 'norm_mlp': norm_mlp.reshape(-1, 1, d),
        'norm_final': row(norm_final),
        'rs_w_in': bf(rs_w_in[0]), 'rs_w_out': bf(rs_w_out[0]), 'ret_gn': row(ret_gn[0]),
        's5_a': s5_a, 's5_b': s5_b, 's5_c': s5_c, 's5_d': row(s5_d[0]), 's5_w_glu': bf(s5_w_glu[0]),
        's5_b_glu': row(s5_b_glu[0]),
        'ml_w_in': bf(ml_w_in[0]), 'ml_conv_w': ml_conv_w[0], 'ml_conv_b': row(ml_conv_b[0]),
        'ml_wq': bf(ml_wq[0]), 'ml_wkt': bf(jnp.swapaxes(ml_wk[0], 1, 2)), 'ml_wv': bf(ml_wv[0]),
        'ml_w_gates': bf(_gate_fold(ml_wq[0], ml_wk[0], ml_wv[0],
                                    jnp.pad(ml_w_gates[0], ((0, 0), (0, LANES - n_gate))))),
        'ml_b_gates': jnp.pad(ml_b_gates[0], (0, LANES - n_gate)).reshape(1, LANES),
        'ml_gn': row(ml_gn[0]), 'ml_skip': row(ml_skip[0]), 'ml_w_out': bf(ml_w_out[0]),
        'x_wq': bf(x_wq), 'x_wo': bf(x_wo), 'mlp_w_up': bf(mlp_w_up), 'mlp_w_down': bf(mlp_w_down),
    }
    mk_p, mv_p, mkb_p, mvb_p = _memkv(mem_prompt, norm_mem, bf(x_wk), bf(x_wv))
    zeros = lambda *s: jnp.zeros(s, F32)
    out_p = _trunk(x_prompt, 0, zeros(bp, RET_HEADS, RET_DK, RET_DK), zeros(bp, S5_GROUPS, S5_STATE),
                   zeros(bp, S5_GROUPS, S5_STATE), None, zeros(bp, ML_CONV - 1, ML_HEADS * ML_DH), mkb_p, mvb_p, w)
    out_s = _trunk(x_sample, PAST_LEN, state_ret[0], state_s5_re[0], state_s5_im[0],
                   (state_mlstm_c[0], state_mlstm_n[0], state_mlstm_m[0]), cache_mlstm_conv[0],
                   bf(jnp.swapaxes(cache_mem_k, 2, 3)), bf(jnp.swapaxes(cache_mem_v, 2, 3)), w)
    return (out_p[0], out_s[0]) + tuple(out_p[1:]) + (mk_p, mv_p) + tuple(out_s[1:])
```

```python
import functools
import math

import jax
import jax.numpy as jnp
from jax import lax
from jax.experimental import pallas as pl
from jax.experimental.pallas import tpu as pltpu

F32 = jnp.float32
BF16 = jnp.bfloat16

EPS = 1e-6
ROPE_BASE = 10000.0
PAST_LEN = 4096
RET_HEADS = 4
RET_DK = 128
S5_GROUPS = 32
S5_GROUP = 16
S5_STATE = 64
ML_HEADS = 4
ML_DH = 512
ML_CONV = 4
X_HEADS = 4
X_DH = 256
LANES = 128
SUBLANES = 8
VMEM_LIMIT = 56 * 1024 * 1024
CHUNK = 256
MIXER_TILE = 256
CELL_TILE = 256
PROJ_TILE = 512
XATTN_TILE = 1024
MLP_ROW_TILE = 1024
MLP_FF_TILE = 2048
MEM_ROW_TILE = 1024


def _dot(a, b):
    return jnp.dot(a, b, preferred_element_type=F32)


def _dot_nt(a, b):
    return lax.dot_general(a, b, (((1,), (1,)), ((), ())), preferred_element_type=F32)


def _dot_tn(a, b):
    return lax.dot_general(a, b, (((0,), (0,)), ((), ())), preferred_element_type=F32)


def _rms(x, g):
    y = x * lax.rsqrt(jnp.mean(x * x, axis=-1, keepdims=True) + EPS)
    return y * g


def _layernorm(x):
    mu = jnp.mean(x, axis=-1, keepdims=True)
    xc = x - mu
    var = jnp.mean(xc * xc, axis=-1, keepdims=True)
    return xc * lax.rsqrt(var + EPS)


def _sigmoid(x):
    return 1.0 / (1.0 + jnp.exp(-x))


def _log_sigmoid(x):
    return -(jnp.maximum(-x, 0.0) + jnp.log(1.0 + jnp.exp(-jnp.abs(x))))


def _gelu_tanh(x):
    c = math.sqrt(2.0 / math.pi)
    return 0.5 * x * (1.0 + jnp.tanh(c * (x + 0.044715 * (x * x * x))))


def _params(*sem):
    return pltpu.CompilerParams(dimension_semantics=sem, vmem_limit_bytes=VMEM_LIMIT)


def _full(shape):
    n = len(shape)
    return pl.BlockSpec(shape, lambda *_: (0,) * n, pipeline_mode=pl.Buffered(1))


def _s5_prep_kernel(are_ref, aim_ref, ldt_ref, bre_ref, bim_ref, abre_ref, abim_ref, bbre_ref, bbim_ref):
    a_re = are_ref[...]
    a_im = aim_ref[...]
    dt = jnp.exp(ldt_ref[...])
    mag = jnp.exp(a_re * dt)
    ab_re = mag * jnp.cos(a_im * dt)
    ab_im = mag * jnp.sin(a_im * dt)
    den = a_re * a_re + a_im * a_im
    x_re = ab_re - 1.0
    f_re = (x_re * a_re + ab_im * a_im) / den
    f_im = (ab_im * a_re - x_re * a_im) / den
    b_re = bre_ref[...]
    b_im = bim_ref[...]
    abre_ref[...] = ab_re
    abim_ref[...] = ab_im
    bbre_ref[...] = f_re * b_re - f_im * b_im
    bbim_ref[...] = f_re * b_im + f_im * b_re


def _s5_prep(a_re, a_im, log_dt, b_re, b_im):
    g, p, c = b_re.shape
    out = pl.pallas_call(
        _s5_prep_kernel,
        out_shape=(jax.ShapeDtypeStruct((g, 1, p), F32), jax.ShapeDtypeStruct((g, 1, p), F32),
                   jax.ShapeDtypeStruct((g, c, p), F32), jax.ShapeDtypeStruct((g, c, p), F32)),
        name="s5_prep",
    )(a_re.reshape(g, 1, p), a_im.reshape(g, 1, p), log_dt.reshape(g, 1, 1),
      jnp.swapaxes(b_re, 1, 2), jnp.swapaxes(b_im, 1, 2))
    return out


def _s5_matrices(ab_re, ab_im, bb_re, bb_im, c_re, c_im):
    g, c, p = bb_re.shape
    nsl = g * c // LANES
    gs = LANES // c
    eye = jnp.eye(gs, dtype=F32)

    def bmat(bb):
        t = bb.reshape(nsl, gs, c, p)
        w = jnp.einsum('mgcp,hg->mhcgp', t, eye)
        return w.reshape(nsl, gs * c, gs * p)

    def cmat(cc):
        t = cc.reshape(nsl, gs, c, p)
        w = jnp.einsum('mgcp,hg->mhpgc', t, eye)
        return w.reshape(nsl, gs * p, gs * c)

    b_all = jnp.stack([bmat(bb_re), bmat(bb_im)], axis=1).reshape(nsl * 2, gs * c, gs * p)
    c_all = jnp.concatenate([cmat(c_re), -cmat(c_im)], axis=1)
    nst = g * p // (SUBLANES * LANES)
    a_all = jnp.concatenate([ab_re.reshape(nst, SUBLANES, LANES), ab_im.reshape(nst, SUBLANES, LANES)], axis=0)
    return a_all, b_all.astype(BF16), c_all.astype(BF16)


def _rs_kernel(*refs, tl, chunk):
    s0_ref, h0_ref = refs[9], refs[10]
    s_ref, h_ref, bu_ref = refs[20], refs[21], refs[22]

    @pl.when(pl.program_id(1) == 0)
    def _():
        s_ref[...] = s0_ref[...]
        h_ref[...] = h0_ref[...]

    for sub in range(tl // chunk):
        _rs_chunk(*refs[:22], bu_ref.at[sub], rows=slice(sub * chunk, (sub + 1) * chunk), chunk=chunk)


def _rs_chunk(x_ref, gn_ref, win_ref, cos_ref, sin_ref, decay_ref, wq_ref, wkv_ref, gc_ref,
              s0_ref, h0_ref, rgn_ref, a_ref, bm_ref, cm_ref, dsk_ref, wglu_ref, bglu_ref, wout_ref,
              xo_ref, s_ref, h_ref, bu, *, rows, chunk):
    qk_w = RET_HEADS * RET_DK
    nsl = S5_GROUPS * S5_GROUP // LANES
    nst = S5_GROUPS * S5_STATE // (SUBLANES * LANES)
    rows_per_slab = SUBLANES // (nsl // nst)

    x = x_ref[rows, :]
    xn = _rms(x, gn_ref[...]).astype(BF16)
    proj = _dot(xn, win_ref[...])
    cos = cos_ref[rows, :]
    sin = sin_ref[rows, :]

    u_off = 4 * qk_w
    for m in range(nsl):
        um = proj[:, u_off + m * LANES: u_off + (m + 1) * LANES].astype(BF16)
        for ri in range(2):
            r = _dot(um, bm_ref[2 * m + ri])
            slab = ri * nst + m // (nsl // nst)
            for jl in range(rows_per_slab):
                j = rows_per_slab * (m % (nsl // nst)) + jl
                bu[slab, pl.ds(j, chunk, stride=SUBLANES), :] = r[:, jl * LANES:(jl + 1) * LANES]

    a = [a_ref[i] for i in range(2 * nst)]

    def scan_steps(carry, t0, t1):
        for t in range(t0, t1):
            trow = pl.ds(t * SUBLANES, SUBLANES)
            new = []
            for s in range(nst):
                hr, hi = carry[s], carry[nst + s]
                ar, ai = a[s], a[nst + s]
                nr = ar * hr - ai * hi + bu[s, trow, :]
                ni = ar * hi + ai * hr + bu[nst + s, trow, :]
                bu[s, trow, :] = nr
                bu[nst + s, trow, :] = ni
                new.append((nr, ni))
            carry = tuple(n[0] for n in new) + tuple(n[1] for n in new)
        return carry

    def rope_head(h):
        q = proj[:, h * RET_DK:(h + 1) * RET_DK]
        k = proj[:, qk_w + h * RET_DK: qk_w + (h + 1) * RET_DK]
        qr = q * cos + pltpu.roll(q, RET_DK // 2, 1) * sin
        kr = (k * cos + pltpu.roll(k, RET_DK // 2, 1) * sin) * (RET_DK ** -0.5)
        qb = qr.astype(BF16)
        return qb, kr, _dot_nt(qb, kr.astype(BF16)), _dot(qb, s_ref[h].astype(BF16))

    carry = tuple(h_ref[i] for i in range(2 * nst))
    per_head = chunk // RET_HEADS
    pieces = []
    nxt = rope_head(0)
    for h in range(RET_HEADS):
        hs = slice(h * RET_DK, (h + 1) * RET_DK)
        qb, kr, sqk, cross = nxt
        vb = proj[:, 2 * qk_w + h * RET_DK: 2 * qk_w + (h + 1) * RET_DK].astype(BF16)
        g = proj[:, 3 * qk_w + h * RET_DK: 3 * qk_w + (h + 1) * RET_DK]
        o = _dot((sqk * decay_ref[h]).astype(BF16), vb) + cross * wq_ref[:, hs]
        if h + 1 < RET_HEADS:
            nxt = rope_head(h + 1)
        carry = scan_steps(carry, h * per_head, (h + 1) * per_head)
        s_ref[h] = gc_ref[h] * s_ref[h] + _dot_tn((kr * wkv_ref[:, hs]).astype(BF16), vb)
        mu = jnp.mean(o, axis=-1, keepdims=True)
        var = jnp.maximum(jnp.mean(o * o, axis=-1, keepdims=True) - mu * mu, 0.0)
        y = (o - mu) * lax.rsqrt(var + EPS) * rgn_ref[:, hs]
        pieces.append((g * _sigmoid(g) * y).astype(BF16))
    carry = scan_steps(carry, RET_HEADS * per_head, chunk)
    for i in range(2 * nst):
        h_ref[i] = carry[i]
    out = x + _dot(jnp.concatenate(pieces, axis=1), wout_ref[0:qk_w, :])

    ys = []
    for m in range(nsl):
        parts = []
        for ri in range(2):
            slab = ri * nst + m // (nsl // nst)
            for jl in range(rows_per_slab):
                j = rows_per_slab * (m % (nsl // nst)) + jl
                parts.append(bu[slab, pl.ds(j, chunk, stride=SUBLANES), :])
        hcat = jnp.concatenate(parts, axis=1).astype(BF16)
        um = proj[:, u_off + m * LANES: u_off + (m + 1) * LANES]
        ym = _dot(hcat, cm_ref[m]) + dsk_ref[:, m * LANES:(m + 1) * LANES] * um
        ys.append(_gelu_tanh(ym))
    yg = jnp.concatenate(ys, axis=1)
    gate = _sigmoid(_dot(yg.astype(BF16), wglu_ref[...]) + bglu_ref[...])
    xo_ref[rows, :] = out + _dot((yg * gate).astype(BF16), wout_ref[qk_w:, :])


def _rs_layer(x, st_ret, st_h, consts, w):
    b, L, d = x.shape
    chunk = min(CHUNK, L)
    tl = min(MIXER_TILE, L)
    nl = L // tl
    cos, sin, decay, wq, wkv, gc = consts
    rs_in = w['rs_w_in'].shape[1]
    nst2 = st_h.shape[1]
    kern = functools.partial(_rs_kernel, tl=tl, chunk=chunk)
    tok = pl.BlockSpec((None, tl, d), lambda i, j: (i, j, 0))
    in_specs = [
        tok, _full((1, d)), _full((d, rs_in)),
        pl.BlockSpec((tl, LANES), lambda i, j: (j, 0)), pl.BlockSpec((tl, LANES), lambda i, j: (j, 0)),
        _full(decay.shape), _full(wq.shape), _full(wkv.shape), _full(gc.shape),
        pl.BlockSpec((None,) + st_ret.shape[1:], lambda i, j: (i, 0, 0, 0)),
        pl.BlockSpec((None,) + st_h.shape[1:], lambda i, j: (i, 0, 0, 0)),
        _full((1, RET_HEADS * RET_DK)), _full(w['s5_a'].shape), _full(w['s5_b'].shape), _full(w['s5_c'].shape),
        _full((1, w['s5_d'].shape[1])), _full(w['s5_w_glu'].shape), _full((1, w['s5_b_glu'].shape[1])),
        _full(w['rs_w_out'].shape),
    ]
    out_specs = [
        tok,
        pl.BlockSpec((None,) + st_ret.shape[1:], lambda i, j: (i, 0, 0, 0)),
        pl.BlockSpec((None,) + st_h.shape[1:], lambda i, j: (i, 0, 0, 0)),
    ]
    return pl.pallas_call(
        kern, grid=(b, nl), in_specs=in_specs, out_specs=out_specs,
        out_shape=(jax.ShapeDtypeStruct(x.shape, F32), jax.ShapeDtypeStruct(st_ret.shape, F32),
                   jax.ShapeDtypeStruct(st_h.shape, F32)),
        scratch_shapes=[pltpu.VMEM((tl // chunk, nst2, chunk * SUBLANES, LANES), F32)],
        compiler_params=_params("parallel", "arbitrary"), name="rs_mixer",
    )(x, w['norm_mix0'], w['rs_w_in'], cos, sin, decay, wq, wkv, gc, st_ret, st_h, w['ret_gn'],
      w['s5_a'], w['s5_b'], w['s5_c'], w['s5_d'], w['s5_w_glu'], w['s5_b_glu'], w['rs_w_out'])


def _ret_consts(L, tl, pos0):
    half = RET_DK // 2
    pos = pos0 + jnp.arange(L, dtype=jnp.int32)
    inv = ROPE_BASE ** (-jnp.arange(half, dtype=F32) / half)
    ang = pos.astype(F32)[:, None] * inv[None, :]
    cos = jnp.cos(ang)
    sin = jnp.sin(ang)
    cos2 = jnp.concatenate([cos, cos], axis=1)
    sin2 = jnp.concatenate([-sin, sin], axis=1)
    log_g = jnp.log1p(-jnp.exp2(-5.0 - jnp.arange(RET_HEADS, dtype=F32)))
    idx = jnp.arange(tl, dtype=F32)
    diff = idx[:, None] - idx[None, :]
    decay = jnp.where(diff[None] >= 0, jnp.exp(jnp.maximum(diff, 0.0)[None] * log_g[:, None, None]), 0.0)
    w_q = jnp.exp((idx + 1.0)[:, None] * log_g[None, :])
    w_kv = jnp.exp((tl - 1.0 - idx)[:, None] * log_g[None, :])
    g_c = jnp.exp(tl * log_g)
    rep = lambda t: jnp.repeat(t, RET_DK, axis=1)
    gc = jnp.broadcast_to(g_c[:, None, None], (RET_HEADS, 1, RET_DK))
    return cos2, sin2, decay, rep(w_q), rep(w_kv), gc


def _gate_fold_kernel(wq_ref, wk_ref, wv_ref, wg_ref, o_ref):
    inner = ML_HEADS * ML_DH
    hp = lax.Precision.HIGHEST
    for h in range(ML_HEADS):
        rows = slice(h * ML_DH, (h + 1) * ML_DH)
        gq = wg_ref[h * ML_DH:(h + 1) * ML_DH, :]
        gk = wg_ref[inner + h * ML_DH: inner + (h + 1) * ML_DH, :]
        gv = wg_ref[2 * inner + h * ML_DH: 2 * inner + (h + 1) * ML_DH, :]
        o_ref[0, rows, :] = (jnp.dot(wq_ref[h], gq, precision=hp, preferred_element_type=F32)
                             + jnp.dot(wk_ref[h], gk, precision=hp, preferred_element_type=F32))
        o_ref[1, rows, :] = jnp.dot(wv_ref[h], gv, precision=hp, preferred_element_type=F32)


def _gate_fold(wq, wk, wv, wg_padded):
    inner = ML_HEADS * ML_DH
    return pl.pallas_call(
        _gate_fold_kernel, out_shape=jax.ShapeDtypeStruct((2, inner, LANES), F32), name="gate_fold",
    )(wq, wk, wv, wg_padded)


def _ml_a_kernel(x_ref, gn_ref, win_ref, cs_ref, cw_ref, cb_ref, wq_ref, wkt_ref, wv_ref, wg_ref, bg_ref, skip_ref,
                 q_ref, kt_ref, v_ref, gates_ref, sz_ref, sx_ref, nc_ref, xs_ref, *, tl, nl):
    l = pl.program_id(1)
    inner = ML_HEADS * ML_DH
    pad = SUBLANES
    hist = ML_CONV - 1

    @pl.when(l == 0)
    def _():
        xs_ref[0:pad, :] = jnp.zeros((pad, inner), F32)
        xs_ref[pl.ds(pad - hist, hist), :] = cs_ref[...]

    xn = _rms(x_ref[...], gn_ref[...]).astype(BF16)
    xm = _dot(xn, win_ref[:, 0:inner])
    sz_ref[...] = _sigmoid(_dot(xn, win_ref[:, inner:2 * inner])).astype(sz_ref.dtype)
    xs_ref[pl.ds(pad, tl), :] = xm
    acc = cw_ref[hist:hist + 1, :] * xm
    for j in range(hist):
        acc = acc + cw_ref[j:j + 1, :] * xs_ref[pl.ds(pad - hist + j, tl), :]
    xc = acc + cb_ref[...]
    xc = xc * _sigmoid(xc)
    sx_ref[...] = (skip_ref[...] * xc).astype(sx_ref.dtype)
    xcb = xc.astype(BF16)
    xmb = xm.astype(BF16)
    gates_ref[...] = _dot(xcb, wg_ref[0]) + _dot(xmb, wg_ref[1]) + bg_ref[...]
    for h in range(ML_HEADS):
        hs = slice(h * ML_DH, (h + 1) * ML_DH)
        q_ref[:, hs] = _dot(xcb[:, hs], wq_ref[h]).astype(BF16)
        kt_ref[h] = _dot_nt(wkt_ref[h], xcb[:, hs]).astype(BF16)
        v_ref[:, hs] = _dot(xmb[:, hs], wv_ref[h]).astype(BF16)

    @pl.when(l == nl - 1)
    def _():
        nc_ref[...] = xs_ref[pl.ds(pad + tl - hist, hist), :]

    xs_ref[0:pad, :] = xs_ref[pl.ds(tl, pad), :]


def _ml_a(x, conv_state, w):
    b, L, d = x.shape
    tl = min(PROJ_TILE, L)
    nl = L // tl
    inner = ML_HEADS * ML_DH
    kern = functools.partial(_ml_a_kernel, tl=tl, nl=nl)
    tok = lambda width: pl.BlockSpec((None, tl, width), lambda i, j: (i, j, 0))
    kt_spec = pl.BlockSpec((None, ML_HEADS, ML_DH, tl), lambda i, j: (i, 0, 0, j))
    cst = pl.BlockSpec((None, ML_CONV - 1, inner), lambda i, j: (i, 0, 0))
    in_specs = [tok(d), _full((1, d)), _full((d, 2 * inner)), cst, _full((ML_CONV, inner)), _full((1, inner)),
                _full(w['ml_wq'].shape), _full(w['ml_wkt'].shape), _full(w['ml_wv'].shape),
                _full(w['ml_w_gates'].shape), _full((1, LANES)), _full((1, inner))]
    out_specs = [tok(inner), kt_spec, tok(inner), tok(LANES), tok(inner), tok(inner), cst]
    sds = jax.ShapeDtypeStruct
    out_shape = (sds((b, L, inner), BF16), sds((b, ML_HEADS, ML_DH, L), BF16), sds((b, L, inner), BF16),
                 sds((b, L, LANES), F32), sds((b, L, inner), BF16), sds((b, L, inner), BF16),
                 sds((b, ML_CONV - 1, inner), F32))
    return pl.pallas_call(
        kern, grid=(b, nl), in_specs=in_specs, out_specs=out_specs, out_shape=out_shape,
        scratch_shapes=[pltpu.VMEM((tl + SUBLANES, inner), F32)],
        compiler_params=_params("parallel", "arbitrary"), name="mlstm_proj",
    )(x, w['norm_mix1'], w['ml_w_in'], conv_state, w['ml_conv_w'], w['ml_conv_b'],
      w['ml_wq'], w['ml_wkt'], w['ml_wv'], w['ml_w_gates'], w['ml_b_gates'], w['ml_skip'])


def _split3(x):
    hi = x.astype(BF16)
    r1 = x - hi.astype(F32)
    mid = r1.astype(BF16)
    lo = (r1 - mid.astype(F32)).astype(BF16)
    return hi, mid, lo


def _ml_b_kernel(*refs, tl, chunk, nl, has_state):
    if has_state:
        (x_ref, q_ref, kt_ref, v_ref, g_ref, sz_ref, sx_ref, c0_ref, n0_ref, m0_ref, gn_ref, wout_ref,
         xo_ref, c_ref, n_ref, m_ref, ncm_ref) = refs
    else:
        (x_ref, q_ref, kt_ref, v_ref, g_ref, sz_ref, sx_ref, gn_ref, wout_ref,
         xo_ref, c_ref, n_ref, m_ref, ncm_ref) = refs
    l = pl.program_id(1)

    @pl.when(l == 0)
    def _():
        if has_state:
            c_ref[...] = c0_ref[...]
            m_ref[...] = m0_ref[...]
            for h in range(ML_HEADS):
                ncm_ref[h] = jnp.broadcast_to(n0_ref[h:h + 1, :], (LANES, ML_DH)).T
        else:
            c_ref[...] = jnp.zeros(c_ref.shape, F32)
            m_ref[...] = jnp.zeros(m_ref.shape, F32)
            ncm_ref[...] = jnp.zeros(ncm_ref.shape, F32)

    for sub in range(tl // chunk):
        rows = slice(sub * chunk, (sub + 1) * chunk)
        _ml_b_chunk(x_ref.at[rows], q_ref.at[rows], kt_ref.at[:, :, rows], v_ref.at[rows], g_ref.at[rows],
                    sz_ref.at[rows], sx_ref.at[rows], gn_ref, wout_ref, xo_ref.at[rows], c_ref, m_ref, ncm_ref,
                    tl=chunk)

    @pl.when(l == nl - 1)
    def _():
        for h in range(ML_HEADS):
            n_ref[h:h + 1, :] = ncm_ref[h].T[0:1, :]


def _ml_b_chunk(x_ref, q_ref, kt_ref, v_ref, g_ref, sz_ref, sx_ref, gn_ref, wout_ref, xo_ref, c_ref, m_ref, ncm_ref,
                *, tl):
    scale = ML_DH ** -0.5
    gates = g_ref[...]
    row = lax.broadcasted_iota(jnp.int32, (tl, tl), 0)
    col = lax.broadcasted_iota(jnp.int32, (tl, tl), 1)
    causal = row >= col
    lower = jnp.where(causal, 1.0, 0.0).astype(BF16)
    ones = jnp.ones((tl, LANES), BF16)
    lf_c = _split3(_log_sigmoid(gates))
    b_f = _dot(lower, lf_c[0]) + _dot(lower, lf_c[1]) + _dot(lower, lf_c[2])
    b_c = pltpu.roll(b_f, LANES - ML_HEADS, 1)
    lane = lax.broadcasted_iota(jnp.int32, (1, LANES), 1)
    m_prev_row = jnp.zeros((1, LANES), F32)
    for h in range(ML_HEADS):
        m_prev_row = jnp.where(lane == h, m_ref[h], m_prev_row)
    run = gates - b_c
    r_t = run.T
    trow = lax.broadcasted_iota(jnp.int32, (tl, LANES), 0)
    shift = 1
    while shift < tl:
        run = jnp.maximum(run, jnp.where(trow >= shift, pltpu.roll(run, shift, 0), -jnp.inf))
        shift *= 2
    m_t_all = b_c + jnp.maximum(m_prev_row, run)
    s_inter_all = jnp.exp(b_c + m_prev_row - m_t_all)
    dmin_all = jnp.exp(-m_t_all)
    a_all = b_c - m_t_all + math.log(scale)
    m_new_row = m_t_all[tl - 1:tl, :]
    b_last_row = b_c[tl - 1:tl, :]
    dec_row = jnp.exp(b_last_row + m_prev_row - m_new_row)
    wsc_row = b_last_row - m_new_row

    hsl = [slice(h * ML_DH, (h + 1) * ML_DH) for h in range(ML_HEADS)]

    def state_dots(h):
        q = q_ref[:, hsl[h]]
        return (_dot(q, kt_ref[h]), _dot(q, c_ref[h].astype(BF16)), _dot(q, ncm_ref[h].astype(BF16))[:, 0:1])

    out = x_ref[...]
    nxt = state_dots(0)
    for h in range(ML_HEADS):
        hs = hsl[h]
        sqk, qc, qn = nxt
        v = v_ref[:, hs]
        r_row = r_t[h:h + 1, :]
        s_inter = s_inter_all[:, h:h + 1]
        dec = dec_row[:, h:h + 1]
        kwt = (kt_ref[h].astype(F32) * (scale * jnp.exp(r_row + wsc_row[:, h:h + 1]))).astype(BF16)
        c_ref[h] = dec * c_ref[h] + _dot(kwt, v)
        ncm_ref[h] = dec * ncm_ref[h] + _dot(kwt, ones)
        m_ref[h] = m_new_row[:, h:h + 1]
        wgt = jnp.exp(jnp.where(causal, a_all[:, h:h + 1] + r_row, -jnp.inf))
        qk = sqk * wgt
        num = s_inter * qc + _dot(qk.astype(BF16), v)
        den = s_inter * qn + jnp.sum(qk, axis=-1, keepdims=True)
        if h + 1 < ML_HEADS:
            nxt = state_dots(h + 1)
        dmax = jnp.maximum(jnp.abs(den), dmin_all[:, h:h + 1])
        xg = sz_ref[:, hs].astype(F32) * num
        mu = jnp.mean(xg, axis=-1, keepdims=True)
        var = jnp.maximum(jnp.mean(xg * xg, axis=-1, keepdims=True) - mu * mu, 0.0)
        y = (xg - mu) * lax.rsqrt(var + EPS * (dmax * dmax)) * gn_ref[:, hs] + sx_ref[:, hs].astype(F32)
        out = out + _dot(y.astype(BF16), wout_ref[hs, :])
    xo_ref[...] = out


def _ml_b(x, q, kt, v, gates, sz, sx, state, w):
    b, L, d = x.shape
    chunk = min(CHUNK, L)
    tl = min(CELL_TILE, L)
    nl = L // tl
    inner = ML_HEADS * ML_DH
    has_state = state is not None
    kern = functools.partial(_ml_b_kernel, tl=tl, chunk=chunk, nl=nl, has_state=has_state)
    tok = lambda width: pl.BlockSpec((None, tl, width), lambda i, j: (i, j, 0))
    kt_spec = pl.BlockSpec((None, ML_HEADS, ML_DH, tl), lambda i, j: (i, 0, 0, j))
    c_spec = pl.BlockSpec((None, ML_HEADS, ML_DH, ML_DH), lambda i, j: (i, 0, 0, 0))
    n_spec = pl.BlockSpec((None, ML_HEADS, ML_DH), lambda i, j: (i, 0, 0))
    m_spec = pl.BlockSpec((None, ML_HEADS, 1, 1), lambda i, j: (i, 0, 0, 0))
    in_specs = [tok(d), tok(inner), kt_spec, tok(inner), tok(LANES), tok(inner), tok(inner)]
    args = [x, q, kt, v, gates, sz, sx]
    if has_state:
        in_specs += [c_spec, n_spec, m_spec]
        args += [state[0], state[1], state[2].reshape(b, ML_HEADS, 1, 1)]
    in_specs += [_full((1, inner)), _full((inner, d))]
    args += [w['ml_gn'], w['ml_w_out']]
    sds = jax.ShapeDtypeStruct
    out_shape = (sds(x.shape, F32), sds((b, ML_HEADS, ML_DH, ML_DH), F32), sds((b, ML_HEADS, ML_DH), F32),
                 sds((b, ML_HEADS, 1, 1), F32))
    xo, c, n, m = pl.pallas_call(
        kern, grid=(b, nl), in_specs=in_specs, out_specs=[tok(d), c_spec, n_spec, m_spec], out_shape=out_shape,
        scratch_shapes=[pltpu.VMEM((ML_HEADS, ML_DH, LANES), F32)],
        compiler_params=_params("parallel", "arbitrary"), name="mlstm_cell",
    )(*args)
    return xo, c, n, m.reshape(b, ML_HEADS)


def _memkv_kernel(m_ref, g_ref, wk_ref, wv_ref, k_ref, v_ref, kb_ref, vb_ref):
    nb, m_len = k_ref.shape[0], k_ref.shape[1]
    mn = _rms(m_ref[...], g_ref[...]).astype(BF16)
    k = _dot(mn, wk_ref[...])
    v = _dot(mn, wv_ref[...])
    for h in range(X_HEADS):
        kh = k[:, h * X_DH:(h + 1) * X_DH].reshape(nb, m_len, X_DH)
        vh = v[:, h * X_DH:(h + 1) * X_DH].reshape(nb, m_len, X_DH)
        k_ref[:, :, h, :] = kh
        v_ref[:, :, h, :] = vh
        kb_ref[:, h] = kh.astype(BF16)
        vb_ref[:, h] = vh.astype(BF16)


def _memkv(mem, norm_mem, wk, wv):
    b, m_len, d = mem.shape
    depth = wk.shape[0]
    rows = b * m_len
    tm = min(MEM_ROW_TILE, rows)
    nb = tm // m_len
    mem2 = mem.reshape(rows, d)
    wspec = pl.BlockSpec((None, d, d), lambda i, j: (i, 0, 0))
    ospec = pl.BlockSpec((None, nb, m_len, X_HEADS, X_DH), lambda i, j: (i, j, 0, 0, 0))
    oshape = jax.ShapeDtypeStruct((depth, b, m_len, X_HEADS, X_DH), F32)
    bspec = pl.BlockSpec((None, nb, X_HEADS, m_len, X_DH), lambda i, j: (i, j, 0, 0, 0))
    bshape = jax.ShapeDtypeStruct((depth, b, X_HEADS, m_len, X_DH), BF16)
    return pl.pallas_call(
        _memkv_kernel, grid=(depth, rows // tm),
        in_specs=[pl.BlockSpec((tm, d), lambda i, j: (j, 0)), pl.BlockSpec((None, 1, d), lambda i, j: (i, 0, 0)),
                  wspec, wspec],
        out_specs=[ospec, ospec, bspec, bspec],
        out_shape=(oshape, oshape, bshape, bshape),
        compiler_params=_params("parallel", "parallel"), name="memory_kv",
    )(mem2, norm_mem.reshape(depth, 1, d), wk, wv)


def _xattn_kernel(x_ref, g_ref, wq_ref, mk_ref, mv_ref, wo_ref, o_ref):
    x = x_ref[...]
    xn = _rms(x, g_ref[...]).astype(BF16)
    q = _dot(xn, wq_ref[...])
    hsl = [slice(h * X_DH, (h + 1) * X_DH) for h in range(X_HEADS)]

    def scores(h):
        return _dot_nt(q[:, hsl[h]].astype(BF16), mk_ref[h])

    out = x
    nxt = scores(0)
    for h in range(X_HEADS):
        s = nxt * (X_DH ** -0.5)
        if h + 1 < X_HEADS:
            nxt = scores(h + 1)
        e = jnp.exp(s - jnp.max(s, axis=-1, keepdims=True))
        pv = _dot(e.astype(BF16), mv_ref[h])
        oh = pv * (1.0 / jnp.sum(e, axis=-1, keepdims=True))
        out = out + _dot(oh.astype(BF16), wo_ref[hsl[h], :])
    o_ref[...] = out


def _xattn(x, layer, g, wq, mk, mv, wo):
    b, L, d = x.shape
    tl = min(XATTN_TILE, L)
    m_len = mk.shape[3]
    tok = pl.BlockSpec((None, tl, d), lambda i, j: (i, j, 0))
    mem = pl.BlockSpec((None, None, X_HEADS, m_len, X_DH), lambda i, j: (layer, i, 0, 0, 0))
    wsp = pl.BlockSpec((None, d, d), lambda i, j: (layer, 0, 0))
    gsp = pl.BlockSpec((None, 1, d), lambda i, j: (layer, 0, 0))
    return pl.pallas_call(
        _xattn_kernel, grid=(b, L // tl),
        in_specs=[tok, gsp, wsp, mem, mem, wsp],
        out_specs=tok, out_shape=jax.ShapeDtypeStruct(x.shape, F32),
        compiler_params=_params("parallel", "arbitrary"), name="cross_attn",
    )(x, g, wq, mk, mv, wo)


def _mlp_kernel(*refs, final):
    if final:
        x_ref, g_ref, wu_ref, wd_ref, gf_ref, o_ref, xn_ref = refs
    else:
        x_ref, g_ref, wu_ref, wd_ref, o_ref, xn_ref = refs
    j = pl.program_id(1)

    @pl.when(j == 0)
    def _():
        x = x_ref[...]
        xn_ref[...] = _rms(x, g_ref[...]).astype(BF16)
        o_ref[...] = x

    hdn = jnp.maximum(_dot(xn_ref[...], wu_ref[...]), 0.0)
    o_ref[...] += _dot((hdn * hdn).astype(BF16), wd_ref[...])

    if final:
        @pl.when(j == pl.num_programs(1) - 1)
        def _():
            o_ref[...] = _rms(o_ref[...], gf_ref[...])


def _mlp(x, layer, g, wu, wd, gf=None):
    shape = x.shape
    d = shape[-1]
    x2 = x.reshape(-1, d)
    rows = x2.shape[0]
    ff = wu.shape[2]
    tm = min(MLP_ROW_TILE, rows)
    tf = min(MLP_FF_TILE, ff)
    final = gf is not None
    in_specs = [pl.BlockSpec((tm, d), lambda i, j: (i, 0)), pl.BlockSpec((None, 1, d), lambda i, j: (layer, 0, 0)),
                pl.BlockSpec((None, d, tf), lambda i, j: (layer, 0, j)),
                pl.BlockSpec((None, tf, d), lambda i, j: (layer, j, 0))]
    args = [x2, g, wu, wd]
    if final:
        in_specs.append(_full((1, d)))
        args.append(gf)
    out = pl.pallas_call(
        functools.partial(_mlp_kernel, final=final), grid=(rows // tm, ff // tf),
        in_specs=in_specs, out_specs=pl.BlockSpec((tm, d), lambda i, j: (i, 0)),
        out_shape=jax.ShapeDtypeStruct(x2.shape, F32),
        scratch_shapes=[pltpu.VMEM((tm, d), BF16)],
        compiler_params=_params("parallel", "arbitrary"), name="mlp",
    )(*args)
    return out.reshape(shape)


def _trunk(x, pos0, st_ret, st_re, st_im, ml_state, conv_state, mem_k, mem_v, w):
    b, L, d = x.shape
    chunk = min(CHUNK, L)
    nst = S5_GROUPS * S5_STATE // (SUBLANES * LANES)
    st_h = jnp.concatenate([st_re.reshape(b, nst, SUBLANES, LANES), st_im.reshape(b, nst, SUBLANES, LANES)], axis=1)
    x, s_new, h_new = _rs_layer(x, st_ret, st_h, _ret_consts(L, chunk, pos0), w)
    hr = h_new[:, :nst].reshape(b, S5_GROUPS, S5_STATE)
    hi = h_new[:, nst:].reshape(b, S5_GROUPS, S5_STATE)
    x = _xattn(x, 0, w['norm_cross'], w['x_wq'], mem_k, mem_v, w['x_wo'])
    x = _mlp(x, 0, w['norm_mlp'], w['mlp_w_up'], w['mlp_w_down'])
    q, kt, v, gates, sz, sx, new_conv = _ml_a(x, conv_state, w)
    x, cf, nf, mf = _ml_b(x, q, kt, v, gates, sz, sx, ml_state, w)
    x = _xattn(x, 1, w['norm_cross'], w['x_wq'], mem_k, mem_v, w['x_wo'])
    y = _mlp(x, 1, w['norm_mlp'], w['mlp_w_up'], w['mlp_w_down'], w['norm_final'])
    return y, s_new[None], hr[None], hi[None], cf[None], nf[None], mf[None], new_conv[None]


def kernel(x_prompt, x_sample, mem_prompt, state_ret, state_s5_re, state_s5_im, state_mlstm_c, state_mlstm_n, state_mlstm_m, cache_mlstm_conv, cache_mem_k, cache_mem_v, norm_mix, norm_cross, norm_mem, norm_mlp, norm_final, rs_w_in, rs_w_out, ret_gn, s5_a_re, s5_a_im, s5_log_dt, s5_b_re, s5_b_im, s5_c_re, s5_c_im, s5_d, s5_w_glu, s5_b_glu, ml_w_in, ml_conv_w, ml_conv_b, ml_wq, ml_wk, ml_wv, ml_w_gates, ml_b_gates, ml_gn, ml_skip, ml_w_out, x_wq, x_wk, x_wv, x_wo, mlp_w_up, mlp_w_down):
    d = x_prompt.shape[-1]
    bp = x_prompt.shape[0]
    bf = lambda t: t.astype(BF16)
    row = lambda t: t.reshape(1, -1).astype(F32)
    ab_re, ab_im, bb_re, bb_im = _s5_prep(s5_a_re[0], s5_a_im[0], s5_log_dt[0], s5_b_re[0], s5_b_im[0])
    s5_a, s5_b, s5_c = _s5_matrices(ab_re, ab_im, bb_re, bb_im, s5_c_re[0], s5_c_im[0])
    n_gate = ml_w_gates.shape[-1]
    w = {
        'norm_mix0': row(norm_mix[0]), 'norm_mix1': row(norm_mix[1]),
        'norm_cross': norm_cross.reshape(-1, 1, d), 'norm_mlp': norm_mlp.reshape(-1, 1, d),
        'norm_final': row(norm_final),
        'rs_w_in': bf(rs_w_in[0]), 'rs_w_out': bf(rs_w_out[0]), 'ret_gn': row(ret_gn[0]),
        's5_a': s5_a, 's5_b': s5_b, 's5_c': s5_c, 's5_d': row(s5_d[0]), 's5_w_glu': bf(s5_w_glu[0]),
        's5_b_glu': row(s5_b_glu[0]),
        'ml_w_in': bf(ml_w_in[0]), 'ml_conv_w': ml_conv_w[0], 'ml_conv_b': row(ml_conv_b[0]),
        'ml_wq': bf(ml_wq[0]), 'ml_wkt': bf(jnp.swapaxes(ml_wk[0], 1, 2)), 'ml_wv': bf(ml_wv[0]),
        'ml_w_gates': bf(_gate_fold(ml_wq[0], ml_wk[0], ml_wv[0],
                                    jnp.pad(ml_w_gates[0], ((0, 0), (0, LANES - n_gate))))),
        'ml_b_gates': jnp.pad(ml_b_gates[0], (0, LANES - n_gate)).reshape(1, LANES),
        'ml_gn': row(ml_gn[0]), 'ml_skip': row(ml_skip[0]), 'ml_w_out': bf(ml_w_out[0]),
        'x_wq': bf(x_wq), 'x_wo': bf(x_wo), 'mlp_w_up': bf(mlp_w_up), 'mlp_w_down': bf(mlp_w_down),
    }
    mk_p, mv_p, mkb_p, mvb_p = _memkv(mem_prompt, norm_mem, bf(x_wk), bf(x_wv))
    zeros = lambda *s: jnp.zeros(s, F32)
    out_p = _trunk(x_prompt, 0, zeros(bp, RET_HEADS, RET_DK, RET_DK), zeros(bp, S5_GROUPS, S5_STATE),
                   zeros(bp, S5_GROUPS, S5_STATE), None, zeros(bp, ML_CONV - 1, ML_HEADS * ML_DH), mkb_p, mvb_p, w)
    out_s = _trunk(x_sample, PAST_LEN, state_ret[0], state_s5_re[0], state_s5_im[0],
                   (state_mlstm_c[0], state_mlstm_n[0], state_mlstm_m[0]), cache_mlstm_conv[0],
                   bf(jnp.swapaxes(cache_mem_k, 2, 3)), bf(jnp.swapaxes(cache_mem_v, 2, 3)), w)
    return (out_p[0], out_s[0]) + tuple(out_p[1:]) + (mk_p, mv_p) + tuple(out_s[1:])
```

```python
import functools
import math

import jax
import jax.numpy as jnp
from jax import lax
from jax.experimental import pallas as pl
from jax.experimental.pallas import tpu as pltpu

F32 = jnp.float32
BF16 = jnp.bfloat16

EPS = 1e-6
ROPE_BASE = 10000.0
PAST_LEN = 4096
RET_HEADS = 4
RET_DK = 128
S5_GROUPS = 32
S5_GROUP = 16
S5_STATE = 64
ML_HEADS = 4
ML_DH = 512
ML_CONV = 4
X_HEADS = 4
X_DH = 256
LANES = 128
SUBLANES = 8
VMEM_LIMIT = 56 * 1024 * 1024
CHUNK = 256
MIXER_TILE = 256
CELL_TILE = 256
PROJ_TILE = 512
XATTN_TILE = 1024
MLP_ROW_TILE = 1024
MLP_FF_TILE = 2048
MEM_ROW_TILE = 1024


def _dot(a, b):
    return jnp.dot(a, b, preferred_element_type=F32)


def _dot_nt(a, b):
    return lax.dot_general(a, b, (((1,), (1,)), ((), ())), preferred_element_type=F32)


def _dot_tn(a, b):
    return lax.dot_general(a, b, (((0,), (0,)), ((), ())), preferred_element_type=F32)


def _rms(x, g):
    y = x * lax.rsqrt(jnp.mean(x * x, axis=-1, keepdims=True) + EPS)
    return y * g


def _sigmoid(x):
    return 1.0 / (1.0 + jnp.exp(-x))


def _log_sigmoid(x):
    return -(jnp.maximum(-x, 0.0) + jnp.log(1.0 + jnp.exp(-jnp.abs(x))))


def _gelu_tanh(x):
    c = math.sqrt(2.0 / math.pi)
    return 0.5 * x * (1.0 + jnp.tanh(c * (x + 0.044715 * (x * x * x))))


def _params(*sem):
    return pltpu.CompilerParams(dimension_semantics=sem, vmem_limit_bytes=VMEM_LIMIT)


def _full(shape):
    n = len(shape)
    return pl.BlockSpec(shape, lambda *_: (0,) * n, pipeline_mode=pl.Buffered(1))


def _s5_prep_kernel(are_ref, aim_ref, ldt_ref, bre_ref, bim_ref, abre_ref, abim_ref, bbre_ref, bbim_ref):
    a_re = are_ref[...]
    a_im = aim_ref[...]
    dt = jnp.exp(ldt_ref[...])
    mag = jnp.exp(a_re * dt)
    ab_re = mag * jnp.cos(a_im * dt)
    ab_im = mag * jnp.sin(a_im * dt)
    den = a_re * a_re + a_im * a_im
    x_re = ab_re - 1.0
    f_re = (x_re * a_re + ab_im * a_im) / den
    f_im = (ab_im * a_re - x_re * a_im) / den
    b_re = bre_ref[...]
    b_im = bim_ref[...]
    abre_ref[...] = ab_re
    abim_ref[...] = ab_im
    bbre_ref[...] = f_re * b_re - f_im * b_im
    bbim_ref[...] = f_re * b_im + f_im * b_re


def _s5_prep(a_re, a_im, log_dt, b_re, b_im):
    g, p, c = b_re.shape
    out = pl.pallas_call(
        _s5_prep_kernel,
        out_shape=(jax.ShapeDtypeStruct((g, 1, p), F32), jax.ShapeDtypeStruct((g, 1, p), F32),
                   jax.ShapeDtypeStruct((g, c, p), F32), jax.ShapeDtypeStruct((g, c, p), F32)),
        name="s5_prep",
    )(a_re.reshape(g, 1, p), a_im.reshape(g, 1, p), log_dt.reshape(g, 1, 1),
      jnp.swapaxes(b_re, 1, 2), jnp.swapaxes(b_im, 1, 2))
    return out


def _s5_matrices(ab_re, ab_im, bb_re, bb_im, c_re, c_im):
    g, c, p = bb_re.shape
    nsl = g * c // LANES
    gs = LANES // c
    eye = jnp.eye(gs, dtype=F32)

    def bmat(bb):
        t = bb.reshape(nsl, gs, c, p)
        w = jnp.einsum('mgcp,hg->mhcgp', t, eye)
        return w.reshape(nsl, gs * c, gs * p)

    def cmat(cc):
        t = cc.reshape(nsl, gs, c, p)
        w = jnp.einsum('mgcp,hg->mhpgc', t, eye)
        return w.reshape(nsl, gs * p, gs * c)

    b_all = jnp.stack([bmat(bb_re), bmat(bb_im)], axis=1).reshape(nsl * 2, gs * c, gs * p)
    c_all = jnp.concatenate([cmat(c_re), -cmat(c_im)], axis=1)
    nst = g * p // (SUBLANES * LANES)
    a_all = jnp.concatenate([ab_re.reshape(nst, SUBLANES, LANES), ab_im.reshape(nst, SUBLANES, LANES)], axis=0)
    return a_all, b_all.astype(BF16), c_all.astype(BF16)


def _rs_kernel(*refs, tl, chunk):
    *chunk_refs, bu_ref = refs
    s0_ref, h0_ref = chunk_refs[9:11]
    s_ref, h_ref = chunk_refs[-2:]

    @pl.when(pl.program_id(1) == 0)
    def _():
        s_ref[...] = s0_ref[...]
        h_ref[...] = h0_ref[...]

    for sub in range(tl // chunk):
        _rs_chunk(*chunk_refs, bu_ref.at[sub], rows=slice(sub * chunk, (sub + 1) * chunk), chunk=chunk)


def _rs_chunk(x_ref, gn_ref, win_ref, cos_ref, sin_ref, decay_ref, wq_ref, wkv_ref, gc_ref,
              s0_ref, h0_ref, rgn_ref, a_ref, bm_ref, cm_ref, dsk_ref, wglu_ref, bglu_ref, wout_ref,
              xo_ref, s_ref, h_ref, bu, *, rows, chunk):
    qk_w = RET_HEADS * RET_DK
    nsl = S5_GROUPS * S5_GROUP // LANES
    nst = S5_GROUPS * S5_STATE // (SUBLANES * LANES)
    rows_per_slab = SUBLANES // (nsl // nst)

    x = x_ref[rows, :]
    xn = _rms(x, gn_ref[...]).astype(BF16)
    proj = _dot(xn, win_ref[...])
    cos = cos_ref[rows, :]
    sin = sin_ref[rows, :]

    u_off = 4 * qk_w
    for m in range(nsl):
        um = proj[:, u_off + m * LANES: u_off + (m + 1) * LANES].astype(BF16)
        for ri in range(2):
            r = _dot(um, bm_ref[2 * m + ri])
            slab = ri * nst + m // (nsl // nst)
            for jl in range(rows_per_slab):
                j = rows_per_slab * (m % (nsl // nst)) + jl
                bu[slab, pl.ds(j, chunk, stride=SUBLANES), :] = r[:, jl * LANES:(jl + 1) * LANES]

    a = [a_ref[i] for i in range(2 * nst)]

    def scan_steps(carry, t0, t1):
        for t in range(t0, t1):
            trow = pl.ds(t * SUBLANES, SUBLANES)
            new = []
            for s in range(nst):
                hr, hi = carry[s], carry[nst + s]
                ar, ai = a[s], a[nst + s]
                nr = ar * hr - ai * hi + bu[s, trow, :]
                ni = ar * hi + ai * hr + bu[nst + s, trow, :]
                bu[s, trow, :] = nr
                bu[nst + s, trow, :] = ni
                new.append((nr, ni))
            carry = tuple(n[0] for n in new) + tuple(n[1] for n in new)
        return carry

    def rope_head(h):
        q = proj[:, h * RET_DK:(h + 1) * RET_DK]
        k = proj[:, qk_w + h * RET_DK: qk_w + (h + 1) * RET_DK]
        qr = q * cos + pltpu.roll(q, RET_DK // 2, 1) * sin
        kr = (k * cos + pltpu.roll(k, RET_DK // 2, 1) * sin) * (RET_DK ** -0.5)
        qb = qr.astype(BF16)
        return qb, kr, _dot_nt(qb, kr.astype(BF16)), _dot(qb, s_ref[h].astype(BF16))

    carry = tuple(h_ref[i] for i in range(2 * nst))
    per_head = chunk // RET_HEADS
    pieces = []
    nxt = rope_head(0)
    for h in range(RET_HEADS):
        hs = slice(h * RET_DK, (h + 1) * RET_DK)
        qb, kr, sqk, cross = nxt
        vb = proj[:, 2 * qk_w + h * RET_DK: 2 * qk_w + (h + 1) * RET_DK].astype(BF16)
        g = proj[:, 3 * qk_w + h * RET_DK: 3 * qk_w + (h + 1) * RET_DK]
        o = _dot((sqk * decay_ref[h]).astype(BF16), vb) + cross * wq_ref[:, hs]
        if h + 1 < RET_HEADS:
            nxt = rope_head(h + 1)
        carry = scan_steps(carry, h * per_head, (h + 1) * per_head)
        s_ref[h] = gc_ref[h] * s_ref[h] + _dot_tn((kr * wkv_ref[:, hs]).astype(BF16), vb)
        mu = jnp.mean(o, axis=-1, keepdims=True)
        var = jnp.maximum(jnp.mean(o * o, axis=-1, keepdims=True) - mu * mu, 0.0)
        y = (o - mu) * lax.rsqrt(var + EPS) * rgn_ref[:, hs]
        pieces.append((g * _sigmoid(g) * y).astype(BF16))
    carry = scan_steps(carry, RET_HEADS * per_head, chunk)
    for i in range(2 * nst):
        h_ref[i] = carry[i]
    out = x + _dot(jnp.concatenate(pieces, axis=1), wout_ref[0:qk_w, :])

    ys = []
    for m in range(nsl):
        parts = []
        for ri in range(2):
            slab = ri * nst + m // (nsl // nst)
            for jl in range(rows_per_slab):
                j = rows_per_slab * (m % (nsl // nst)) + jl
                parts.append(bu[slab, pl.ds(j, chunk, stride=SUBLANES), :])
        hcat = jnp.concatenate(parts, axis=1).astype(BF16)
        um = proj[:, u_off + m * LANES: u_off + (m + 1) * LANES]
        ym = _dot(hcat, cm_ref[m]) + dsk_ref[:, m * LANES:(m + 1) * LANES] * um
        ys.append(_gelu_tanh(ym))
    yg = jnp.concatenate(ys, axis=1)
    gate = _sigmoid(_dot(yg.astype(BF16), wglu_ref[...]) + bglu_ref[...])
    xo_ref[rows, :] = out + _dot((yg * gate).astype(BF16), wout_ref[qk_w:, :])


def _rs_layer(x, st_ret, st_h, consts, w):
    b, L, d = x.shape
    chunk = min(CHUNK, L)
    tl = min(MIXER_TILE, L)
    nl = L // tl
    cos, sin, decay, wq, wkv, gc = consts
    rs_in = w['rs_w_in'].shape[1]
    nst2 = st_h.shape[1]
    kern = functools.partial(_rs_kernel, tl=tl, chunk=chunk)
    tok = pl.BlockSpec((None, tl, d), lambda i, j: (i, j, 0))
    in_specs = [
        tok, _full((1, d)), _full((d, rs_in)),
        pl.BlockSpec((tl, LANES), lambda i, j: (j, 0)), pl.BlockSpec((tl, LANES), lambda i, j: (j, 0)),
        _full(decay.shape), _full(wq.shape), _full(wkv.shape), _full(gc.shape),
        pl.BlockSpec((None,) + st_ret.shape[1:], lambda i, j: (i, 0, 0, 0)),
        pl.BlockSpec((None,) + st_h.shape[1:], lambda i, j: (i, 0, 0, 0)),
        _full((1, RET_HEADS * RET_DK)), _full(w['s5_a'].shape), _full(w['s5_b'].shape), _full(w['s5_c'].shape),
        _full((1, w['s5_d'].shape[1])), _full(w['s5_w_glu'].shape), _full((1, w['s5_b_glu'].shape[1])),
        _full(w['rs_w_out'].shape),
    ]
    out_specs = [
        tok,
        pl.BlockSpec((None,) + st_ret.shape[1:], lambda i, j: (i, 0, 0, 0)),
        pl.BlockSpec((None,) + st_h.shape[1:], lambda i, j: (i, 0, 0, 0)),
    ]
    return pl.pallas_call(
        kern, grid=(b, nl), in_specs=in_specs, out_specs=out_specs,
        out_shape=(jax.ShapeDtypeStruct(x.shape, F32), jax.ShapeDtypeStruct(st_ret.shape, F32),
                   jax.ShapeDtypeStruct(st_h.shape, F32)),
        scratch_shapes=[pltpu.VMEM((tl // chunk, nst2, chunk * SUBLANES, LANES), F32)],
        compiler_params=_params("parallel", "arbitrary"), name="rs_mixer",
    )(x, w['norm_mix0'], w['rs_w_in'], cos, sin, decay, wq, wkv, gc, st_ret, st_h, w['ret_gn'],
      w['s5_a'], w['s5_b'], w['s5_c'], w['s5_d'], w['s5_w_glu'], w['s5_b_glu'], w['rs_w_out'])


def _ret_consts(L, tl, pos0):
    half = RET_DK // 2
    pos = pos0 + jnp.arange(L, dtype=jnp.int32)
    inv = ROPE_BASE ** (-jnp.arange(half, dtype=F32) / half)
    ang = pos.astype(F32)[:, None] * inv[None, :]
    cos = jnp.cos(ang)
    sin = jnp.sin(ang)
    cos2 = jnp.concatenate([cos, cos], axis=1)
    sin2 = jnp.concatenate([-sin, sin], axis=1)
    log_g = jnp.log1p(-jnp.exp2(-5.0 - jnp.arange(RET_HEADS, dtype=F32)))
    idx = jnp.arange(tl, dtype=F32)
    diff = idx[:, None] - idx[None, :]
    decay = jnp.where(diff[None] >= 0, jnp.exp(jnp.maximum(diff, 0.0)[None] * log_g[:, None, None]), 0.0)
    w_q = jnp.exp((idx + 1.0)[:, None] * log_g[None, :])
    w_kv = jnp.exp((tl - 1.0 - idx)[:, None] * log_g[None, :])
    g_c = jnp.exp(tl * log_g)
    rep = lambda t: jnp.repeat(t, RET_DK, axis=1)
    gc = jnp.broadcast_to(g_c[:, None, None], (RET_HEADS, 1, RET_DK))
    return cos2, sin2, decay, rep(w_q), rep(w_kv), gc


def _gate_fold_kernel(wq_ref, wk_ref, wv_ref, wg_ref, o_ref):
    inner = ML_HEADS * ML_DH
    hp = lax.Precision.HIGHEST
    for h in range(ML_HEADS):
        rows = slice(h * ML_DH, (h + 1) * ML_DH)
        gq = wg_ref[h * ML_DH:(h + 1) * ML_DH, :]
        gk = wg_ref[inner + h * ML_DH: inner + (h + 1) * ML_DH, :]
        gv = wg_ref[2 * inner + h * ML_DH: 2 * inner + (h + 1) * ML_DH, :]
        o_ref[0, rows, :] = (jnp.dot(wq_ref[h], gq, precision=hp, preferred_element_type=F32)
                             + jnp.dot(wk_ref[h], gk, precision=hp, preferred_element_type=F32))
        o_ref[1, rows, :] = jnp.dot(wv_ref[h], gv, precision=hp, preferred_element_type=F32)


def _gate_fold(wq, wk, wv, wg_padded):
    inner = ML_HEADS * ML_DH
    return pl.pallas_call(
        _gate_fold_kernel, out_shape=jax.ShapeDtypeStruct((2, inner, LANES), F32), name="gate_fold",
    )(wq, wk, wv, wg_padded)


def _ml_a_kernel(x_ref, gn_ref, win_ref, cs_ref, cw_ref, cb_ref, wq_ref, wkt_ref, wv_ref, wg_ref, bg_ref, skip_ref,
                 q_ref, kt_ref, v_ref, gates_ref, sz_ref, sx_ref, nc_ref, xs_ref, *, tl, nl):
    l = pl.program_id(1)
    inner = ML_HEADS * ML_DH
    pad = SUBLANES
    hist = ML_CONV - 1

    @pl.when(l == 0)
    def _():
        xs_ref[0:pad, :] = jnp.zeros((pad, inner), F32)
        xs_ref[pl.ds(pad - hist, hist), :] = cs_ref[...]

    xn = _rms(x_ref[...], gn_ref[...]).astype(BF16)
    xm = _dot(xn, win_ref[:, 0:inner])
    sz_ref[...] = _sigmoid(_dot(xn, win_ref[:, inner:2 * inner])).astype(sz_ref.dtype)
    xs_ref[pl.ds(pad, tl), :] = xm
    acc = cw_ref[hist:hist + 1, :] * xm
    for j in range(hist):
        acc = acc + cw_ref[j:j + 1, :] * xs_ref[pl.ds(pad - hist + j, tl), :]
    xc = acc + cb_ref[...]
    xc = xc * _sigmoid(xc)
    sx_ref[...] = (skip_ref[...] * xc).astype(sx_ref.dtype)
    xcb = xc.astype(BF16)
    xmb = xm.astype(BF16)
    gates_ref[...] = _dot(xcb, wg_ref[0]) + _dot(xmb, wg_ref[1]) + bg_ref[...]
    for h in range(ML_HEADS):
        hs = slice(h * ML_DH, (h + 1) * ML_DH)
        q_ref[:, hs] = _dot(xcb[:, hs], wq_ref[h]).astype(BF16)
        kt_ref[h] = _dot_nt(wkt_ref[h], xcb[:, hs]).astype(BF16)
        v_ref[:, hs] = _dot(xmb[:, hs], wv_ref[h]).astype(BF16)

    @pl.when(l == nl - 1)
    def _():
        nc_ref[...] = xs_ref[pl.ds(pad + tl - hist, hist), :]

    xs_ref[0:pad, :] = xs_ref[pl.ds(tl, pad), :]


def _ml_a(x, conv_state, w):
    b, L, d = x.shape
    tl = min(PROJ_TILE, L)
    nl = L // tl
    inner = ML_HEADS * ML_DH
    kern = functools.partial(_ml_a_kernel, tl=tl, nl=nl)
    tok = lambda width: pl.BlockSpec((None, tl, width), lambda i, j: (i, j, 0))
    kt_spec = pl.BlockSpec((None, ML_HEADS, ML_DH, tl), lambda i, j: (i, 0, 0, j))
    cst = pl.BlockSpec((None, ML_CONV - 1, inner), lambda i, j: (i, 0, 0))
    in_specs = [tok(d), _full((1, d)), _full((d, 2 * inner)), cst, _full((ML_CONV, inner)), _full((1, inner)),
                _full(w['ml_wq'].shape), _full(w['ml_wkt'].shape), _full(w['ml_wv'].shape),
                _full(w['ml_w_gates'].shape), _full((1, LANES)), _full((1, inner))]
    out_specs = [tok(inner), kt_spec, tok(inner), tok(LANES), tok(inner), tok(inner), cst]
    sds = jax.ShapeDtypeStruct
    out_shape = (sds((b, L, inner), BF16), sds((b, ML_HEADS, ML_DH, L), BF16), sds((b, L, inner), BF16),
                 sds((b, L, LANES), F32), sds((b, L, inner), BF16), sds((b, L, inner), BF16),
                 sds((b, ML_CONV - 1, inner), F32))
    return pl.pallas_call(
        kern, grid=(b, nl), in_specs=in_specs, out_specs=out_specs, out_shape=out_shape,
        scratch_shapes=[pltpu.VMEM((tl + SUBLANES, inner), F32)],
        compiler_params=_params("parallel", "arbitrary"), name="mlstm_proj",
    )(x, w['norm_mix1'], w['ml_w_in'], conv_state, w['ml_conv_w'], w['ml_conv_b'],
      w['ml_wq'], w['ml_wkt'], w['ml_wv'], w['ml_w_gates'], w['ml_b_gates'], w['ml_skip'])


def _split3(x):
    hi = x.astype(BF16)
    r1 = x - hi.astype(F32)
    mid = r1.astype(BF16)
    lo = (r1 - mid.astype(F32)).astype(BF16)
    return hi, mid, lo


def _ml_b_kernel(*refs, tl, chunk, nl, has_state):
    if has_state:
        (x_ref, q_ref, kt_ref, v_ref, g_ref, sz_ref, sx_ref, c0_ref, n0_ref, m0_ref, gn_ref, wout_ref,
         xo_ref, c_ref, n_ref, m_ref, ncm_ref) = refs
    else:
        (x_ref, q_ref, kt_ref, v_ref, g_ref, sz_ref, sx_ref, gn_ref, wout_ref,
         xo_ref, c_ref, n_ref, m_ref, ncm_ref) = refs
    l = pl.program_id(1)

    @pl.when(l == 0)
    def _():
        if has_state:
            c_ref[...] = c0_ref[...]
            m_ref[...] = m0_ref[...]
            for h in range(ML_HEADS):
                ncm_ref[h] = jnp.broadcast_to(n0_ref[h:h + 1, :], (LANES, ML_DH)).T
        else:
            c_ref[...] = jnp.zeros(c_ref.shape, F32)
            m_ref[...] = jnp.zeros(m_ref.shape, F32)
            ncm_ref[...] = jnp.zeros(ncm_ref.shape, F32)

    for sub in range(tl // chunk):
        rows = slice(sub * chunk, (sub + 1) * chunk)
        _ml_b_chunk(x_ref.at[rows], q_ref.at[rows], kt_ref.at[:, :, rows], v_ref.at[rows], g_ref.at[rows],
                    sz_ref.at[rows], sx_ref.at[rows], gn_ref, wout_ref, xo_ref.at[rows], c_ref, m_ref, ncm_ref,
                    tl=chunk)

    @pl.when(l == nl - 1)
    def _():
        for h in range(ML_HEADS):
            n_ref[h:h + 1, :] = ncm_ref[h].T[0:1, :]


def _ml_b_chunk(x_ref, q_ref, kt_ref, v_ref, g_ref, sz_ref, sx_ref, gn_ref, wout_ref, xo_ref, c_ref, m_ref, ncm_ref,
                *, tl):
    scale = ML_DH ** -0.5
    gates = g_ref[...]
    row = lax.broadcasted_iota(jnp.int32, (tl, tl), 0)
    col = lax.broadcasted_iota(jnp.int32, (tl, tl), 1)
    causal = row >= col
    lower = jnp.where(causal, 1.0, 0.0).astype(BF16)
    ones = jnp.ones((tl, LANES), BF16)
    lf_c = _split3(_log_sigmoid(gates))
    b_f = _dot(lower, lf_c[0]) + _dot(lower, lf_c[1]) + _dot(lower, lf_c[2])
    b_c = pltpu.roll(b_f, LANES - ML_HEADS, 1)
    lane = lax.broadcasted_iota(jnp.int32, (1, LANES), 1)
    m_prev_row = jnp.zeros((1, LANES), F32)
    for h in range(ML_HEADS):
        m_prev_row = jnp.where(lane == h, m_ref[h], m_prev_row)
    run = gates - b_c
    r_t = run.T
    trow = lax.broadcasted_iota(jnp.int32, (tl, LANES), 0)
    shift = 1
    while shift < tl:
        run = jnp.maximum(run, jnp.where(trow >= shift, pltpu.roll(run, shift, 0), -jnp.inf))
        shift *= 2
    m_t_all = b_c + jnp.maximum(m_prev_row, run)
    s_inter_all = jnp.exp(b_c + m_prev_row - m_t_all)
    dmin_all = jnp.exp(-m_t_all)
    a_all = b_c - m_t_all + math.log(scale)
    m_new_row = m_t_all[tl - 1:tl, :]
    b_last_row = b_c[tl - 1:tl, :]
    dec_row = jnp.exp(b_last_row + m_prev_row - m_new_row)
    wsc_row = b_last_row - m_new_row

    hsl = [slice(h * ML_DH, (h + 1) * ML_DH) for h in range(ML_HEADS)]

    def state_dots(h):
        q = q_ref[:, hsl[h]]
        return (_dot(q, kt_ref[h]), _dot(q, c_ref[h].astype(BF16)), _dot(q, ncm_ref[h].astype(BF16))[:, 0:1])

    out = x_ref[...]
    nxt = state_dots(0)
    for h in range(ML_HEADS):
        hs = hsl[h]
        sqk, qc, qn = nxt
        v = v_ref[:, hs]
        r_row = r_t[h:h + 1, :]
        s_inter = s_inter_all[:, h:h + 1]
        dec = dec_row[:, h:h + 1]
        kwt = (kt_ref[h].astype(F32) * (scale * jnp.exp(r_row + wsc_row[:, h:h + 1]))).astype(BF16)
        c_ref[h] = dec * c_ref[h] + _dot(kwt, v)
        ncm_ref[h] = dec * ncm_ref[h] + _dot(kwt, ones)
        m_ref[h] = m_new_row[:, h:h + 1]
        wgt = jnp.exp(jnp.where(causal, a_all[:, h:h + 1] + r_row, -jnp.inf))
        qk = sqk * wgt
        num = s_inter * qc + _dot(qk.astype(BF16), v)
        den = s_inter * qn + jnp.sum(qk, axis=-1, keepdims=True)
        if h + 1 < ML_HEADS:
            nxt = state_dots(h + 1)
        dmax = jnp.maximum(jnp.abs(den), dmin_all[:, h:h + 1])
        xg = sz_ref[:, hs].astype(F32) * num
        mu = jnp.mean(xg, axis=-1, keepdims=True)
        var = jnp.maximum(jnp.mean(xg * xg, axis=-1, keepdims=True) - mu * mu, 0.0)
        y = (xg - mu) * lax.rsqrt(var + EPS * (dmax * dmax)) * gn_ref[:, hs] + sx_ref[:, hs].astype(F32)
        out = out + _dot(y.astype(BF16), wout_ref[hs, :])
    xo_ref[...] = out


def _ml_b(x, q, kt, v, gates, sz, sx, state, w):
    b, L, d = x.shape
    chunk = min(CHUNK, L)
    tl = min(CELL_TILE, L)
    nl = L // tl
    inner = ML_HEADS * ML_DH
    has_state = state is not None
    kern = functools.partial(_ml_b_kernel, tl=tl, chunk=chunk, nl=nl, has_state=has_state)
    tok = lambda width: pl.BlockSpec((None, tl, width), lambda i, j: (i, j, 0))
    kt_spec = pl.BlockSpec((None, ML_HEADS, ML_DH, tl), lambda i, j: (i, 0, 0, j))
    c_spec = pl.BlockSpec((None, ML_HEADS, ML_DH, ML_DH), lambda i, j: (i, 0, 0, 0))
    n_spec = pl.BlockSpec((None, ML_HEADS, ML_DH), lambda i, j: (i, 0, 0))
    m_spec = pl.BlockSpec((None, ML_HEADS, 1, 1), lambda i, j: (i, 0, 0, 0))
    in_specs = [tok(d), tok(inner), kt_spec, tok(inner), tok(LANES), tok(inner), tok(inner)]
    args = [x, q, kt, v, gates, sz, sx]
    if has_state:
        in_specs += [c_spec, n_spec, m_spec]
        args += [state[0], state[1], state[2].reshape(b, ML_HEADS, 1, 1)]
    in_specs += [_full((1, inner)), _full((inner, d))]
    args += [w['ml_gn'], w['ml_w_out']]
    sds = jax.ShapeDtypeStruct
    out_shape = (sds(x.shape, F32), sds((b, ML_HEADS, ML_DH, ML_DH), F32), sds((b, ML_HEADS, ML_DH), F32),
                 sds((b, ML_HEADS, 1, 1), F32))
    xo, c, n, m = pl.pallas_call(
        kern, grid=(b, nl), in_specs=in_specs, out_specs=[tok(d), c_spec, n_spec, m_spec], out_shape=out_shape,
        scratch_shapes=[pltpu.VMEM((ML_HEADS, ML_DH, LANES), F32)],
        compiler_params=_params("parallel", "arbitrary"), name="mlstm_cell",
    )(*args)
    return xo, c, n, m.reshape(b, ML_HEADS)


def _memkv_kernel(m_ref, g_ref, wk_ref, wv_ref, k_ref, v_ref, kb_ref, vb_ref):
    nb, m_len = k_ref.shape[0], k_ref.shape[1]
    mn = _rms(m_ref[...], g_ref[...]).astype(BF16)
    k = _dot(mn, wk_ref[...])
    v = _dot(mn, wv_ref[...])
    for h in range(X_HEADS):
        kh = k[:, h * X_DH:(h + 1) * X_DH].reshape(nb, m_len, X_DH)
        vh = v[:, h * X_DH:(h + 1) * X_DH].reshape(nb, m_len, X_DH)
        k_ref[:, :, h, :] = kh
        v_ref[:, :, h, :] = vh
        kb_ref[:, h] = kh.astype(BF16)
        vb_ref[:, h] = vh.astype(BF16)


def _memkv(mem, norm_mem, wk, wv):
    b, m_len, d = mem.shape
    depth = wk.shape[0]
    rows = b * m_len
    tm = min(MEM_ROW_TILE, rows)
    nb = tm // m_len
    mem2 = mem.reshape(rows, d)
    wspec = pl.BlockSpec((None, d, d), lambda i, j: (i, 0, 0))
    ospec = pl.BlockSpec((None, nb, m_len, X_HEADS, X_DH), lambda i, j: (i, j, 0, 0, 0))
    oshape = jax.ShapeDtypeStruct((depth, b, m_len, X_HEADS, X_DH), F32)
    bspec = pl.BlockSpec((None, nb, X_HEADS, m_len, X_DH), lambda i, j: (i, j, 0, 0, 0))
    bshape = jax.ShapeDtypeStruct((depth, b, X_HEADS, m_len, X_DH), BF16)
    return pl.pallas_call(
        _memkv_kernel, grid=(depth, rows // tm),
        in_specs=[pl.BlockSpec((tm, d), lambda i, j: (j, 0)), pl.BlockSpec((None, 1, d), lambda i, j: (i, 0, 0)),
                  wspec, wspec],
        out_specs=[ospec, ospec, bspec, bspec],
        out_shape=(oshape, oshape, bshape, bshape),
        compiler_params=_params("parallel", "parallel"), name="memory_kv",
    )(mem2, norm_mem.reshape(depth, 1, d), wk, wv)


def _xattn_kernel(x_ref, g_ref, wq_ref, mk_ref, mv_ref, wo_ref, o_ref):
    x = x_ref[...]
    xn = _rms(x, g_ref[...]).astype(BF16)
    q = _dot(xn, wq_ref[...])
    hsl = [slice(h * X_DH, (h + 1) * X_DH) for h in range(X_HEADS)]

    def scores(h):
        return _dot_nt(q[:, hsl[h]].astype(BF16), mk_ref[h])

    out = x
    nxt = scores(0)
    for h in range(X_HEADS):
        s = nxt * (X_DH ** -0.5)
        if h + 1 < X_HEADS:
            nxt = scores(h + 1)
        e = jnp.exp(s - jnp.max(s, axis=-1, keepdims=True))
        pv = _dot(e.astype(BF16), mv_ref[h])
        oh = pv * (1.0 / jnp.sum(e, axis=-1, keepdims=True))
        out = out + _dot(oh.astype(BF16), wo_ref[hsl[h], :])
    o_ref[...] = out


def _xattn(x, layer, g, wq, mk, mv, wo):
    b, L, d = x.shape
    tl = min(XATTN_TILE, L)
    m_len = mk.shape[3]
    tok = pl.BlockSpec((None, tl, d), lambda i, j: (i, j, 0))
    mem = pl.BlockSpec((None, None, X_HEADS, m_len, X_DH), lambda i, j: (layer, i, 0, 0, 0))
    wsp = pl.BlockSpec((None, d, d), lambda i, j: (layer, 0, 0))
    gsp = pl.BlockSpec((None, 1, d), lambda i, j: (layer, 0, 0))
    return pl.pallas_call(
        _xattn_kernel, grid=(b, L // tl),
        in_specs=[tok, gsp, wsp, mem, mem, wsp],
        out_specs=tok, out_shape=jax.ShapeDtypeStruct(x.shape, F32),
        compiler_params=_params("parallel", "arbitrary"), name="cross_attn",
    )(x, g, wq, mk, mv, wo)


def _mlp_kernel(*refs, final):
    if final:
        x_ref, g_ref, wu_ref, wd_ref, gf_ref, o_ref, xn_ref = refs
    else:
        x_ref, g_ref, wu_ref, wd_ref, o_ref, xn_ref = refs
    j = pl.program_id(1)

    @pl.when(j == 0)
    def _():
        x = x_ref[...]
        xn_ref[...] = _rms(x, g_ref[...]).astype(BF16)
        o_ref[...] = x

    hdn = jnp.maximum(_dot(xn_ref[...], wu_ref[...]), 0.0)
    o_ref[...] += _dot((hdn * hdn).astype(BF16), wd_ref[...])

    if final:
        @pl.when(j == pl.num_programs(1) - 1)
        def _():
            o_ref[...] = _rms(o_ref[...], gf_ref[...])


def _mlp(x, layer, g, wu, wd, gf=None):
    shape = x.shape
    d = shape[-1]
    x2 = x.reshape(-1, d)
    rows = x2.shape[0]
    ff = wu.shape[2]
    tm = min(MLP_ROW_TILE, rows)
    tf = min(MLP_FF_TILE, ff)
    final = gf is not None
    in_specs = [pl.BlockSpec((tm, d), lambda i, j: (i, 0)), pl.BlockSpec((None, 1, d), lambda i, j: (layer, 0, 0)),
                pl.BlockSpec((None, d, tf), lambda i, j: (layer, 0, j)),
                pl.BlockSpec((None, tf, d), lambda i, j: (layer, j, 0))]
    args = [x2, g, wu, wd]
    if final:
        in_specs.append(_full((1, d)))
        args.append(gf)
    out = pl.pallas_call(
        functools.partial(_mlp_kernel, final=final), grid=(rows // tm, ff // tf),
        in_specs=in_specs, out_specs=pl.BlockSpec((tm, d), lambda i, j: (i, 0)),
        out_shape=jax.ShapeDtypeStruct(x2.shape, F32),
        scratch_shapes=[pltpu.VMEM((tm, d), BF16)],
        compiler_params=_params("parallel", "arbitrary"), name="mlp",
    )(*args)
    return out.reshape(shape)


def _trunk(x, pos0, st_ret, st_re, st_im, ml_state, conv_state, mem_k, mem_v, w):
    b, L, d = x.shape
    chunk = min(CHUNK, L)
    nst = S5_GROUPS * S5_STATE // (SUBLANES * LANES)
    st_h = jnp.concatenate([st_re.reshape(b, nst, SUBLANES, LANES), st_im.reshape(b, nst, SUBLANES, LANES)], axis=1)
    x, s_new, h_new = _rs_layer(x, st_ret, st_h, _ret_consts(L, chunk, pos0), w)
    hr = h_new[:, :nst].reshape(b, S5_GROUPS, S5_STATE)
    hi = h_new[:, nst:].reshape(b, S5_GROUPS, S5_STATE)
    x = _xattn(x, 0, w['norm_cross'], w['x_wq'], mem_k, mem_v, w['x_wo'])
    x = _mlp(x, 0, w['norm_mlp'], w['mlp_w_up'], w['mlp_w_down'])
    q, kt, v, gates, sz, sx, new_conv = _ml_a(x, conv_state, w)
    x, cf, nf, mf = _ml_b(x, q, kt, v, gates, sz, sx, ml_state, w)
    x = _xattn(x, 1, w['norm_cross'], w['x_wq'], mem_k, mem_v, w['x_wo'])
    y = _mlp(x, 1, w['norm_mlp'], w['mlp_w_up'], w['mlp_w_down'], w['norm_final'])
    return y, s_new[None], hr[None], hi[None], cf[None], nf[None], mf[None], new_conv[None]


def kernel(x_prompt, x_sample, mem_prompt, state_ret, state_s5_re, state_s5_im, state_mlstm_c, state_mlstm_n, state_mlstm_m, cache_mlstm_conv, cache_mem_k, cache_mem_v, norm_mix, norm_cross, norm_mem, norm_mlp, norm_final, rs_w_in, rs_w_out, ret_gn, s5_a_re, s5_a_im, s5_log_dt, s5_b_re, s5_b_im, s5_c_re, s5_c_im, s5_d, s5_w_glu, s5_b_glu, ml_w_in, ml_conv_w, ml_conv_b, ml_wq, ml_wk, ml_wv, ml_w_gates, ml_b_gates, ml_gn, ml_skip, ml_w_out, x_wq, x_wk, x_wv, x_wo, mlp_w_up, mlp_w_down):
    d = x_prompt.shape[-1]
    bp = x_prompt.shape[0]
    bf = lambda t: t.astype(BF16)
    row = lambda t: t.reshape(1, -1).astype(F32)
    ab_re, ab_im, bb_re, bb_im = _s5_prep(s5_a_re[0], s5_a_im[0], s5_log_dt[0], s5_b_re[0], s5_b_im[0])
    s5_a, s5_b, s5_c = _s5_matrices(ab_re, ab_im, bb_re, bb_im, s5_c_re[0], s5_c_im[0])
    n_gate = ml_w_gates.shape[-1]
    w = {
        'norm_mix0': row(norm_mix[0]), 'norm_mix1': row(norm_mix[1]),
        'norm_cross': norm_cross.reshape(-1, 1, d), 'norm_mlp': norm_mlp.reshape(-1, 1, d),
        'norm_final': row(norm_final),
        'rs_w_in': bf(rs_w_in[0]), 'rs_w_out': bf(rs_w_out[0]), 'ret_gn': row(ret_gn[0]),
        's5_a': s5_a, 's5_b': s5_b, 's5_c': s5_c, 's5_d': row(s5_d[0]), 's5_w_glu': bf(s5_w_glu[0]),
        's5_b_glu': row(s5_b_glu[0]),
        'ml_w_in': bf(ml_w_in[0]), 'ml_conv_w': ml_conv_w[0], 'ml_conv_b': row(ml_conv_b[0]),
        'ml_wq': bf(ml_wq[0]), 'ml_wkt': bf(jnp.swapaxes(ml_wk[0], 1, 2)), 'ml_wv': bf(ml_wv[0]),
        'ml_w_gates': bf(_gate_fold(ml_wq[0], ml_wk[0], ml_wv[0],
                                    jnp.pad(ml_w_gates[0], ((0, 0), (0, LANES - n_gate))))),
        'ml_b_gates': jnp.pad(ml_b_gates[0], (0, LANES - n_gate)).reshape(1, LANES),
        'ml_gn': row(ml_gn[0]), 'ml_skip': row(ml_skip[0]), 'ml_w_out': bf(ml_w_out[0]),
        'x_wq': bf(x_wq), 'x_wo': bf(x_wo), 'mlp_w_up': bf(mlp_w_up), 'mlp_w_down': bf(mlp_w_down),
    }
    mk_p, mv_p, mkb_p, mvb_p = _memkv(mem_prompt, norm_mem, bf(x_wk), bf(x_wv))
    zeros = lambda *s: jnp.zeros(s, F32)
    out_p = _trunk(x_prompt, 0, zeros(bp, RET_HEADS, RET_DK, RET_DK), zeros(bp, S5_GROUPS, S5_STATE),
                   zeros(bp, S5_GROUPS, S5_STATE), None, zeros(bp, ML_CONV - 1, ML_HEADS * ML_DH), mkb_p, mvb_p, w)
    out_s = _trunk(x_sample, PAST_LEN, state_ret[0], state_s5_re[0], state_s5_im[0],
                   (state_mlstm_c[0], state_mlstm_n[0], state_mlstm_m[0]), cache_mlstm_conv[0],
                   bf(jnp.swapaxes(cache_mem_k, 2, 3)), bf(jnp.swapaxes(cache_mem_v, 2, 3)), w)
    return (out_p[0], out_s[0]) + tuple(out_p[1:]) + (mk_p, mv_p) + tuple(out_s[1:])
```

```python
import functools
import math

import jax
import jax.numpy as jnp
from jax import lax
from jax.experimental import pallas as pl
from jax.experimental.pallas import tpu as pltpu

F32 = jnp.float32
BF16 = jnp.bfloat16

EPS = 1e-6
ROPE_BASE = 10000.0
PAST_LEN = 4096
RET_HEADS = 4
RET_DK = 128
S5_GROUPS = 32
S5_GROUP = 16
S5_STATE = 64
ML_HEADS = 4
ML_DH = 512
ML_CONV = 4
X_HEADS = 4
X_DH = 256
LANES = 128
SUBLANES = 8
VMEM_LIMIT = 56 * 1024 * 1024
CHUNK = 256
MIXER_TILE = 256
CELL_TILE = 256
PROJ_TILE = 512
XATTN_TILE = 1024
MLP_ROW_TILE = 512
MLP_FF_TILE = 4096
MEM_ROW_TILE = 1024


def _dot(a, b):
    return jnp.dot(a, b, preferred_element_type=F32)


def _dot_nt(a, b):
    return lax.dot_general(a, b, (((1,), (1,)), ((), ())), preferred_element_type=F32)


def _dot_tn(a, b):
    return lax.dot_general(a, b, (((0,), (0,)), ((), ())), preferred_element_type=F32)


def _rms(x, g):
    y = x * lax.rsqrt(jnp.mean(x * x, axis=-1, keepdims=True) + EPS)
    return y * g


def _sigmoid(x):
    return 1.0 / (1.0 + jnp.exp(-x))


def _log_sigmoid(x):
    return -(jnp.maximum(-x, 0.0) + jnp.log(1.0 + jnp.exp(-jnp.abs(x))))


def _gelu_tanh(x):
    c = math.sqrt(2.0 / math.pi)
    return 0.5 * x * (1.0 + jnp.tanh(c * (x + 0.044715 * (x * x * x))))


def _params(*sem):
    return pltpu.CompilerParams(dimension_semantics=sem, vmem_limit_bytes=VMEM_LIMIT)


def _full(shape):
    n = len(shape)
    return pl.BlockSpec(shape, lambda *_: (0,) * n, pipeline_mode=pl.Buffered(1))


def _s5_prep_kernel(are_ref, aim_ref, ldt_ref, bre_ref, bim_ref, abre_ref, abim_ref, bbre_ref, bbim_ref):
    a_re = are_ref[...]
    a_im = aim_ref[...]
    dt = jnp.exp(ldt_ref[...])
    mag = jnp.exp(a_re * dt)
    ab_re = mag * jnp.cos(a_im * dt)
    ab_im = mag * jnp.sin(a_im * dt)
    den = a_re * a_re + a_im * a_im
    x_re = ab_re - 1.0
    f_re = (x_re * a_re + ab_im * a_im) / den
    f_im = (ab_im * a_re - x_re * a_im) / den
    b_re = bre_ref[...]
    b_im = bim_ref[...]
    abre_ref[...] = ab_re
    abim_ref[...] = ab_im
    bbre_ref[...] = f_re * b_re - f_im * b_im
    bbim_ref[...] = f_re * b_im + f_im * b_re


def _s5_prep(a_re, a_im, log_dt, b_re, b_im):
    g, p, c = b_re.shape
    out = pl.pallas_call(
        _s5_prep_kernel,
        out_shape=(jax.ShapeDtypeStruct((g, 1, p), F32), jax.ShapeDtypeStruct((g, 1, p), F32),
                   jax.ShapeDtypeStruct((g, c, p), F32), jax.ShapeDtypeStruct((g, c, p), F32)),
        name="s5_prep",
    )(a_re.reshape(g, 1, p), a_im.reshape(g, 1, p), log_dt.reshape(g, 1, 1),
      jnp.swapaxes(b_re, 1, 2), jnp.swapaxes(b_im, 1, 2))
    return out


def _s5_matrices(ab_re, ab_im, bb_re, bb_im, c_re, c_im):
    g, c, p = bb_re.shape
    nsl = g * c // LANES
    gs = LANES // c
    eye = jnp.eye(gs, dtype=F32)

    def bmat(bb):
        t = bb.reshape(nsl, gs, c, p)
        w = jnp.einsum('mgcp,hg->mhcgp', t, eye)
        return w.reshape(nsl, gs * c, gs * p)

    def cmat(cc):
        t = cc.reshape(nsl, gs, c, p)
        w = jnp.einsum('mgcp,hg->mhpgc', t, eye)
        return w.reshape(nsl, gs * p, gs * c)

    b_all = jnp.stack([bmat(bb_re), bmat(bb_im)], axis=1).reshape(nsl * 2, gs * c, gs * p)
    c_all = jnp.concatenate([cmat(c_re), -cmat(c_im)], axis=1)
    nst = g * p // (SUBLANES * LANES)
    a_all = jnp.concatenate([ab_re.reshape(nst, SUBLANES, LANES), ab_im.reshape(nst, SUBLANES, LANES)], axis=0)
    return a_all, b_all.astype(BF16), c_all.astype(BF16)


def _rs_kernel(*refs, tl, chunk):
    *chunk_refs, bu_ref = refs
    s0_ref, h0_ref = chunk_refs[9:11]
    s_ref, h_ref = chunk_refs[-2:]

    @pl.when(pl.program_id(1) == 0)
    def _():
        s_ref[...] = s0_ref[...]
        h_ref[...] = h0_ref[...]

    for sub in range(tl // chunk):
        _rs_chunk(*chunk_refs, bu_ref.at[sub], rows=slice(sub * chunk, (sub + 1) * chunk), chunk=chunk)


def _rs_chunk(x_ref, gn_ref, win_ref, cos_ref, sin_ref, decay_ref, wq_ref, wkv_ref, gc_ref,
              s0_ref, h0_ref, rgn_ref, a_ref, bm_ref, cm_ref, dsk_ref, wglu_ref, bglu_ref, wout_ref,
              xo_ref, s_ref, h_ref, bu, *, rows, chunk):
    qk_w = RET_HEADS * RET_DK
    nsl = S5_GROUPS * S5_GROUP // LANES
    nst = S5_GROUPS * S5_STATE // (SUBLANES * LANES)
    rows_per_slab = SUBLANES // (nsl // nst)

    x = x_ref[rows, :]
    xn = _rms(x, gn_ref[...]).astype(BF16)
    proj = _dot(xn, win_ref[...])
    cos = cos_ref[rows, :]
    sin = sin_ref[rows, :]

    u_off = 4 * qk_w
    for m in range(nsl):
        um = proj[:, u_off + m * LANES: u_off + (m + 1) * LANES].astype(BF16)
        for ri in range(2):
            r = _dot(um, bm_ref[2 * m + ri])
            slab = ri * nst + m // (nsl // nst)
            for jl in range(rows_per_slab):
                j = rows_per_slab * (m % (nsl // nst)) + jl
                bu[slab, pl.ds(j, chunk, stride=SUBLANES), :] = r[:, jl * LANES:(jl + 1) * LANES]

    a = [a_ref[i] for i in range(2 * nst)]

    def scan_steps(carry, t0, t1):
        for t in range(t0, t1):
            trow = pl.ds(t * SUBLANES, SUBLANES)
            new = []
            for s in range(nst):
                hr, hi = carry[s], carry[nst + s]
                ar, ai = a[s], a[nst + s]
                nr = ar * hr - ai * hi + bu[s, trow, :]
                ni = ar * hi + ai * hr + bu[nst + s, trow, :]
                bu[s, trow, :] = nr
                bu[nst + s, trow, :] = ni
                new.append((nr, ni))
            carry = tuple(n[0] for n in new) + tuple(n[1] for n in new)
        return carry

    def rope_head(h):
        q = proj[:, h * RET_DK:(h + 1) * RET_DK]
        k = proj[:, qk_w + h * RET_DK: qk_w + (h + 1) * RET_DK]
        qr = q * cos + pltpu.roll(q, RET_DK // 2, 1) * sin
        kr = (k * cos + pltpu.roll(k, RET_DK // 2, 1) * sin) * (RET_DK ** -0.5)
        qb = qr.astype(BF16)
        return qb, kr, _dot_nt(qb, kr.astype(BF16)), _dot(qb, s_ref[h].astype(BF16))

    carry = tuple(h_ref[i] for i in range(2 * nst))
    per_head = chunk // RET_HEADS
    pieces = []
    nxt = rope_head(0)
    for h in range(RET_HEADS):
        hs = slice(h * RET_DK, (h + 1) * RET_DK)
        qb, kr, sqk, cross = nxt
        vb = proj[:, 2 * qk_w + h * RET_DK: 2 * qk_w + (h + 1) * RET_DK].astype(BF16)
        g = proj[:, 3 * qk_w + h * RET_DK: 3 * qk_w + (h + 1) * RET_DK]
        o = _dot((sqk * decay_ref[h]).astype(BF16), vb) + cross * wq_ref[:, hs]
        if h + 1 < RET_HEADS:
            nxt = rope_head(h + 1)
        carry = scan_steps(carry, h * per_head, (h + 1) * per_head)
        s_ref[h] = gc_ref[h] * s_ref[h] + _dot_tn((kr * wkv_ref[:, hs]).astype(BF16), vb)
        mu = jnp.mean(o, axis=-1, keepdims=True)
        var = jnp.maximum(jnp.mean(o * o, axis=-1, keepdims=True) - mu * mu, 0.0)
        y = (o - mu) * lax.rsqrt(var + EPS) * rgn_ref[:, hs]
        pieces.append((g * _sigmoid(g) * y).astype(BF16))
    carry = scan_steps(carry, RET_HEADS * per_head, chunk)
    for i in range(2 * nst):
        h_ref[i] = carry[i]
    out = x + _dot(jnp.concatenate(pieces, axis=1), wout_ref[0:qk_w, :])

    ys = []
    for m in range(nsl):
        parts = []
        for ri in range(2):
            slab = ri * nst + m // (nsl // nst)
            for jl in range(rows_per_slab):
                j = rows_per_slab * (m % (nsl // nst)) + jl
                parts.append(bu[slab, pl.ds(j, chunk, stride=SUBLANES), :])
        hcat = jnp.concatenate(parts, axis=1).astype(BF16)
        um = proj[:, u_off + m * LANES: u_off + (m + 1) * LANES]
        ym = _dot(hcat, cm_ref[m]) + dsk_ref[:, m * LANES:(m + 1) * LANES] * um
        ys.append(_gelu_tanh(ym))
    yg = jnp.concatenate(ys, axis=1)
    gate = _sigmoid(_dot(yg.astype(BF16), wglu_ref[...]) + bglu_ref[...])
    xo_ref[rows, :] = out + _dot((yg * gate).astype(BF16), wout_ref[qk_w:, :])


def _rs_layer(x, st_ret, st_h, consts, w):
    b, L, d = x.shape
    chunk = min(CHUNK, L)
    tl = min(MIXER_TILE, L)
    nl = L // tl
    cos, sin, decay, wq, wkv, gc = consts
    rs_in = w['rs_w_in'].shape[1]
    nst2 = st_h.shape[1]
    kern = functools.partial(_rs_kernel, tl=tl, chunk=chunk)
    tok = pl.BlockSpec((None, tl, d), lambda i, j: (i, j, 0))
    in_specs = [
        tok, _full((1, d)), _full((d, rs_in)),
        pl.BlockSpec((tl, LANES), lambda i, j: (j, 0)), pl.BlockSpec((tl, LANES), lambda i, j: (j, 0)),
        _full(decay.shape), _full(wq.shape), _full(wkv.shape), _full(gc.shape),
        pl.BlockSpec((None,) + st_ret.shape[1:], lambda i, j: (i, 0, 0, 0)),
        pl.BlockSpec((None,) + st_h.shape[1:], lambda i, j: (i, 0, 0, 0)),
        _full((1, RET_HEADS * RET_DK)), _full(w['s5_a'].shape), _full(w['s5_b'].shape), _full(w['s5_c'].shape),
        _full((1, w['s5_d'].shape[1])), _full(w['s5_w_glu'].shape), _full((1, w['s5_b_glu'].shape[1])),
        _full(w['rs_w_out'].shape),
    ]
    out_specs = [
        tok,
        pl.BlockSpec((None,) + st_ret.shape[1:], lambda i, j: (i, 0, 0, 0)),
        pl.BlockSpec((None,) + st_h.shape[1:], lambda i, j: (i, 0, 0, 0)),
    ]
    return pl.pallas_call(
        kern, grid=(b, nl), in_specs=in_specs, out_specs=out_specs,
        out_shape=(jax.ShapeDtypeStruct(x.shape, F32), jax.ShapeDtypeStruct(st_ret.shape, F32),
                   jax.ShapeDtypeStruct(st_h.shape, F32)),
        scratch_shapes=[pltpu.VMEM((tl // chunk, nst2, chunk * SUBLANES, LANES), F32)],
        compiler_params=_params("parallel", "arbitrary"), name="rs_mixer",
    )(x, w['norm_mix0'], w['rs_w_in'], cos, sin, decay, wq, wkv, gc, st_ret, st_h, w['ret_gn'],
      w['s5_a'], w['s5_b'], w['s5_c'], w['s5_d'], w['s5_w_glu'], w['s5_b_glu'], w['rs_w_out'])


def _ret_consts(L, tl, pos0):
    half = RET_DK // 2
    pos = pos0 + jnp.arange(L, dtype=jnp.int32)
    inv = ROPE_BASE ** (-jnp.arange(half, dtype=F32) / half)
    ang = pos.astype(F32)[:, None] * inv[None, :]
    cos = jnp.cos(ang)
    sin = jnp.sin(ang)
    cos2 = jnp.concatenate([cos, cos], axis=1)
    sin2 = jnp.concatenate([-sin, sin], axis=1)
    log_g = jnp.log1p(-jnp.exp2(-5.0 - jnp.arange(RET_HEADS, dtype=F32)))
    idx = jnp.arange(tl, dtype=F32)
    diff = idx[:, None] - idx[None, :]
    decay = jnp.where(diff[None] >= 0, jnp.exp(jnp.maximum(diff, 0.0)[None] * log_g[:, None, None]), 0.0)
    w_q = jnp.exp((idx + 1.0)[:, None] * log_g[None, :])
    w_kv = jnp.exp((tl - 1.0 - idx)[:, None] * log_g[None, :])
    g_c = jnp.exp(tl * log_g)
    rep = lambda t: jnp.repeat(t, RET_DK, axis=1)
    gc = jnp.broadcast_to(g_c[:, None, None], (RET_HEADS, 1, RET_DK))
    return cos2, sin2, decay, rep(w_q), rep(w_kv), gc


def _gate_fold_kernel(wq_ref, wk_ref, wv_ref, wg_ref, o_ref):
    inner = ML_HEADS * ML_DH
    hp = lax.Precision.HIGHEST
    for h in range(ML_HEADS):
        rows = slice(h * ML_DH, (h + 1) * ML_DH)
        gq = wg_ref[h * ML_DH:(h + 1) * ML_DH, :]
        gk = wg_ref[inner + h * ML_DH: inner + (h + 1) * ML_DH, :]
        gv = wg_ref[2 * inner + h * ML_DH: 2 * inner + (h + 1) * ML_DH, :]
        o_ref[0, rows, :] = (jnp.dot(wq_ref[h], gq, precision=hp, preferred_element_type=F32)
                             + jnp.dot(wk_ref[h], gk, precision=hp, preferred_element_type=F32))
        o_ref[1, rows, :] = jnp.dot(wv_ref[h], gv, precision=hp, preferred_element_type=F32)


def _gate_fold(wq, wk, wv, wg_padded):
    inner = ML_HEADS * ML_DH
    return pl.pallas_call(
        _gate_fold_kernel, out_shape=jax.ShapeDtypeStruct((2, inner, LANES), F32), name="gate_fold",
    )(wq, wk, wv, wg_padded)


def _ml_a_kernel(x_ref, gn_ref, win_ref, cs_ref, cw_ref, cb_ref, wq_ref, wkt_ref, wv_ref, wg_ref, bg_ref, skip_ref,
                 q_ref, kt_ref, v_ref, gates_ref, sz_ref, sx_ref, nc_ref, xs_ref, *, tl, nl):
    l = pl.program_id(1)
    inner = ML_HEADS * ML_DH
    pad = SUBLANES
    hist = ML_CONV - 1

    @pl.when(l == 0)
    def _():
        xs_ref[0:pad, :] = jnp.zeros((pad, inner), F32)
        xs_ref[pl.ds(pad - hist, hist), :] = cs_ref[...]

    xn = _rms(x_ref[...], gn_ref[...]).astype(BF16)
    xm = _dot(xn, win_ref[:, 0:inner])
    sz_ref[...] = _sigmoid(_dot(xn, win_ref[:, inner:2 * inner])).astype(sz_ref.dtype)
    xs_ref[pl.ds(pad, tl), :] = xm
    acc = cw_ref[hist:hist + 1, :] * xm
    for j in range(hist):
        acc = acc + cw_ref[j:j + 1, :] * xs_ref[pl.ds(pad - hist + j, tl), :]
    xc = acc + cb_ref[...]
    xc = xc * _sigmoid(xc)
    sx_ref[...] = (skip_ref[...] * xc).astype(sx_ref.dtype)
    xcb = xc.astype(BF16)
    xmb = xm.astype(BF16)
    gates_ref[...] = _dot(xcb, wg_ref[0]) + _dot(xmb, wg_ref[1]) + bg_ref[...]
    for h in range(ML_HEADS):
        hs = slice(h * ML_DH, (h + 1) * ML_DH)
        q_ref[:, hs] = _dot(xcb[:, hs], wq_ref[h]).astype(BF16)
        kt_ref[h] = _dot_nt(wkt_ref[h], xcb[:, hs]).astype(BF16)
        v_ref[:, hs] = _dot(xmb[:, hs], wv_ref[h]).astype(BF16)

    @pl.when(l == nl - 1)
    def _():
        nc_ref[...] = xs_ref[pl.ds(pad + tl - hist, hist), :]

    xs_ref[0:pad, :] = xs_ref[pl.ds(tl, pad), :]


def _ml_a(x, conv_state, w):
    b, L, d = x.shape
    tl = min(PROJ_TILE, L)
    nl = L // tl
    inner = ML_HEADS * ML_DH
    kern = functools.partial(_ml_a_kernel, tl=tl, nl=nl)
    tok = lambda width: pl.BlockSpec((None, tl, width), lambda i, j: (i, j, 0))
    kt_spec = pl.BlockSpec((None, ML_HEADS, ML_DH, tl), lambda i, j: (i, 0, 0, j))
    cst = pl.BlockSpec((None, ML_CONV - 1, inner), lambda i, j: (i, 0, 0))
    in_specs = [tok(d), _full((1, d)), _full((d, 2 * inner)), cst, _full((ML_CONV, inner)), _full((1, inner)),
                _full(w['ml_wq'].shape), _full(w['ml_wkt'].shape), _full(w['ml_wv'].shape),
                _full(w['ml_w_gates'].shape), _full((1, LANES)), _full((1, inner))]
    out_specs = [tok(inner), kt_spec, tok(inner), tok(LANES), tok(inner), tok(inner), cst]
    sds = jax.ShapeDtypeStruct
    out_shape = (sds((b, L, inner), BF16), sds((b, ML_HEADS, ML_DH, L), BF16), sds((b, L, inner), BF16),
                 sds((b, L, LANES), F32), sds((b, L, inner), BF16), sds((b, L, inner), BF16),
                 sds((b, ML_CONV - 1, inner), F32))
    return pl.pallas_call(
        kern, grid=(b, nl), in_specs=in_specs, out_specs=out_specs, out_shape=out_shape,
        scratch_shapes=[pltpu.VMEM((tl + SUBLANES, inner), F32)],
        compiler_params=_params("parallel", "arbitrary"), name="mlstm_proj",
    )(x, w['norm_mix1'], w['ml_w_in'], conv_state, w['ml_conv_w'], w['ml_conv_b'],
      w['ml_wq'], w['ml_wkt'], w['ml_wv'], w['ml_w_gates'], w['ml_b_gates'], w['ml_skip'])


def _split3(x):
    hi = x.astype(BF16)
    r1 = x - hi.astype(F32)
    mid = r1.astype(BF16)
    lo = (r1 - mid.astype(F32)).astype(BF16)
    return hi, mid, lo


def _ml_b_kernel(*refs, tl, chunk, nl, has_state):
    if has_state:
        (x_ref, q_ref, kt_ref, v_ref, g_ref, sz_ref, sx_ref, c0_ref, n0_ref, m0_ref, gn_ref, wout_ref,
         xo_ref, c_ref, n_ref, m_ref, ncm_ref) = refs
    else:
        (x_ref, q_ref, kt_ref, v_ref, g_ref, sz_ref, sx_ref, gn_ref, wout_ref,
         xo_ref, c_ref, n_ref, m_ref, ncm_ref) = refs
    l = pl.program_id(1)

    @pl.when(l == 0)
    def _():
        if has_state:
            c_ref[...] = c0_ref[...]
            m_ref[...] = m0_ref[...]
            for h in range(ML_HEADS):
                ncm_ref[h] = jnp.broadcast_to(n0_ref[h:h + 1, :], (LANES, ML_DH)).T
        else:
            c_ref[...] = jnp.zeros(c_ref.shape, F32)
            m_ref[...] = jnp.zeros(m_ref.shape, F32)
            ncm_ref[...] = jnp.zeros(ncm_ref.shape, F32)

    for sub in range(tl // chunk):
        rows = slice(sub * chunk, (sub + 1) * chunk)
        _ml_b_chunk(x_ref.at[rows], q_ref.at[rows], kt_ref.at[:, :, rows], v_ref.at[rows], g_ref.at[rows],
                    sz_ref.at[rows], sx_ref.at[rows], gn_ref, wout_ref, xo_ref.at[rows], c_ref, m_ref, ncm_ref,
                    tl=chunk)

    @pl.when(l == nl - 1)
    def _():
        for h in range(ML_HEADS):
            n_ref[h:h + 1, :] = ncm_ref[h].T[0:1, :]


def _ml_b_chunk(x_ref, q_ref, kt_ref, v_ref, g_ref, sz_ref, sx_ref, gn_ref, wout_ref, xo_ref, c_ref, m_ref, ncm_ref,
                *, tl):
    scale = ML_DH ** -0.5
    gates = g_ref[...]
    row = lax.broadcasted_iota(jnp.int32, (tl, tl), 0)
    col = lax.broadcasted_iota(jnp.int32, (tl, tl), 1)
    causal = row >= col
    lower = jnp.where(causal, 1.0, 0.0).astype(BF16)
    ones = jnp.ones((tl, LANES), BF16)
    lf_c = _split3(_log_sigmoid(gates))
    b_f = _dot(lower, lf_c[0]) + _dot(lower, lf_c[1]) + _dot(lower, lf_c[2])
    b_c = pltpu.roll(b_f, LANES - ML_HEADS, 1)
    lane = lax.broadcasted_iota(jnp.int32, (1, LANES), 1)
    m_prev_row = jnp.zeros((1, LANES), F32)
    for h in range(ML_HEADS):
        m_prev_row = jnp.where(lane == h, m_ref[h], m_prev_row)
    run = gates - b_c
    r_t = run.T
    trow = lax.broadcasted_iota(jnp.int32, (tl, LANES), 0)
    shift = 1
    while shift < tl:
        run = jnp.maximum(run, jnp.where(trow >= shift, pltpu.roll(run, shift, 0), -jnp.inf))
        shift *= 2
    m_t_all = b_c + jnp.maximum(m_prev_row, run)
    s_inter_all = jnp.exp(b_c + m_prev_row - m_t_all)
    dmin_all = jnp.exp(-m_t_all)
    a_all = b_c - m_t_all + math.log(scale)
    m_new_row = m_t_all[tl - 1:tl, :]
    b_last_row = b_c[tl - 1:tl, :]
    dec_row = jnp.exp(b_last_row + m_prev_row - m_new_row)
    wsc_row = b_last_row - m_new_row

    hsl = [slice(h * ML_DH, (h + 1) * ML_DH) for h in range(ML_HEADS)]

    def state_dots(h):
        q = q_ref[:, hsl[h]]
        return (_dot(q, kt_ref[h]), _dot(q, c_ref[h].astype(BF16)), _dot(q, ncm_ref[h].astype(BF16))[:, 0:1])

    out = x_ref[...]
    nxt = state_dots(0)
    for h in range(ML_HEADS):
        hs = hsl[h]
        sqk, qc, qn = nxt
        v = v_ref[:, hs]
        r_row = r_t[h:h + 1, :]
        s_inter = s_inter_all[:, h:h + 1]
        dec = dec_row[:, h:h + 1]
        kwt = (kt_ref[h].astype(F32) * (scale * jnp.exp(r_row + wsc_row[:, h:h + 1]))).astype(BF16)
        c_ref[h] = dec * c_ref[h] + _dot(kwt, v)
        ncm_ref[h] = dec * ncm_ref[h] + _dot(kwt, ones)
        m_ref[h] = m_new_row[:, h:h + 1]
        wgt = jnp.exp(jnp.where(causal, a_all[:, h:h + 1] + r_row, -jnp.inf))
        qk = sqk * wgt
        num = s_inter * qc + _dot(qk.astype(BF16), v)
        den = s_inter * qn + jnp.sum(qk, axis=-1, keepdims=True)
        if h + 1 < ML_HEADS:
            nxt = state_dots(h + 1)
        dmax = jnp.maximum(jnp.abs(den), dmin_all[:, h:h + 1])
        xg = sz_ref[:, hs].astype(F32) * num
        mu = jnp.mean(xg, axis=-1, keepdims=True)
        var = jnp.maximum(jnp.mean(xg * xg, axis=-1, keepdims=True) - mu * mu, 0.0)
        y = (xg - mu) * lax.rsqrt(var + EPS * (dmax * dmax)) * gn_ref[:, hs] + sx_ref[:, hs].astype(F32)
        out = out + _dot(y.astype(BF16), wout_ref[hs, :])
    xo_ref[...] = out


def _ml_b(x, q, kt, v, gates, sz, sx, state, w):
    b, L, d = x.shape
    chunk = min(CHUNK, L)
    tl = min(CELL_TILE, L)
    nl = L // tl
    inner = ML_HEADS * ML_DH
    has_state = state is not None
    kern = functools.partial(_ml_b_kernel, tl=tl, chunk=chunk, nl=nl, has_state=has_state)
    tok = lambda width: pl.BlockSpec((None, tl, width), lambda i, j: (i, j, 0))
    kt_spec = pl.BlockSpec((None, ML_HEADS, ML_DH, tl), lambda i, j: (i, 0, 0, j))
    c_spec = pl.BlockSpec((None, ML_HEADS, ML_DH, ML_DH), lambda i, j: (i, 0, 0, 0))
    n_spec = pl.BlockSpec((None, ML_HEADS, ML_DH), lambda i, j: (i, 0, 0))
    m_spec = pl.BlockSpec((None, ML_HEADS, 1, 1), lambda i, j: (i, 0, 0, 0))
    in_specs = [tok(d), tok(inner), kt_spec, tok(inner), tok(LANES), tok(inner), tok(inner)]
    args = [x, q, kt, v, gates, sz, sx]
    if has_state:
        in_specs += [c_spec, n_spec, m_spec]
        args += [state[0], state[1], state[2].reshape(b, ML_HEADS, 1, 1)]
    in_specs += [_full((1, inner)), _full((inner, d))]
    args += [w['ml_gn'], w['ml_w_out']]
    sds = jax.ShapeDtypeStruct
    out_shape = (sds(x.shape, F32), sds((b, ML_HEADS, ML_DH, ML_DH), F32), sds((b, ML_HEADS, ML_DH), F32),
                 sds((b, ML_HEADS, 1, 1), F32))
    xo, c, n, m = pl.pallas_call(
        kern, grid=(b, nl), in_specs=in_specs, out_specs=[tok(d), c_spec, n_spec, m_spec], out_shape=out_shape,
        scratch_shapes=[pltpu.VMEM((ML_HEADS, ML_DH, LANES), F32)],
        compiler_params=_params("parallel", "arbitrary"), name="mlstm_cell",
    )(*args)
    return xo, c, n, m.reshape(b, ML_HEADS)


def _memkv_kernel(m_ref, g_ref, wk_ref, wv_ref, k_ref, v_ref, kb_ref, vb_ref):
    nb, m_len = k_ref.shape[0], k_ref.shape[1]
    mn = _rms(m_ref[...], g_ref[...]).astype(BF16)
    k = _dot(mn, wk_ref[...])
    v = _dot(mn, wv_ref[...])
    for h in range(X_HEADS):
        kh = k[:, h * X_DH:(h + 1) * X_DH].reshape(nb, m_len, X_DH)
        vh = v[:, h * X_DH:(h + 1) * X_DH].reshape(nb, m_len, X_DH)
        k_ref[:, :, h, :] = kh
        v_ref[:, :, h, :] = vh
        kb_ref[:, h] = kh.astype(BF16)
        vb_ref[:, h] = vh.astype(BF16)


def _memkv(mem, norm_mem, wk, wv):
    b, m_len, d = mem.shape
    depth = wk.shape[0]
    rows = b * m_len
    tm = min(MEM_ROW_TILE, rows)
    nb = tm // m_len
    mem2 = mem.reshape(rows, d)
    wspec = pl.BlockSpec((None, d, d), lambda i, j: (i, 0, 0))
    ospec = pl.BlockSpec((None, nb, m_len, X_HEADS, X_DH), lambda i, j: (i, j, 0, 0, 0))
    oshape = jax.ShapeDtypeStruct((depth, b, m_len, X_HEADS, X_DH), F32)
    bspec = pl.BlockSpec((None, nb, X_HEADS, m_len, X_DH), lambda i, j: (i, j, 0, 0, 0))
    bshape = jax.ShapeDtypeStruct((depth, b, X_HEADS, m_len, X_DH), BF16)
    return pl.pallas_call(
        _memkv_kernel, grid=(depth, rows // tm),
        in_specs=[pl.BlockSpec((tm, d), lambda i, j: (j, 0)), pl.BlockSpec((None, 1, d), lambda i, j: (i, 0, 0)),
                  wspec, wspec],
        out_specs=[ospec, ospec, bspec, bspec],
        out_shape=(oshape, oshape, bshape, bshape),
        compiler_params=_params("parallel", "parallel"), name="memory_kv",
    )(mem2, norm_mem.reshape(depth, 1, d), wk, wv)


def _xattn_kernel(x_ref, g_ref, wq_ref, mk_ref, mv_ref, wo_ref, o_ref):
    x = x_ref[...]
    xn = _rms(x, g_ref[...]).astype(BF16)
    q = _dot(xn, wq_ref[...])
    hsl = [slice(h * X_DH, (h + 1) * X_DH) for h in range(X_HEADS)]

    def scores(h):
        return _dot_nt(q[:, hsl[h]].astype(BF16), mk_ref[h])

    out = x
    nxt = scores(0)
    for h in range(X_HEADS):
        s = nxt * (X_DH ** -0.5)
        if h + 1 < X_HEADS:
            nxt = scores(h + 1)
        e = jnp.exp(s - jnp.max(s, axis=-1, keepdims=True))
        pv = _dot(e.astype(BF16), mv_ref[h])
        oh = pv * (1.0 / jnp.sum(e, axis=-1, keepdims=True))
        out = out + _dot(oh.astype(BF16), wo_ref[hsl[h], :])
    o_ref[...] = out


def _xattn(x, layer, g, wq, mk, mv, wo):
    b, L, d = x.shape
    tl = min(XATTN_TILE, L)
    m_len = mk.shape[3]
    tok = pl.BlockSpec((None, tl, d), lambda i, j: (i, j, 0))
    mem = pl.BlockSpec((None, None, X_HEADS, m_len, X_DH), lambda i, j: (layer, i, 0, 0, 0))
    wsp = pl.BlockSpec((None, d, d), lambda i, j: (layer, 0, 0))
    gsp = pl.BlockSpec((None, 1, d), lambda i, j: (layer, 0, 0))
    return pl.pallas_call(
        _xattn_kernel, grid=(b, L // tl),
        in_specs=[tok, gsp, wsp, mem, mem, wsp],
        out_specs=tok, out_shape=jax.ShapeDtypeStruct(x.shape, F32),
        compiler_params=_params("parallel", "arbitrary"), name="cross_attn",
    )(x, g, wq, mk, mv, wo)


def _mlp_kernel(*refs, final):
    if final:
        x_ref, g_ref, wu_ref, wd_ref, gf_ref, o_ref, xn_ref = refs
    else:
        x_ref, g_ref, wu_ref, wd_ref, o_ref, xn_ref = refs
    j = pl.program_id(1)

    @pl.when(j == 0)
    def _():
        x = x_ref[...]
        xn_ref[...] = _rms(x, g_ref[...]).astype(BF16)
        o_ref[...] = x

    hdn = jnp.maximum(_dot(xn_ref[...], wu_ref[...]), 0.0)
    o_ref[...] += _dot((hdn * hdn).astype(BF16), wd_ref[...])

    if final:
        @pl.when(j == pl.num_programs(1) - 1)
        def _():
            o_ref[...] = _rms(o_ref[...], gf_ref[...])


def _mlp(x, layer, g, wu, wd, gf=None):
    shape = x.shape
    d = shape[-1]
    x2 = x.reshape(-1, d)
    rows = x2.shape[0]
    ff = wu.shape[2]
    tm = min(MLP_ROW_TILE, rows)
    tf = min(MLP_FF_TILE, ff)
    final = gf is not None
    wmode = pl.Buffered(1) if tf == ff else None
    in_specs = [pl.BlockSpec((tm, d), lambda i, j: (i, 0)), pl.BlockSpec((None, 1, d), lambda i, j: (layer, 0, 0)),
                pl.BlockSpec((None, d, tf), lambda i, j: (layer, 0, j), pipeline_mode=wmode),
                pl.BlockSpec((None, tf, d), lambda i, j: (layer, j, 0), pipeline_mode=wmode)]
    args = [x2, g, wu, wd]
    if final:
        in_specs.append(_full((1, d)))
        args.append(gf)
    out = pl.pallas_call(
        functools.partial(_mlp_kernel, final=final), grid=(rows // tm, ff // tf),
        in_specs=in_specs, out_specs=pl.BlockSpec((tm, d), lambda i, j: (i, 0)),
        out_shape=jax.ShapeDtypeStruct(x2.shape, F32),
        scratch_shapes=[pltpu.VMEM((tm, d), BF16)],
        compiler_params=_params("parallel", "arbitrary"), name="mlp",
    )(*args)
    return out.reshape(shape)


def _trunk(x, pos0, st_ret, st_re, st_im, ml_state, conv_state, mem_k, mem_v, w):
    b, L, d = x.shape
    chunk = min(CHUNK, L)
    nst = S5_GROUPS * S5_STATE // (SUBLANES * LANES)
    st_h = jnp.concatenate([st_re.reshape(b, nst, SUBLANES, LANES), st_im.reshape(b, nst, SUBLANES, LANES)], axis=1)
    x, s_new, h_new = _rs_layer(x, st_ret, st_h, _ret_consts(L, chunk, pos0), w)
    hr = h_new[:, :nst].reshape(b, S5_GROUPS, S5_STATE)
    hi = h_new[:, nst:].reshape(b, S5_GROUPS, S5_STATE)
    x = _xattn(x, 0, w['norm_cross'], w['x_wq'], mem_k, mem_v, w['x_wo'])
    x = _mlp(x, 0, w['norm_mlp'], w['mlp_w_up'], w['mlp_w_down'])
    q, kt, v, gates, sz, sx, new_conv = _ml_a(x, conv_state, w)
    x, cf, nf, mf = _ml_b(x, q, kt, v, gates, sz, sx, ml_state, w)
    x = _xattn(x, 1, w['norm_cross'], w['x_wq'], mem_k, mem_v, w['x_wo'])
    y = _mlp(x, 1, w['norm_mlp'], w['mlp_w_up'], w['mlp_w_down'], w['norm_final'])
    return y, s_new[None], hr[None], hi[None], cf[None], nf[None], mf[None], new_conv[None]


def kernel(x_prompt, x_sample, mem_prompt, state_ret, state_s5_re, state_s5_im, state_mlstm_c, state_mlstm_n, state_mlstm_m, cache_mlstm_conv, cache_mem_k, cache_mem_v, norm_mix, norm_cross, norm_mem, norm_mlp, norm_final, rs_w_in, rs_w_out, ret_gn, s5_a_re, s5_a_im, s5_log_dt, s5_b_re, s5_b_im, s5_c_re, s5_c_im, s5_d, s5_w_glu, s5_b_glu, ml_w_in, ml_conv_w, ml_conv_b, ml_wq, ml_wk, ml_wv, ml_w_gates, ml_b_gates, ml_gn, ml_skip, ml_w_out, x_wq, x_wk, x_wv, x_wo, mlp_w_up, mlp_w_down):
    d = x_prompt.shape[-1]
    bp = x_prompt.shape[0]
    bf = lambda t: t.astype(BF16)
    row = lambda t: t.reshape(1, -1).astype(F32)
    ab_re, ab_im, bb_re, bb_im = _s5_prep(s5_a_re[0], s5_a_im[0], s5_log_dt[0], s5_b_re[0], s5_b_im[0])
    s5_a, s5_b, s5_c = _s5_matrices(ab_re, ab_im, bb_re, bb_im, s5_c_re[0], s5_c_im[0])
    n_gate = ml_w_gates.shape[-1]
    w = {
        'norm_mix0': row(norm_mix[0]), 'norm_mix1': row(norm_mix[1]),
        'norm_cross': norm_cross.reshape(-1, 1, d), 'norm_mlp': norm_mlp.reshape(-1, 1, d),
        'norm_final': row(norm_final),
        'rs_w_in': bf(rs_w_in[0]), 'rs_w_out': bf(rs_w_out[0]), 'ret_gn': row(ret_gn[0]),
        's5_a': s5_a, 's5_b': s5_b, 's5_c': s5_c, 's5_d': row(s5_d[0]), 's5_w_glu': bf(s5_w_glu[0]),
        's5_b_glu': row(s5_b_glu[0]),
        'ml_w_in': bf(ml_w_in[0]), 'ml_conv_w': ml_conv_w[0], 'ml_conv_b': row(ml_conv_b[0]),
        'ml_wq': bf(ml_wq[0]), 'ml_wkt': bf(jnp.swapaxes(ml_wk[0], 1, 2)), 'ml_wv': bf(ml_wv[0]),
        'ml_w_gates': bf(_gate_fold(ml_wq[0], ml_wk[0], ml_wv[0],
                                    jnp.pad(ml_w_gates[0], ((0, 0), (0, LANES - n_gate))))),
        'ml_b_gates': jnp.pad(ml_b_gates[0], (0, LANES - n_gate)).reshape(1, LANES),
        'ml_gn': row(ml_gn[0]), 'ml_skip': row(ml_skip[0]), 'ml_w_out': bf(ml_w_out[0]),
        'x_wq': bf(x_wq), 'x_wo': bf(x_wo), 'mlp_w_up': bf(mlp_w_up), 'mlp_w_down': bf(mlp_w_down),
    }
    mk_p, mv_p, mkb_p, mvb_p = _memkv(mem_prompt, norm_mem, bf(x_wk), bf(x_wv))
    zeros = lambda *s: jnp.zeros(s, F32)
    out_p = _trunk(x_prompt, 0, zeros(bp, RET_HEADS, RET_DK, RET_DK), zeros(bp, S5_GROUPS, S5_STATE),
                   zeros(bp, S5_GROUPS, S5_STATE), None, zeros(bp, ML_CONV - 1, ML_HEADS * ML_DH), mkb_p, mvb_p, w)
    out_s = _trunk(x_sample, PAST_LEN, state_ret[0], state_s5_re[0], state_s5_im[0],
                   (state_mlstm_c[0], state_mlstm_n[0], state_mlstm_m[0]), cache_mlstm_conv[0],
                   bf(jnp.swapaxes(cache_mem_k, 2, 3)), bf(jnp.swapaxes(cache_mem_v, 2, 3)), w)
    return (out_p[0], out_s[0]) + tuple(out_p[1:]) + (mk_p, mv_p) + tuple(out_s[1:])
```

```python
import functools
import math

import jax
import jax.numpy as jnp
from jax import lax
from jax.experimental import pallas as pl
from jax.experimental.pallas import tpu as pltpu

F32 = jnp.float32
BF16 = jnp.bfloat16

EPS = 1e-6
ROPE_BASE = 10000.0
PAST_LEN = 4096
RET_HEADS = 4
RET_DK = 128
S5_GROUPS = 32
S5_GROUP = 16
S5_STATE = 64
ML_HEADS = 4
ML_DH = 512
ML_CONV = 4
X_HEADS = 4
X_DH = 256
LANES = 128
SUBLANES = 8
VMEM_LIMIT = 56 * 1024 * 1024
CHUNK = 256
MIXER_TILE = 256
CELL_TILE = 256
PROJ_TILE = 512
XATTN_TILE = 1024
MLP_ROW_TILE = 512
MLP_FF_TILE = 4096
MEM_ROW_TILE = 1024


def _dot(a, b):
    return jnp.dot(a, b, preferred_element_type=F32)


def _dot_nt(a, b):
    return lax.dot_general(a, b, (((1,), (1,)), ((), ())), preferred_element_type=F32)


def _dot_tn(a, b):
    return lax.dot_general(a, b, (((0,), (0,)), ((), ())), preferred_element_type=F32)


def _rms(x, g):
    y = x * lax.rsqrt(jnp.mean(x * x, axis=-1, keepdims=True) + EPS)
    return y * g


def _sigmoid(x):
    return 1.0 / (1.0 + jnp.exp(-x))


def _log_sigmoid(x):
    return -(jnp.maximum(-x, 0.0) + jnp.log(1.0 + jnp.exp(-jnp.abs(x))))


def _gelu_tanh(x):
    c = math.sqrt(2.0 / math.pi)
    return 0.5 * x * (1.0 + jnp.tanh(c * (x + 0.044715 * (x * x * x))))


def _params(*sem):
    return pltpu.CompilerParams(dimension_semantics=sem, vmem_limit_bytes=VMEM_LIMIT)


def _full(shape):
    n = len(shape)
    return pl.BlockSpec(shape, lambda *_: (0,) * n, pipeline_mode=pl.Buffered(1))


def _s5_prep_kernel(are_ref, aim_ref, ldt_ref, bre_ref, bim_ref, abre_ref, abim_ref, bbre_ref, bbim_ref):
    a_re = are_ref[...]
    a_im = aim_ref[...]
    dt = jnp.exp(ldt_ref[...])
    mag = jnp.exp(a_re * dt)
    ab_re = mag * jnp.cos(a_im * dt)
    ab_im = mag * jnp.sin(a_im * dt)
    den = a_re * a_re + a_im * a_im
    x_re = ab_re - 1.0
    f_re = (x_re * a_re + ab_im * a_im) / den
    f_im = (ab_im * a_re - x_re * a_im) / den
    b_re = bre_ref[...]
    b_im = bim_ref[...]
    abre_ref[...] = ab_re
    abim_ref[...] = ab_im
    bbre_ref[...] = f_re * b_re - f_im * b_im
    bbim_ref[...] = f_re * b_im + f_im * b_re


def _s5_prep(a_re, a_im, log_dt, b_re, b_im):
    g, p, c = b_re.shape
    out = pl.pallas_call(
        _s5_prep_kernel,
        out_shape=(jax.ShapeDtypeStruct((g, 1, p), F32), jax.ShapeDtypeStruct((g, 1, p), F32),
                   jax.ShapeDtypeStruct((g, c, p), F32), jax.ShapeDtypeStruct((g, c, p), F32)),
        name="s5_prep",
    )(a_re.reshape(g, 1, p), a_im.reshape(g, 1, p), log_dt.reshape(g, 1, 1),
      jnp.swapaxes(b_re, 1, 2), jnp.swapaxes(b_im, 1, 2))
    return out


def _s5_matrices(ab_re, ab_im, bb_re, bb_im, c_re, c_im):
    g, c, p = bb_re.shape
    nsl = g * c // LANES
    gs = LANES // c
    eye = jnp.eye(gs, dtype=F32)

    def bmat(bb):
        t = bb.reshape(nsl, gs, c, p)
        w = jnp.einsum('mgcp,hg->mhcgp', t, eye)
        return w.reshape(nsl, gs * c, gs * p)

    def cmat(cc):
        t = cc.reshape(nsl, gs, c, p)
        w = jnp.einsum('mgcp,hg->mhpgc', t, eye)
        return w.reshape(nsl, gs * p, gs * c)

    b_all = jnp.stack([bmat(bb_re), bmat(bb_im)], axis=1).reshape(nsl * 2, gs * c, gs * p)
    c_all = jnp.concatenate([cmat(c_re), -cmat(c_im)], axis=1)
    nst = g * p // (SUBLANES * LANES)
    a_all = jnp.concatenate([ab_re.reshape(nst, SUBLANES, LANES), ab_im.reshape(nst, SUBLANES, LANES)], axis=0)
    return a_all, b_all.astype(BF16), c_all.astype(BF16)


def _rs_kernel(*refs, tl, chunk):
    *chunk_refs, bu_ref = refs
    s0_ref, h0_ref = chunk_refs[9:11]
    s_ref, h_ref = chunk_refs[-2:]

    @pl.when(pl.program_id(1) == 0)
    def _():
        s_ref[...] = s0_ref[...]
        h_ref[...] = h0_ref[...]

    for sub in range(tl // chunk):
        _rs_chunk(*chunk_refs, bu_ref.at[sub], rows=slice(sub * chunk, (sub + 1) * chunk), chunk=chunk)


def _rs_chunk(x_ref, gn_ref, win_ref, cos_ref, sin_ref, decay_ref, wq_ref, wkv_ref, gc_ref,
              s0_ref, h0_ref, rgn_ref, a_ref, bm_ref, cm_ref, dsk_ref, wglu_ref, bglu_ref, wout_ref,
              xo_ref, s_ref, h_ref, bu, *, rows, chunk):
    qk_w = RET_HEADS * RET_DK
    nsl = S5_GROUPS * S5_GROUP // LANES
    nst = S5_GROUPS * S5_STATE // (SUBLANES * LANES)
    rows_per_slab = SUBLANES // (nsl // nst)

    x = x_ref[rows, :]
    xn = _rms(x, gn_ref[...]).astype(BF16)
    proj = _dot(xn, win_ref[...])
    cos = cos_ref[rows, :]
    sin = sin_ref[rows, :]

    u_off = 4 * qk_w
    for m in range(nsl):
        um = proj[:, u_off + m * LANES: u_off + (m + 1) * LANES].astype(BF16)
        for ri in range(2):
            r = _dot(um, bm_ref[2 * m + ri])
            slab = ri * nst + m // (nsl // nst)
            for jl in range(rows_per_slab):
                j = rows_per_slab * (m % (nsl // nst)) + jl
                bu[slab, pl.ds(j, chunk, stride=SUBLANES), :] = r[:, jl * LANES:(jl + 1) * LANES]

    a = [a_ref[i] for i in range(2 * nst)]

    def scan_steps(carry, t0, t1):
        for t in range(t0, t1):
            trow = pl.ds(t * SUBLANES, SUBLANES)
            new = []
            for s in range(nst):
                hr, hi = carry[s], carry[nst + s]
                ar, ai = a[s], a[nst + s]
                nr = ar * hr - ai * hi + bu[s, trow, :]
                ni = ar * hi + ai * hr + bu[nst + s, trow, :]
                bu[s, trow, :] = nr
                bu[nst + s, trow, :] = ni
                new.append((nr, ni))
            carry = tuple(n[0] for n in new) + tuple(n[1] for n in new)
        return carry

    def rope_head(h):
        q = proj[:, h * RET_DK:(h + 1) * RET_DK]
        k = proj[:, qk_w + h * RET_DK: qk_w + (h + 1) * RET_DK]
        qr = q * cos + pltpu.roll(q, RET_DK // 2, 1) * sin
        kr = (k * cos + pltpu.roll(k, RET_DK // 2, 1) * sin) * (RET_DK ** -0.5)
        qb = qr.astype(BF16)
        return qb, kr, _dot_nt(qb, kr.astype(BF16)), _dot(qb, s_ref[h].astype(BF16))

    carry = tuple(h_ref[i] for i in range(2 * nst))
    per_head = chunk // RET_HEADS
    pieces = []
    nxt = rope_head(0)
    for h in range(RET_HEADS):
        hs = slice(h * RET_DK, (h + 1) * RET_DK)
        qb, kr, sqk, cross = nxt
        vb = proj[:, 2 * qk_w + h * RET_DK: 2 * qk_w + (h + 1) * RET_DK].astype(BF16)
        g = proj[:, 3 * qk_w + h * RET_DK: 3 * qk_w + (h + 1) * RET_DK]
        o = _dot((sqk * decay_ref[h]).astype(BF16), vb) + cross * wq_ref[:, hs]
        if h + 1 < RET_HEADS:
            nxt = rope_head(h + 1)
        carry = scan_steps(carry, h * per_head, (h + 1) * per_head)
        s_ref[h] = gc_ref[h] * s_ref[h] + _dot_tn((kr * wkv_ref[:, hs]).astype(BF16), vb)
        mu = jnp.mean(o, axis=-1, keepdims=True)
        var = jnp.maximum(jnp.mean(o * o, axis=-1, keepdims=True) - mu * mu, 0.0)
        y = (o - mu) * lax.rsqrt(var + EPS) * rgn_ref[:, hs]
        pieces.append((g * _sigmoid(g) * y).astype(BF16))
    carry = scan_steps(carry, RET_HEADS * per_head, chunk)
    for i in range(2 * nst):
        h_ref[i] = carry[i]
    out = x + _dot(jnp.concatenate(pieces, axis=1), wout_ref[0:qk_w, :])

    ys = []
    for m in range(nsl):
        parts = []
        for ri in range(2):
            slab = ri * nst + m // (nsl // nst)
            for jl in range(rows_per_slab):
                j = rows_per_slab * (m % (nsl // nst)) + jl
                parts.append(bu[slab, pl.ds(j, chunk, stride=SUBLANES), :])
        hcat = jnp.concatenate(parts, axis=1).astype(BF16)
        um = proj[:, u_off + m * LANES: u_off + (m + 1) * LANES]
        ym = _dot(hcat, cm_ref[m]) + dsk_ref[:, m * LANES:(m + 1) * LANES] * um
        ys.append(_gelu_tanh(ym))
    yg = jnp.concatenate(ys, axis=1)
    gate = _sigmoid(_dot(yg.astype(BF16), wglu_ref[...]) + bglu_ref[...])
    xo_ref[rows, :] = out + _dot((yg * gate).astype(BF16), wout_ref[qk_w:, :])


def _rs_layer(x, st_ret, st_h, consts, w):
    b, L, d = x.shape
    chunk = min(CHUNK, L)
    tl = min(MIXER_TILE, L)
    assert L % tl == 0 and tl % chunk == 0 and chunk % SUBLANES == 0, (L, tl, chunk)
    nl = L // tl
    cos, sin, decay, wq, wkv, gc = consts
    rs_in = w['rs_w_in'].shape[1]
    nst2 = st_h.shape[1]
    kern = functools.partial(_rs_kernel, tl=tl, chunk=chunk)
    tok = pl.BlockSpec((None, tl, d), lambda i, j: (i, j, 0))
    in_specs = [
        tok, _full((1, d)), _full((d, rs_in)),
        pl.BlockSpec((tl, LANES), lambda i, j: (j, 0)), pl.BlockSpec((tl, LANES), lambda i, j: (j, 0)),
        _full(decay.shape), _full(wq.shape), _full(wkv.shape), _full(gc.shape),
        pl.BlockSpec((None,) + st_ret.shape[1:], lambda i, j: (i, 0, 0, 0)),
        pl.BlockSpec((None,) + st_h.shape[1:], lambda i, j: (i, 0, 0, 0)),
        _full((1, RET_HEADS * RET_DK)), _full(w['s5_a'].shape), _full(w['s5_b'].shape), _full(w['s5_c'].shape),
        _full((1, w['s5_d'].shape[1])), _full(w['s5_w_glu'].shape), _full((1, w['s5_b_glu'].shape[1])),
        _full(w['rs_w_out'].shape),
    ]
    out_specs = [
        tok,
        pl.BlockSpec((None,) + st_ret.shape[1:], lambda i, j: (i, 0, 0, 0)),
        pl.BlockSpec((None,) + st_h.shape[1:], lambda i, j: (i, 0, 0, 0)),
    ]
    return pl.pallas_call(
        kern, grid=(b, nl), in_specs=in_specs, out_specs=out_specs,
        out_shape=(jax.ShapeDtypeStruct(x.shape, F32), jax.ShapeDtypeStruct(st_ret.shape, F32),
                   jax.ShapeDtypeStruct(st_h.shape, F32)),
        scratch_shapes=[pltpu.VMEM((tl // chunk, nst2, chunk * SUBLANES, LANES), F32)],
        compiler_params=_params("parallel", "arbitrary"), name="rs_mixer",
    )(x, w['norm_mix0'], w['rs_w_in'], cos, sin, decay, wq, wkv, gc, st_ret, st_h, w['ret_gn'],
      w['s5_a'], w['s5_b'], w['s5_c'], w['s5_d'], w['s5_w_glu'], w['s5_b_glu'], w['rs_w_out'])


def _ret_consts(L, tl, pos0):
    half = RET_DK // 2
    pos = pos0 + jnp.arange(L, dtype=jnp.int32)
    inv = ROPE_BASE ** (-jnp.arange(half, dtype=F32) / half)
    ang = pos.astype(F32)[:, None] * inv[None, :]
    cos = jnp.cos(ang)
    sin = jnp.sin(ang)
    cos2 = jnp.concatenate([cos, cos], axis=1)
    sin2 = jnp.concatenate([-sin, sin], axis=1)
    log_g = jnp.log1p(-jnp.exp2(-5.0 - jnp.arange(RET_HEADS, dtype=F32)))
    idx = jnp.arange(tl, dtype=F32)
    diff = idx[:, None] - idx[None, :]
    decay = jnp.where(diff[None] >= 0, jnp.exp(jnp.maximum(diff, 0.0)[None] * log_g[:, None, None]), 0.0)
    w_q = jnp.exp((idx + 1.0)[:, None] * log_g[None, :])
    w_kv = jnp.exp((tl - 1.0 - idx)[:, None] * log_g[None, :])
    g_c = jnp.exp(tl * log_g)
    rep = lambda t: jnp.repeat(t, RET_DK, axis=1)
    gc = jnp.broadcast_to(g_c[:, None, None], (RET_HEADS, 1, RET_DK))
    return cos2, sin2, decay, rep(w_q), rep(w_kv), gc


def _gate_fold_kernel(wq_ref, wk_ref, wv_ref, wg_ref, o_ref):
    inner = ML_HEADS * ML_DH
    hp = lax.Precision.HIGHEST
    for h in range(ML_HEADS):
        rows = slice(h * ML_DH, (h + 1) * ML_DH)
        gq = wg_ref[h * ML_DH:(h + 1) * ML_DH, :]
        gk = wg_ref[inner + h * ML_DH: inner + (h + 1) * ML_DH, :]
        gv = wg_ref[2 * inner + h * ML_DH: 2 * inner + (h + 1) * ML_DH, :]
        o_ref[0, rows, :] = (jnp.dot(wq_ref[h], gq, precision=hp, preferred_element_type=F32)
                             + jnp.dot(wk_ref[h], gk, precision=hp, preferred_element_type=F32))
        o_ref[1, rows, :] = jnp.dot(wv_ref[h], gv, precision=hp, preferred_element_type=F32)


def _gate_fold(wq, wk, wv, wg_padded):
    inner = ML_HEADS * ML_DH
    return pl.pallas_call(
        _gate_fold_kernel, out_shape=jax.ShapeDtypeStruct((2, inner, LANES), F32), name="gate_fold",
    )(wq, wk, wv, wg_padded)


def _ml_a_kernel(x_ref, gn_ref, win_ref, cs_ref, cw_ref, cb_ref, wq_ref, wkt_ref, wv_ref, wg_ref, bg_ref, skip_ref,
                 q_ref, kt_ref, v_ref, gates_ref, sz_ref, sx_ref, nc_ref, xs_ref, *, tl, nl):
    l = pl.program_id(1)
    inner = ML_HEADS * ML_DH
    pad = SUBLANES
    hist = ML_CONV - 1

    @pl.when(l == 0)
    def _():
        xs_ref[0:pad, :] = jnp.zeros((pad, inner), F32)
        xs_ref[pl.ds(pad - hist, hist), :] = cs_ref[...]

    xn = _rms(x_ref[...], gn_ref[...]).astype(BF16)
    xm = _dot(xn, win_ref[:, 0:inner])
    sz_ref[...] = _sigmoid(_dot(xn, win_ref[:, inner:2 * inner])).astype(sz_ref.dtype)
    xs_ref[pl.ds(pad, tl), :] = xm
    acc = cw_ref[hist:hist + 1, :] * xm
    for j in range(hist):
        acc = acc + cw_ref[j:j + 1, :] * xs_ref[pl.ds(pad - hist + j, tl), :]
    xc = acc + cb_ref[...]
    xc = xc * _sigmoid(xc)
    sx_ref[...] = (skip_ref[...] * xc).astype(sx_ref.dtype)
    xcb = xc.astype(BF16)
    xmb = xm.astype(BF16)
    gates_ref[...] = _dot(xcb, wg_ref[0]) + _dot(xmb, wg_ref[1]) + bg_ref[...]
    for h in range(ML_HEADS):
        hs = slice(h * ML_DH, (h + 1) * ML_DH)
        q_ref[:, hs] = _dot(xcb[:, hs], wq_ref[h]).astype(BF16)
        kt_ref[h] = _dot_nt(wkt_ref[h], xcb[:, hs]).astype(BF16)
        v_ref[:, hs] = _dot(xmb[:, hs], wv_ref[h]).astype(BF16)

    @pl.when(l == nl - 1)
    def _():
        nc_ref[...] = xs_ref[pl.ds(pad + tl - hist, hist), :]

    xs_ref[0:pad, :] = xs_ref[pl.ds(tl, pad), :]


def _ml_a(x, conv_state, w):
    b, L, d = x.shape
    tl = min(PROJ_TILE, L)
    assert L % tl == 0 and tl % SUBLANES == 0 and tl >= ML_CONV - 1, (L, tl)
    nl = L // tl
    inner = ML_HEADS * ML_DH
    kern = functools.partial(_ml_a_kernel, tl=tl, nl=nl)
    tok = lambda width: pl.BlockSpec((None, tl, width), lambda i, j: (i, j, 0))
    kt_spec = pl.BlockSpec((None, ML_HEADS, ML_DH, tl), lambda i, j: (i, 0, 0, j))
    cst = pl.BlockSpec((None, ML_CONV - 1, inner), lambda i, j: (i, 0, 0))
    in_specs = [tok(d), _full((1, d)), _full((d, 2 * inner)), cst, _full((ML_CONV, inner)), _full((1, inner)),
                _full(w['ml_wq'].shape), _full(w['ml_wkt'].shape), _full(w['ml_wv'].shape),
                _full(w['ml_w_gates'].shape), _full((1, LANES)), _full((1, inner))]
    out_specs = [tok(inner), kt_spec, tok(inner), tok(LANES), tok(inner), tok(inner), cst]
    sds = jax.ShapeDtypeStruct
    out_shape = (sds((b, L, inner), BF16), sds((b, ML_HEADS, ML_DH, L), BF16), sds((b, L, inner), BF16),
                 sds((b, L, LANES), F32), sds((b, L, inner), BF16), sds((b, L, inner), BF16),
                 sds((b, ML_CONV - 1, inner), F32))
    return pl.pallas_call(
        kern, grid=(b, nl), in_specs=in_specs, out_specs=out_specs, out_shape=out_shape,
        scratch_shapes=[pltpu.VMEM((tl + SUBLANES, inner), F32)],
        compiler_params=_params("parallel", "arbitrary"), name="mlstm_proj",
    )(x, w['norm_mix1'], w['ml_w_in'], conv_state, w['ml_conv_w'], w['ml_conv_b'],
      w['ml_wq'], w['ml_wkt'], w['ml_wv'], w['ml_w_gates'], w['ml_b_gates'], w['ml_skip'])


def _split3(x):
    hi = x.astype(BF16)
    r1 = x - hi.astype(F32)
    mid = r1.astype(BF16)
    lo = (r1 - mid.astype(F32)).astype(BF16)
    return hi, mid, lo


def _ml_b_kernel(*refs, tl, chunk, nl, has_state):
    if has_state:
        (x_ref, q_ref, kt_ref, v_ref, g_ref, sz_ref, sx_ref, c0_ref, n0_ref, m0_ref, gn_ref, wout_ref,
         xo_ref, c_ref, n_ref, m_ref, ncm_ref) = refs
    else:
        (x_ref, q_ref, kt_ref, v_ref, g_ref, sz_ref, sx_ref, gn_ref, wout_ref,
         xo_ref, c_ref, n_ref, m_ref, ncm_ref) = refs
    l = pl.program_id(1)

    @pl.when(l == 0)
    def _():
        if has_state:
            c_ref[...] = c0_ref[...]
            m_ref[...] = m0_ref[...]
            for h in range(ML_HEADS):
                ncm_ref[h] = jnp.broadcast_to(n0_ref[h:h + 1, :], (LANES, ML_DH)).T
        else:
            c_ref[...] = jnp.zeros(c_ref.shape, F32)
            m_ref[...] = jnp.zeros(m_ref.shape, F32)
            ncm_ref[...] = jnp.zeros(ncm_ref.shape, F32)

    for sub in range(tl // chunk):
        rows = slice(sub * chunk, (sub + 1) * chunk)
        _ml_b_chunk(x_ref.at[rows], q_ref.at[rows], kt_ref.at[:, :, rows], v_ref.at[rows], g_ref.at[rows],
                    sz_ref.at[rows], sx_ref.at[rows], gn_ref, wout_ref, xo_ref.at[rows], c_ref, m_ref, ncm_ref,
                    tl=chunk)

    @pl.when(l == nl - 1)
    def _():
        for h in range(ML_HEADS):
            n_ref[h:h + 1, :] = ncm_ref[h].T[0:1, :]


def _ml_b_chunk(x_ref, q_ref, kt_ref, v_ref, g_ref, sz_ref, sx_ref, gn_ref, wout_ref, xo_ref, c_ref, m_ref, ncm_ref,
                *, tl):
    scale = ML_DH ** -0.5
    gates = g_ref[...]
    row = lax.broadcasted_iota(jnp.int32, (tl, tl), 0)
    col = lax.broadcasted_iota(jnp.int32, (tl, tl), 1)
    causal = row >= col
    lower = jnp.where(causal, 1.0, 0.0).astype(BF16)
    ones = jnp.ones((tl, LANES), BF16)
    lf_c = _split3(_log_sigmoid(gates))
    b_f = _dot(lower, lf_c[0]) + _dot(lower, lf_c[1]) + _dot(lower, lf_c[2])
    b_c = pltpu.roll(b_f, LANES - ML_HEADS, 1)
    lane = lax.broadcasted_iota(jnp.int32, (1, LANES), 1)
    m_prev_row = jnp.zeros((1, LANES), F32)
    for h in range(ML_HEADS):
        m_prev_row = jnp.where(lane == h, m_ref[h], m_prev_row)
    run = gates - b_c
    r_t = run.T
    trow = lax.broadcasted_iota(jnp.int32, (tl, LANES), 0)
    shift = 1
    while shift < tl:
        run = jnp.maximum(run, jnp.where(trow >= shift, pltpu.roll(run, shift, 0), -jnp.inf))
        shift *= 2
    m_t_all = b_c + jnp.maximum(m_prev_row, run)
    s_inter_all = jnp.exp(b_c + m_prev_row - m_t_all)
    dmin_all = jnp.exp(-m_t_all)
    a_all = b_c - m_t_all + math.log(scale)
    m_new_row = m_t_all[tl - 1:tl, :]
    b_last_row = b_c[tl - 1:tl, :]
    dec_row = jnp.exp(b_last_row + m_prev_row - m_new_row)
    wsc_row = b_last_row - m_new_row

    hsl = [slice(h * ML_DH, (h + 1) * ML_DH) for h in range(ML_HEADS)]

    def state_dots(h):
        q = q_ref[:, hsl[h]]
        return (_dot(q, kt_ref[h]), _dot(q, c_ref[h].astype(BF16)), _dot(q, ncm_ref[h].astype(BF16))[:, 0:1])

    out = x_ref[...]
    nxt = state_dots(0)
    for h in range(ML_HEADS):
        hs = hsl[h]
        sqk, qc, qn = nxt
        v = v_ref[:, hs]
        r_row = r_t[h:h + 1, :]
        s_inter = s_inter_all[:, h:h + 1]
        dec = dec_row[:, h:h + 1]
        kwt = (kt_ref[h].astype(F32) * (scale * jnp.exp(r_row + wsc_row[:, h:h + 1]))).astype(BF16)
        c_ref[h] = dec * c_ref[h] + _dot(kwt, v)
        ncm_ref[h] = dec * ncm_ref[h] + _dot(kwt, ones)
        m_ref[h] = m_new_row[:, h:h + 1]
        wgt = jnp.exp(jnp.where(causal, a_all[:, h:h + 1] + r_row, -jnp.inf))
        qk = sqk * wgt
        num = s_inter * qc + _dot(qk.astype(BF16), v)
        den = s_inter * qn + jnp.sum(qk, axis=-1, keepdims=True)
        if h + 1 < ML_HEADS:
            nxt = state_dots(h + 1)
        dmax = jnp.maximum(jnp.abs(den), dmin_all[:, h:h + 1])
        xg = sz_ref[:, hs].astype(F32) * num
        mu = jnp.mean(xg, axis=-1, keepdims=True)
        var = jnp.maximum(jnp.mean(xg * xg, axis=-1, keepdims=True) - mu * mu, 0.0)
        y = (xg - mu) * lax.rsqrt(var + EPS * (dmax * dmax)) * gn_ref[:, hs] + sx_ref[:, hs].astype(F32)
        out = out + _dot(y.astype(BF16), wout_ref[hs, :])
    xo_ref[...] = out


def _ml_b(x, q, kt, v, gates, sz, sx, state, w):
    b, L, d = x.shape
    chunk = min(CHUNK, L)
    tl = min(CELL_TILE, L)
    assert L % tl == 0 and tl % chunk == 0 and chunk % SUBLANES == 0, (L, tl, chunk)
    nl = L // tl
    inner = ML_HEADS * ML_DH
    has_state = state is not None
    kern = functools.partial(_ml_b_kernel, tl=tl, chunk=chunk, nl=nl, has_state=has_state)
    tok = lambda width: pl.BlockSpec((None, tl, width), lambda i, j: (i, j, 0))
    kt_spec = pl.BlockSpec((None, ML_HEADS, ML_DH, tl), lambda i, j: (i, 0, 0, j))
    c_spec = pl.BlockSpec((None, ML_HEADS, ML_DH, ML_DH), lambda i, j: (i, 0, 0, 0))
    n_spec = pl.BlockSpec((None, ML_HEADS, ML_DH), lambda i, j: (i, 0, 0))
    m_spec = pl.BlockSpec((None, ML_HEADS, 1, 1), lambda i, j: (i, 0, 0, 0))
    in_specs = [tok(d), tok(inner), kt_spec, tok(inner), tok(LANES), tok(inner), tok(inner)]
    args = [x, q, kt, v, gates, sz, sx]
    if has_state:
        in_specs += [c_spec, n_spec, m_spec]
        args += [state[0], state[1], state[2].reshape(b, ML_HEADS, 1, 1)]
    in_specs += [_full((1, inner)), _full((inner, d))]
    args += [w['ml_gn'], w['ml_w_out']]
    sds = jax.ShapeDtypeStruct
    out_shape = (sds(x.shape, F32), sds((b, ML_HEADS, ML_DH, ML_DH), F32), sds((b, ML_HEADS, ML_DH), F32),
                 sds((b, ML_HEADS, 1, 1), F32))
    xo, c, n, m = pl.pallas_call(
        kern, grid=(b, nl), in_specs=in_specs, out_specs=[tok(d), c_spec, n_spec, m_spec], out_shape=out_shape,
        scratch_shapes=[pltpu.VMEM((ML_HEADS, ML_DH, LANES), F32)],
        compiler_params=_params("parallel", "arbitrary"), name="mlstm_cell",
    )(*args)
    return xo, c, n, m.reshape(b, ML_HEADS)


def _memkv_kernel(m_ref, g_ref, wk_ref, wv_ref, k_ref, v_ref, kb_ref, vb_ref):
    nb, m_len = k_ref.shape[0], k_ref.shape[1]
    mn = _rms(m_ref[...], g_ref[...]).astype(BF16)
    k = _dot(mn, wk_ref[...])
    v = _dot(mn, wv_ref[...])
    for h in range(X_HEADS):
        kh = k[:, h * X_DH:(h + 1) * X_DH].reshape(nb, m_len, X_DH)
        vh = v[:, h * X_DH:(h + 1) * X_DH].reshape(nb, m_len, X_DH)
        k_ref[:, :, h, :] = kh
        v_ref[:, :, h, :] = vh
        kb_ref[:, h] = kh.astype(BF16)
        vb_ref[:, h] = vh.astype(BF16)


def _memkv(mem, norm_mem, wk, wv):
    b, m_len, d = mem.shape
    depth = wk.shape[0]
    rows = b * m_len
    tm = min(MEM_ROW_TILE, rows)
    assert rows % tm == 0 and tm % m_len == 0 and d == X_HEADS * X_DH, (rows, tm, m_len, d)
    nb = tm // m_len
    mem2 = mem.reshape(rows, d)
    wspec = pl.BlockSpec((None, d, d), lambda i, j: (i, 0, 0))
    ospec = pl.BlockSpec((None, nb, m_len, X_HEADS, X_DH), lambda i, j: (i, j, 0, 0, 0))
    oshape = jax.ShapeDtypeStruct((depth, b, m_len, X_HEADS, X_DH), F32)
    bspec = pl.BlockSpec((None, nb, X_HEADS, m_len, X_DH), lambda i, j: (i, j, 0, 0, 0))
    bshape = jax.ShapeDtypeStruct((depth, b, X_HEADS, m_len, X_DH), BF16)
    return pl.pallas_call(
        _memkv_kernel, grid=(depth, rows // tm),
        in_specs=[pl.BlockSpec((tm, d), lambda i, j: (j, 0)), pl.BlockSpec((None, 1, d), lambda i, j: (i, 0, 0)),
                  wspec, wspec],
        out_specs=[ospec, ospec, bspec, bspec],
        out_shape=(oshape, oshape, bshape, bshape),
        compiler_params=_params("parallel", "parallel"), name="memory_kv",
    )(mem2, norm_mem.reshape(depth, 1, d), wk, wv)


def _xattn_kernel(x_ref, g_ref, wq_ref, mk_ref, mv_ref, wo_ref, o_ref):
    x = x_ref[...]
    xn = _rms(x, g_ref[...]).astype(BF16)
    q = _dot(xn, wq_ref[...])
    hsl = [slice(h * X_DH, (h + 1) * X_DH) for h in range(X_HEADS)]

    def scores(h):
        return _dot_nt(q[:, hsl[h]].astype(BF16), mk_ref[h])

    out = x
    nxt = scores(0)
    for h in range(X_HEADS):
        s = nxt * (X_DH ** -0.5)
        if h + 1 < X_HEADS:
            nxt = scores(h + 1)
        e = jnp.exp(s - jnp.max(s, axis=-1, keepdims=True))
        pv = _dot(e.astype(BF16), mv_ref[h])
        oh = pv * (1.0 / jnp.sum(e, axis=-1, keepdims=True))
        out = out + _dot(oh.astype(BF16), wo_ref[hsl[h], :])
    o_ref[...] = out


def _xattn(x, layer, g, wq, mk, mv, wo):
    b, L, d = x.shape
    tl = min(XATTN_TILE, L)
    assert L % tl == 0 and d == X_HEADS * X_DH, (L, tl, d)
    m_len = mk.shape[3]
    tok = pl.BlockSpec((None, tl, d), lambda i, j: (i, j, 0))
    mem = pl.BlockSpec((None, None, X_HEADS, m_len, X_DH), lambda i, j: (layer, i, 0, 0, 0))
    wsp = pl.BlockSpec((None, d, d), lambda i, j: (layer, 0, 0))
    gsp = pl.BlockSpec((None, 1, d), lambda i, j: (layer, 0, 0))
    return pl.pallas_call(
        _xattn_kernel, grid=(b, L // tl),
        in_specs=[tok, gsp, wsp, mem, mem, wsp],
        out_specs=tok, out_shape=jax.ShapeDtypeStruct(x.shape, F32),
        compiler_params=_params("parallel", "arbitrary"), name="cross_attn",
    )(x, g, wq, mk, mv, wo)


def _mlp_kernel(*refs, final):
    if final:
        x_ref, g_ref, wu_ref, wd_ref, gf_ref, o_ref, xn_ref = refs
    else:
        x_ref, g_ref, wu_ref, wd_ref, o_ref, xn_ref = refs
    j = pl.program_id(1)

    @pl.when(j == 0)
    def _():
        x = x_ref[...]
        xn_ref[...] = _rms(x, g_ref[...]).astype(BF16)
        o_ref[...] = x

    hdn = jnp.maximum(_dot(xn_ref[...], wu_ref[...]), 0.0)
    o_ref[...] += _dot((hdn * hdn).astype(BF16), wd_ref[...])

    if final:
        @pl.when(j == pl.num_programs(1) - 1)
        def _():
            o_ref[...] = _rms(o_ref[...], gf_ref[...])


def _mlp(x, layer, g, wu, wd, gf=None):
    shape = x.shape
    d = shape[-1]
    x2 = x.reshape(-1, d)
    rows = x2.shape[0]
    ff = wu.shape[2]
    tm = min(MLP_ROW_TILE, rows)
    tf = min(MLP_FF_TILE, ff)
    assert rows % tm == 0 and ff % tf == 0, (rows, tm, ff, tf)
    final = gf is not None
    wmode = pl.Buffered(1) if tf == ff else None
    in_specs = [pl.BlockSpec((tm, d), lambda i, j: (i, 0)), pl.BlockSpec((None, 1, d), lambda i, j: (layer, 0, 0)),
                pl.BlockSpec((None, d, tf), lambda i, j: (layer, 0, j), pipeline_mode=wmode),
                pl.BlockSpec((None, tf, d), lambda i, j: (layer, j, 0), pipeline_mode=wmode)]
    args = [x2, g, wu, wd]
    if final:
        in_specs.append(_full((1, d)))
        args.append(gf)
    out = pl.pallas_call(
        functools.partial(_mlp_kernel, final=final), grid=(rows // tm, ff // tf),
        in_specs=in_specs, out_specs=pl.BlockSpec((tm, d), lambda i, j: (i, 0)),
        out_shape=jax.ShapeDtypeStruct(x2.shape, F32),
        scratch_shapes=[pltpu.VMEM((tm, d), BF16)],
        compiler_params=_params("parallel", "arbitrary"), name="mlp",
    )(*args)
    return out.reshape(shape)


def _trunk(x, pos0, st_ret, st_re, st_im, ml_state, conv_state, mem_k, mem_v, w):
    b, L, d = x.shape
    chunk = min(CHUNK, L)
    nst = S5_GROUPS * S5_STATE // (SUBLANES * LANES)
    st_h = jnp.concatenate([st_re.reshape(b, nst, SUBLANES, LANES), st_im.reshape(b, nst, SUBLANES, LANES)], axis=1)
    x, s_new, h_new = _rs_layer(x, st_ret, st_h, _ret_consts(L, chunk, pos0), w)
    hr = h_new[:, :nst].reshape(b, S5_GROUPS, S5_STATE)
    hi = h_new[:, nst:].reshape(b, S5_GROUPS, S5_STATE)
    x = _xattn(x, 0, w['norm_cross'], w['x_wq'], mem_k, mem_v, w['x_wo'])
    x = _mlp(x, 0, w['norm_mlp'], w['mlp_w_up'], w['mlp_w_down'])
    q, kt, v, gates, sz, sx, new_conv = _ml_a(x, conv_state, w)
    x, cf, nf, mf = _ml_b(x, q, kt, v, gates, sz, sx, ml_state, w)
    x = _xattn(x, 1, w['norm_cross'], w['x_wq'], mem_k, mem_v, w['x_wo'])
    y = _mlp(x, 1, w['norm_mlp'], w['mlp_w_up'], w['mlp_w_down'], w['norm_final'])
    return y, s_new[None], hr[None], hi[None], cf[None], nf[None], mf[None], new_conv[None]


def kernel(x_prompt, x_sample, mem_prompt, state_ret, state_s5_re, state_s5_im, state_mlstm_c, state_mlstm_n, state_mlstm_m, cache_mlstm_conv, cache_mem_k, cache_mem_v, norm_mix, norm_cross, norm_mem, norm_mlp, norm_final, rs_w_in, rs_w_out, ret_gn, s5_a_re, s5_a_im, s5_log_dt, s5_b_re, s5_b_im, s5_c_re, s5_c_im, s5_d, s5_w_glu, s5_b_glu, ml_w_in, ml_conv_w, ml_conv_b, ml_wq, ml_wk, ml_wv, ml_w_gates, ml_b_gates, ml_gn, ml_skip, ml_w_out, x_wq, x_wk, x_wv, x_wo, mlp_w_up, mlp_w_down):
    d = x_prompt.shape[-1]
    bp = x_prompt.shape[0]
    bf = lambda t: t.astype(BF16)
    row = lambda t: t.reshape(1, -1).astype(F32)
    ab_re, ab_im, bb_re, bb_im = _s5_prep(s5_a_re[0], s5_a_im[0], s5_log_dt[0], s5_b_re[0], s5_b_im[0])
    s5_a, s5_b, s5_c = _s5_matrices(ab_re, ab_im, bb_re, bb_im, s5_c_re[0], s5_c_im[0])
    n_gate = ml_w_gates.shape[-1]
    w = {
        'norm_mix0': row(norm_mix[0]), 'norm_mix1': row(norm_mix[1]),
        'norm_cross': norm_cross.reshape(-1, 1, d), 'norm_mlp': norm_mlp.reshape(-1, 1, d),
        'norm_final': row(norm_final),
        'rs_w_in': bf(rs_w_in[0]), 'rs_w_out': bf(rs_w_out[0]), 'ret_gn': row(ret_gn[0]),
        's5_a': s5_a, 's5_b': s5_b, 's5_c': s5_c, 's5_d': row(s5_d[0]), 's5_w_glu': bf(s5_w_glu[0]),
        's5_b_glu': row(s5_b_glu[0]),
        'ml_w_in': bf(ml_w_in[0]), 'ml_conv_w': ml_conv_w[0], 'ml_conv_b': row(ml_conv_b[0]),
        'ml_wq': bf(ml_wq[0]), 'ml_wkt': bf(jnp.swapaxes(ml_wk[0], 1, 2)), 'ml_wv': bf(ml_wv[0]),
        'ml_w_gates': bf(_gate_fold(ml_wq[0], ml_wk[0], ml_wv[0],
                                    jnp.pad(ml_w_gates[0], ((0, 0), (0, LANES - n_gate))))),
        'ml_b_gates': jnp.pad(ml_b_gates[0], (0, LANES - n_gate)).reshape(1, LANES),
        'ml_gn': row(ml_gn[0]), 'ml_skip': row(ml_skip[0]), 'ml_w_out': bf(ml_w_out[0]),
        'x_wq': bf(x_wq), 'x_wo': bf(x_wo), 'mlp_w_up': bf(mlp_w_up), 'mlp_w_down': bf(mlp_w_down),
    }
    mk_p, mv_p, mkb_p, mvb_p = _memkv(mem_prompt, norm_mem, bf(x_wk), bf(x_wv))
    zeros = lambda *s: jnp.zeros(s, F32)
    out_p = _trunk(x_prompt, 0, zeros(bp, RET_HEADS, RET_DK, RET_DK), zeros(bp, S5_GROUPS, S5_STATE),
                   zeros(bp, S5_GROUPS, S5_STATE), None, zeros(bp, ML_CONV - 1, ML_HEADS * ML_DH), mkb_p, mvb_p, w)
    out_s = _trunk(x_sample, PAST_LEN, state_ret[0], state_s5_re[0], state_s5_im[0],
                   (state_mlstm_c[0], state_mlstm_n[0], state_mlstm_m[0]), cache_mlstm_conv[0],
                   bf(jnp.swapaxes(cache_mem_k, 2, 3)), bf(jnp.swapaxes(cache_mem_v, 2, 3)), w)
    return (out_p[0], out_s[0]) + tuple(out_p[1:]) + (mk_p, mv_p) + tuple(out_s[1:])
```

```python
import functools
import math

import jax
import jax.numpy as jnp
from jax import lax
from jax.experimental import pallas as pl
from jax.experimental.pallas import tpu as pltpu

F32 = jnp.float32
BF16 = jnp.bfloat16

EPS = 1e-6
ROPE_BASE = 10000.0
PAST_LEN = 4096
RET_HEADS = 4
RET_DK = 128
S5_GROUPS = 32
S5_GROUP = 16
S5_STATE = 64
ML_HEADS = 4
ML_DH = 512
ML_CONV = 4
X_HEADS = 4
X_DH = 256
LANES = 128
SUBLANES = 8
VMEM_LIMIT = 56 * 1024 * 1024
RET_CHUNK = 512
MIXER_TILE = 512
CHUNK = 256
CELL_TILE = 256
PROJ_TILE = 512
XATTN_TILE = 1024
MLP_ROW_TILE = 512
MLP_FF_TILE = 4096
MEM_ROW_TILE = 1024


def _dot(a, b):
    return jnp.dot(a, b, preferred_element_type=F32)


def _dot_nt(a, b):
    return lax.dot_general(a, b, (((1,), (1,)), ((), ())), preferred_element_type=F32)


def _dot_tn(a, b):
    return lax.dot_general(a, b, (((0,), (0,)), ((), ())), preferred_element_type=F32)


def _rms(x, g):
    y = x * lax.rsqrt(jnp.mean(x * x, axis=-1, keepdims=True) + EPS)
    return y * g


def _sigmoid(x):
    return 1.0 / (1.0 + jnp.exp(-x))


def _log_sigmoid(x):
    return -(jnp.maximum(-x, 0.0) + jnp.log(1.0 + jnp.exp(-jnp.abs(x))))


def _gelu_tanh(x):
    c = math.sqrt(2.0 / math.pi)
    return 0.5 * x * (1.0 + jnp.tanh(c * (x + 0.044715 * (x * x * x))))


def _params(*sem):
    return pltpu.CompilerParams(dimension_semantics=sem, vmem_limit_bytes=VMEM_LIMIT)


def _full(shape):
    n = len(shape)
    return pl.BlockSpec(shape, lambda *_: (0,) * n, pipeline_mode=pl.Buffered(1))


def _s5_prep_kernel(are_ref, aim_ref, ldt_ref, bre_ref, bim_ref, abre_ref, abim_ref, bbre_ref, bbim_ref):
    a_re = are_ref[...]
    a_im = aim_ref[...]
    dt = jnp.exp(ldt_ref[...])
    mag = jnp.exp(a_re * dt)
    ab_re = mag * jnp.cos(a_im * dt)
    ab_im = mag * jnp.sin(a_im * dt)
    den = a_re * a_re + a_im * a_im
    x_re = ab_re - 1.0
    f_re = (x_re * a_re + ab_im * a_im) / den
    f_im = (ab_im * a_re - x_re * a_im) / den
    b_re = bre_ref[...]
    b_im = bim_ref[...]
    abre_ref[...] = ab_re
    abim_ref[...] = ab_im
    bbre_ref[...] = f_re * b_re - f_im * b_im
    bbim_ref[...] = f_re * b_im + f_im * b_re


def _s5_prep(a_re, a_im, log_dt, b_re, b_im):
    g, p, c = b_re.shape
    out = pl.pallas_call(
        _s5_prep_kernel,
        out_shape=(jax.ShapeDtypeStruct((g, 1, p), F32), jax.ShapeDtypeStruct((g, 1, p), F32),
                   jax.ShapeDtypeStruct((g, c, p), F32), jax.ShapeDtypeStruct((g, c, p), F32)),
        name="s5_prep",
    )(a_re.reshape(g, 1, p), a_im.reshape(g, 1, p), log_dt.reshape(g, 1, 1),
      jnp.swapaxes(b_re, 1, 2), jnp.swapaxes(b_im, 1, 2))
    return out


def _s5_matrices(ab_re, ab_im, bb_re, bb_im, c_re, c_im):
    g, c, p = bb_re.shape
    nsl = g * c // LANES
    gs = LANES // c
    eye = jnp.eye(gs, dtype=F32)

    def bmat(bb):
        t = bb.reshape(nsl, gs, c, p)
        w = jnp.einsum('mgcp,hg->mhcgp', t, eye)
        return w.reshape(nsl, gs * c, gs * p)

    def cmat(cc):
        t = cc.reshape(nsl, gs, c, p)
        w = jnp.einsum('mgcp,hg->mhpgc', t, eye)
        return w.reshape(nsl, gs * p, gs * c)

    b_all = jnp.stack([bmat(bb_re), bmat(bb_im)], axis=1).reshape(nsl * 2, gs * c, gs * p)
    c_all = jnp.concatenate([cmat(c_re), -cmat(c_im)], axis=1)
    nst = g * p // (SUBLANES * LANES)
    a_all = jnp.concatenate([ab_re.reshape(nst, SUBLANES, LANES), ab_im.reshape(nst, SUBLANES, LANES)], axis=0)
    return a_all, b_all.astype(BF16), c_all.astype(BF16)


def _rs_kernel(*refs, tl, chunk):
    *chunk_refs, bu_ref = refs
    s0_ref, h0_ref = chunk_refs[9:11]
    s_ref, h_ref = chunk_refs[-2:]

    @pl.when(pl.program_id(1) == 0)
    def _():
        s_ref[...] = s0_ref[...]
        h_ref[...] = h0_ref[...]

    for sub in range(tl // chunk):
        _rs_chunk(*chunk_refs, bu_ref.at[sub], rows=slice(sub * chunk, (sub + 1) * chunk), chunk=chunk)


def _rs_chunk(x_ref, gn_ref, win_ref, cos_ref, sin_ref, decay_ref, wq_ref, wkv_ref, gc_ref,
              s0_ref, h0_ref, rgn_ref, a_ref, bm_ref, cm_ref, dsk_ref, wglu_ref, bglu_ref, wout_ref,
              xo_ref, s_ref, h_ref, bu, *, rows, chunk):
    qk_w = RET_HEADS * RET_DK
    nsl = S5_GROUPS * S5_GROUP // LANES
    nst = S5_GROUPS * S5_STATE // (SUBLANES * LANES)
    rows_per_slab = SUBLANES // (nsl // nst)

    x = x_ref[rows, :]
    xn = _rms(x, gn_ref[...]).astype(BF16)
    proj = _dot(xn, win_ref[...])
    cos = cos_ref[rows, :]
    sin = sin_ref[rows, :]

    u_off = 4 * qk_w
    for m in range(nsl):
        um = proj[:, u_off + m * LANES: u_off + (m + 1) * LANES].astype(BF16)
        for ri in range(2):
            r = _dot(um, bm_ref[2 * m + ri])
            slab = ri * nst + m // (nsl // nst)
            for jl in range(rows_per_slab):
                j = rows_per_slab * (m % (nsl // nst)) + jl
                bu[slab, pl.ds(j, chunk, stride=SUBLANES), :] = r[:, jl * LANES:(jl + 1) * LANES]

    a = [a_ref[i] for i in range(2 * nst)]

    def scan_steps(carry, t0, t1):
        for t in range(t0, t1):
            trow = pl.ds(t * SUBLANES, SUBLANES)
            new = []
            for s in range(nst):
                hr, hi = carry[s], carry[nst + s]
                ar, ai = a[s], a[nst + s]
                nr = ar * hr - ai * hi + bu[s, trow, :]
                ni = ar * hi + ai * hr + bu[nst + s, trow, :]
                bu[s, trow, :] = nr
                bu[nst + s, trow, :] = ni
                new.append((nr, ni))
            carry = tuple(n[0] for n in new) + tuple(n[1] for n in new)
        return carry

    def rope_head(h):
        q = proj[:, h * RET_DK:(h + 1) * RET_DK]
        k = proj[:, qk_w + h * RET_DK: qk_w + (h + 1) * RET_DK]
        qr = q * cos + pltpu.roll(q, RET_DK // 2, 1) * sin
        kr = (k * cos + pltpu.roll(k, RET_DK // 2, 1) * sin) * (RET_DK ** -0.5)
        qb = qr.astype(BF16)
        return qb, kr, _dot_nt(qb, kr.astype(BF16)), _dot(qb, s_ref[h].astype(BF16))

    carry = tuple(h_ref[i] for i in range(2 * nst))
    per_head = chunk // RET_HEADS
    pieces = []
    nxt = rope_head(0)
    for h in range(RET_HEADS):
        hs = slice(h * RET_DK, (h + 1) * RET_DK)
        qb, kr, sqk, cross = nxt
        vb = proj[:, 2 * qk_w + h * RET_DK: 2 * qk_w + (h + 1) * RET_DK].astype(BF16)
        g = proj[:, 3 * qk_w + h * RET_DK: 3 * qk_w + (h + 1) * RET_DK]
        o = _dot((sqk * decay_ref[h]).astype(BF16), vb) + cross * wq_ref[:, hs]
        if h + 1 < RET_HEADS:
            nxt = rope_head(h + 1)
        carry = scan_steps(carry, h * per_head, (h + 1) * per_head)
        s_ref[h] = gc_ref[h] * s_ref[h] + _dot_tn((kr * wkv_ref[:, hs]).astype(BF16), vb)
        mu = jnp.mean(o, axis=-1, keepdims=True)
        var = jnp.maximum(jnp.mean(o * o, axis=-1, keepdims=True) - mu * mu, 0.0)
        y = (o - mu) * lax.rsqrt(var + EPS) * rgn_ref[:, hs]
        pieces.append((g * _sigmoid(g) * y).astype(BF16))
    carry = scan_steps(carry, RET_HEADS * per_head, chunk)
    for i in range(2 * nst):
        h_ref[i] = carry[i]
    out = x + _dot(jnp.concatenate(pieces, axis=1), wout_ref[0:qk_w, :])

    ys = []
    for m in range(nsl):
        parts = []
        for ri in range(2):
            slab = ri * nst + m // (nsl // nst)
            for jl in range(rows_per_slab):
                j = rows_per_slab * (m % (nsl // nst)) + jl
                parts.append(bu[slab, pl.ds(j, chunk, stride=SUBLANES), :])
        hcat = jnp.concatenate(parts, axis=1).astype(BF16)
        um = proj[:, u_off + m * LANES: u_off + (m + 1) * LANES]
        ym = _dot(hcat, cm_ref[m]) + dsk_ref[:, m * LANES:(m + 1) * LANES] * um
        ys.append(_gelu_tanh(ym))
    yg = jnp.concatenate(ys, axis=1)
    gate = _sigmoid(_dot(yg.astype(BF16), wglu_ref[...]) + bglu_ref[...])
    xo_ref[rows, :] = out + _dot((yg * gate).astype(BF16), wout_ref[qk_w:, :])


def _rs_layer(x, st_ret, st_h, consts, w):
    b, L, d = x.shape
    chunk = min(RET_CHUNK, L)
    tl = min(MIXER_TILE, L)
    assert L % tl == 0 and tl % chunk == 0 and chunk % SUBLANES == 0, (L, tl, chunk)
    nl = L // tl
    cos, sin, decay, wq, wkv, gc = consts
    rs_in = w['rs_w_in'].shape[1]
    nst2 = st_h.shape[1]
    kern = functools.partial(_rs_kernel, tl=tl, chunk=chunk)
    tok = pl.BlockSpec((None, tl, d), lambda i, j: (i, j, 0))
    in_specs = [
        tok, _full((1, d)), _full((d, rs_in)),
        pl.BlockSpec((tl, LANES), lambda i, j: (j, 0)), pl.BlockSpec((tl, LANES), lambda i, j: (j, 0)),
        _full(decay.shape), _full(wq.shape), _full(wkv.shape), _full(gc.shape),
        pl.BlockSpec((None,) + st_ret.shape[1:], lambda i, j: (i, 0, 0, 0)),
        pl.BlockSpec((None,) + st_h.shape[1:], lambda i, j: (i, 0, 0, 0)),
        _full((1, RET_HEADS * RET_DK)), _full(w['s5_a'].shape), _full(w['s5_b'].shape), _full(w['s5_c'].shape),
        _full((1, w['s5_d'].shape[1])), _full(w['s5_w_glu'].shape), _full((1, w['s5_b_glu'].shape[1])),
        _full(w['rs_w_out'].shape),
    ]
    out_specs = [
        tok,
        pl.BlockSpec((None,) + st_ret.shape[1:], lambda i, j: (i, 0, 0, 0)),
        pl.BlockSpec((None,) + st_h.shape[1:], lambda i, j: (i, 0, 0, 0)),
    ]
    return pl.pallas_call(
        kern, grid=(b, nl), in_specs=in_specs, out_specs=out_specs,
        out_shape=(jax.ShapeDtypeStruct(x.shape, F32), jax.ShapeDtypeStruct(st_ret.shape, F32),
                   jax.ShapeDtypeStruct(st_h.shape, F32)),
        scratch_shapes=[pltpu.VMEM((tl // chunk, nst2, chunk * SUBLANES, LANES), F32)],
        compiler_params=_params("parallel", "arbitrary"), name="rs_mixer",
    )(x, w['norm_mix0'], w['rs_w_in'], cos, sin, decay, wq, wkv, gc, st_ret, st_h, w['ret_gn'],
      w['s5_a'], w['s5_b'], w['s5_c'], w['s5_d'], w['s5_w_glu'], w['s5_b_glu'], w['rs_w_out'])


def _ret_consts(L, tl, pos0):
    half = RET_DK // 2
    pos = pos0 + jnp.arange(L, dtype=jnp.int32)
    inv = ROPE_BASE ** (-jnp.arange(half, dtype=F32) / half)
    ang = pos.astype(F32)[:, None] * inv[None, :]
    cos = jnp.cos(ang)
    sin = jnp.sin(ang)
    cos2 = jnp.concatenate([cos, cos], axis=1)
    sin2 = jnp.concatenate([-sin, sin], axis=1)
    log_g = jnp.log1p(-jnp.exp2(-5.0 - jnp.arange(RET_HEADS, dtype=F32)))
    idx = jnp.arange(tl, dtype=F32)
    diff = idx[:, None] - idx[None, :]
    decay = jnp.where(diff[None] >= 0, jnp.exp(jnp.maximum(diff, 0.0)[None] * log_g[:, None, None]), 0.0)
    w_q = jnp.exp((idx + 1.0)[:, None] * log_g[None, :])
    w_kv = jnp.exp((tl - 1.0 - idx)[:, None] * log_g[None, :])
    g_c = jnp.exp(tl * log_g)
    rep = lambda t: jnp.repeat(t, RET_DK, axis=1)
    gc = jnp.broadcast_to(g_c[:, None, None], (RET_HEADS, 1, RET_DK))
    return cos2, sin2, decay, rep(w_q), rep(w_kv), gc


def _gate_fold_kernel(wq_ref, wk_ref, wv_ref, wg_ref, o_ref):
    inner = ML_HEADS * ML_DH
    hp = lax.Precision.HIGHEST
    for h in range(ML_HEADS):
        rows = slice(h * ML_DH, (h + 1) * ML_DH)
        gq = wg_ref[h * ML_DH:(h + 1) * ML_DH, :]
        gk = wg_ref[inner + h * ML_DH: inner + (h + 1) * ML_DH, :]
        gv = wg_ref[2 * inner + h * ML_DH: 2 * inner + (h + 1) * ML_DH, :]
        o_ref[0, rows, :] = (jnp.dot(wq_ref[h], gq, precision=hp, preferred_element_type=F32)
                             + jnp.dot(wk_ref[h], gk, precision=hp, preferred_element_type=F32))
        o_ref[1, rows, :] = jnp.dot(wv_ref[h], gv, precision=hp, preferred_element_type=F32)


def _gate_fold(wq, wk, wv, wg_padded):
    inner = ML_HEADS * ML_DH
    return pl.pallas_call(
        _gate_fold_kernel, out_shape=jax.ShapeDtypeStruct((2, inner, LANES), F32), name="gate_fold",
    )(wq, wk, wv, wg_padded)


def _ml_a_kernel(x_ref, gn_ref, win_ref, cs_ref, cw_ref, cb_ref, wq_ref, wkt_ref, wv_ref, wg_ref, bg_ref, skip_ref,
                 q_ref, kt_ref, v_ref, gates_ref, sz_ref, sx_ref, nc_ref, xs_ref, *, tl, nl):
    l = pl.program_id(1)
    inner = ML_HEADS * ML_DH
    pad = SUBLANES
    hist = ML_CONV - 1

    @pl.when(l == 0)
    def _():
        xs_ref[0:pad, :] = jnp.zeros((pad, inner), F32)
        xs_ref[pl.ds(pad - hist, hist), :] = cs_ref[...]

    xn = _rms(x_ref[...], gn_ref[...]).astype(BF16)
    xm = _dot(xn, win_ref[:, 0:inner])
    sz_ref[...] = _sigmoid(_dot(xn, win_ref[:, inner:2 * inner])).astype(sz_ref.dtype)
    xs_ref[pl.ds(pad, tl), :] = xm
    acc = cw_ref[hist:hist + 1, :] * xm
    for j in range(hist):
        acc = acc + cw_ref[j:j + 1, :] * xs_ref[pl.ds(pad - hist + j, tl), :]
    xc = acc + cb_ref[...]
    xc = xc * _sigmoid(xc)
    sx_ref[...] = (skip_ref[...] * xc).astype(sx_ref.dtype)
    xcb = xc.astype(BF16)
    xmb = xm.astype(BF16)
    gates_ref[...] = _dot(xcb, wg_ref[0]) + _dot(xmb, wg_ref[1]) + bg_ref[...]
    for h in range(ML_HEADS):
        hs = slice(h * ML_DH, (h + 1) * ML_DH)
        q_ref[:, hs] = _dot(xcb[:, hs], wq_ref[h]).astype(BF16)
        kt_ref[h] = _dot_nt(wkt_ref[h], xcb[:, hs]).astype(BF16)
        v_ref[:, hs] = _dot(xmb[:, hs], wv_ref[h]).astype(BF16)

    @pl.when(l == nl - 1)
    def _():
        nc_ref[...] = xs_ref[pl.ds(pad + tl - hist, hist), :]

    xs_ref[0:pad, :] = xs_ref[pl.ds(tl, pad), :]


def _ml_a(x, conv_state, w):
    b, L, d = x.shape
    tl = min(PROJ_TILE, L)
    assert L % tl == 0 and tl % SUBLANES == 0 and tl >= ML_CONV - 1, (L, tl)
    nl = L // tl
    inner = ML_HEADS * ML_DH
    kern = functools.partial(_ml_a_kernel, tl=tl, nl=nl)
    tok = lambda width: pl.BlockSpec((None, tl, width), lambda i, j: (i, j, 0))
    kt_spec = pl.BlockSpec((None, ML_HEADS, ML_DH, tl), lambda i, j: (i, 0, 0, j))
    cst = pl.BlockSpec((None, ML_CONV - 1, inner), lambda i, j: (i, 0, 0))
    in_specs = [tok(d), _full((1, d)), _full((d, 2 * inner)), cst, _full((ML_CONV, inner)), _full((1, inner)),
                _full(w['ml_wq'].shape), _full(w['ml_wkt'].shape), _full(w['ml_wv'].shape),
                _full(w['ml_w_gates'].shape), _full((1, LANES)), _full((1, inner))]
    out_specs = [tok(inner), kt_spec, tok(inner), tok(LANES), tok(inner), tok(inner), cst]
    sds = jax.ShapeDtypeStruct
    out_shape = (sds((b, L, inner), BF16), sds((b, ML_HEADS, ML_DH, L), BF16), sds((b, L, inner), BF16),
                 sds((b, L, LANES), F32), sds((b, L, inner), BF16), sds((b, L, inner), BF16),
                 sds((b, ML_CONV - 1, inner), F32))
    return pl.pallas_call(
        kern, grid=(b, nl), in_specs=in_specs, out_specs=out_specs, out_shape=out_shape,
        scratch_shapes=[pltpu.VMEM((tl + SUBLANES, inner), F32)],
        compiler_params=_params("parallel", "arbitrary"), name="mlstm_proj",
    )(x, w['norm_mix1'], w['ml_w_in'], conv_state, w['ml_conv_w'], w['ml_conv_b'],
      w['ml_wq'], w['ml_wkt'], w['ml_wv'], w['ml_w_gates'], w['ml_b_gates'], w['ml_skip'])


def _split3(x):
    hi = x.astype(BF16)
    r1 = x - hi.astype(F32)
    mid = r1.astype(BF16)
    lo = (r1 - mid.astype(F32)).astype(BF16)
    return hi, mid, lo


def _ml_b_kernel(*refs, tl, chunk, nl, has_state):
    if has_state:
        (x_ref, q_ref, kt_ref, v_ref, g_ref, sz_ref, sx_ref, c0_ref, n0_ref, m0_ref, gn_ref, wout_ref,
         xo_ref, c_ref, n_ref, m_ref, ncm_ref) = refs
    else:
        (x_ref, q_ref, kt_ref, v_ref, g_ref, sz_ref, sx_ref, gn_ref, wout_ref,
         xo_ref, c_ref, n_ref, m_ref, ncm_ref) = refs
    l = pl.program_id(1)

    @pl.when(l == 0)
    def _():
        if has_state:
            c_ref[...] = c0_ref[...]
            m_ref[...] = m0_ref[...]
            for h in range(ML_HEADS):
                ncm_ref[h] = jnp.broadcast_to(n0_ref[h:h + 1, :], (LANES, ML_DH)).T
        else:
            c_ref[...] = jnp.zeros(c_ref.shape, F32)
            m_ref[...] = jnp.zeros(m_ref.shape, F32)
            ncm_ref[...] = jnp.zeros(ncm_ref.shape, F32)

    for sub in range(tl // chunk):
        rows = slice(sub * chunk, (sub + 1) * chunk)
        _ml_b_chunk(x_ref.at[rows], q_ref.at[rows], kt_ref.at[:, :, rows], v_ref.at[rows], g_ref.at[rows],
                    sz_ref.at[rows], sx_ref.at[rows], gn_ref, wout_ref, xo_ref.at[rows], c_ref, m_ref, ncm_ref,
                    tl=chunk)

    @pl.when(l == nl - 1)
    def _():
        for h in range(ML_HEADS):
            n_ref[h:h + 1, :] = ncm_ref[h].T[0:1, :]


def _ml_b_chunk(x_ref, q_ref, kt_ref, v_ref, g_ref, sz_ref, sx_ref, gn_ref, wout_ref, xo_ref, c_ref, m_ref, ncm_ref,
                *, tl):
    scale = ML_DH ** -0.5
    gates = g_ref[...]
    row = lax.broadcasted_iota(jnp.int32, (tl, tl), 0)
    col = lax.broadcasted_iota(jnp.int32, (tl, tl), 1)
    causal = row >= col
    lower = jnp.where(causal, 1.0, 0.0).astype(BF16)
    ones = jnp.ones((tl, LANES), BF16)
    lf_c = _split3(_log_sigmoid(gates))
    b_f = _dot(lower, lf_c[0]) + _dot(lower, lf_c[1]) + _dot(lower, lf_c[2])
    b_c = pltpu.roll(b_f, LANES - ML_HEADS, 1)
    lane = lax.broadcasted_iota(jnp.int32, (1, LANES), 1)
    m_prev_row = jnp.zeros((1, LANES), F32)
    for h in range(ML_HEADS):
        m_prev_row = jnp.where(lane == h, m_ref[h], m_prev_row)
    run = gates - b_c
    r_t = run.T
    trow = lax.broadcasted_iota(jnp.int32, (tl, LANES), 0)
    shift = 1
    while shift < tl:
        run = jnp.maximum(run, jnp.where(trow >= shift, pltpu.roll(run, shift, 0), -jnp.inf))
        shift *= 2
    m_t_all = b_c + jnp.maximum(m_prev_row, run)
    s_inter_all = jnp.exp(b_c + m_prev_row - m_t_all)
    dmin_all = jnp.exp(-m_t_all)
    a_all = b_c - m_t_all + math.log(scale)
    m_new_row = m_t_all[tl - 1:tl, :]
    b_last_row = b_c[tl - 1:tl, :]
    dec_row = jnp.exp(b_last_row + m_prev_row - m_new_row)
    wsc_row = b_last_row - m_new_row

    hsl = [slice(h * ML_DH, (h + 1) * ML_DH) for h in range(ML_HEADS)]

    def state_dots(h):
        q = q_ref[:, hsl[h]]
        return (_dot(q, kt_ref[h]), _dot(q, c_ref[h].astype(BF16)), _dot(q, ncm_ref[h].astype(BF16))[:, 0:1])

    out = x_ref[...]
    nxt = state_dots(0)
    for h in range(ML_HEADS):
        hs = hsl[h]
        sqk, qc, qn = nxt
        v = v_ref[:, hs]
        r_row = r_t[h:h + 1, :]
        s_inter = s_inter_all[:, h:h + 1]
        dec = dec_row[:, h:h + 1]
        kwt = (kt_ref[h].astype(F32) * (scale * jnp.exp(r_row + wsc_row[:, h:h + 1]))).astype(BF16)
        c_ref[h] = dec * c_ref[h] + _dot(kwt, v)
        ncm_ref[h] = dec * ncm_ref[h] + _dot(kwt, ones)
        m_ref[h] = m_new_row[:, h:h + 1]
        wgt = jnp.exp(jnp.where(causal, a_all[:, h:h + 1] + r_row, -jnp.inf))
        qk = sqk * wgt
        num = s_inter * qc + _dot(qk.astype(BF16), v)
        den = s_inter * qn + jnp.sum(qk, axis=-1, keepdims=True)
        if h + 1 < ML_HEADS:
            nxt = state_dots(h + 1)
        dmax = jnp.maximum(jnp.abs(den), dmin_all[:, h:h + 1])
        xg = sz_ref[:, hs].astype(F32) * num
        mu = jnp.mean(xg, axis=-1, keepdims=True)
        var = jnp.maximum(jnp.mean(xg * xg, axis=-1, keepdims=True) - mu * mu, 0.0)
        y = (xg - mu) * lax.rsqrt(var + EPS * (dmax * dmax)) * gn_ref[:, hs] + sx_ref[:, hs].astype(F32)
        out = out + _dot(y.astype(BF16), wout_ref[hs, :])
    xo_ref[...] = out


def _ml_b(x, q, kt, v, gates, sz, sx, state, w):
    b, L, d = x.shape
    chunk = min(CHUNK, L)
    tl = min(CELL_TILE, L)
    assert L % tl == 0 and tl % chunk == 0 and chunk % SUBLANES == 0, (L, tl, chunk)
    nl = L // tl
    inner = ML_HEADS * ML_DH
    has_state = state is not None
    kern = functools.partial(_ml_b_kernel, tl=tl, chunk=chunk, nl=nl, has_state=has_state)
    tok = lambda width: pl.BlockSpec((None, tl, width), lambda i, j: (i, j, 0))
    kt_spec = pl.BlockSpec((None, ML_HEADS, ML_DH, tl), lambda i, j: (i, 0, 0, j))
    c_spec = pl.BlockSpec((None, ML_HEADS, ML_DH, ML_DH), lambda i, j: (i, 0, 0, 0))
    n_spec = pl.BlockSpec((None, ML_HEADS, ML_DH), lambda i, j: (i, 0, 0))
    m_spec = pl.BlockSpec((None, ML_HEADS, 1, 1), lambda i, j: (i, 0, 0, 0))
    in_specs = [tok(d), tok(inner), kt_spec, tok(inner), tok(LANES), tok(inner), tok(inner)]
    args = [x, q, kt, v, gates, sz, sx]
    if has_state:
        in_specs += [c_spec, n_spec, m_spec]
        args += [state[0], state[1], state[2].reshape(b, ML_HEADS, 1, 1)]
    in_specs += [_full((1, inner)), _full((inner, d))]
    args += [w['ml_gn'], w['ml_w_out']]
    sds = jax.ShapeDtypeStruct
    out_shape = (sds(x.shape, F32), sds((b, ML_HEADS, ML_DH, ML_DH), F32), sds((b, ML_HEADS, ML_DH), F32),
                 sds((b, ML_HEADS, 1, 1), F32))
    xo, c, n, m = pl.pallas_call(
        kern, grid=(b, nl), in_specs=in_specs, out_specs=[tok(d), c_spec, n_spec, m_spec], out_shape=out_shape,
        scratch_shapes=[pltpu.VMEM((ML_HEADS, ML_DH, LANES), F32)],
        compiler_params=_params("parallel", "arbitrary"), name="mlstm_cell",
    )(*args)
    return xo, c, n, m.reshape(b, ML_HEADS)


def _memkv_kernel(m_ref, g_ref, wk_ref, wv_ref, k_ref, v_ref, kb_ref, vb_ref):
    nb, m_len = k_ref.shape[0], k_ref.shape[1]
    mn = _rms(m_ref[...], g_ref[...]).astype(BF16)
    k = _dot(mn, wk_ref[...])
    v = _dot(mn, wv_ref[...])
    for h in range(X_HEADS):
        kh = k[:, h * X_DH:(h + 1) * X_DH].reshape(nb, m_len, X_DH)
        vh = v[:, h * X_DH:(h + 1) * X_DH].reshape(nb, m_len, X_DH)
        k_ref[:, :, h, :] = kh
        v_ref[:, :, h, :] = vh
        kb_ref[:, h] = kh.astype(BF16)
        vb_ref[:, h] = vh.astype(BF16)


def _memkv(mem, norm_mem, wk, wv):
    b, m_len, d = mem.shape
    depth = wk.shape[0]
    rows = b * m_len
    tm = min(MEM_ROW_TILE, rows)
    assert rows % tm == 0 and tm % m_len == 0 and d == X_HEADS * X_DH, (rows, tm, m_len, d)
    nb = tm // m_len
    mem2 = mem.reshape(rows, d)
    wspec = pl.BlockSpec((None, d, d), lambda i, j: (i, 0, 0))
    ospec = pl.BlockSpec((None, nb, m_len, X_HEADS, X_DH), lambda i, j: (i, j, 0, 0, 0))
    oshape = jax.ShapeDtypeStruct((depth, b, m_len, X_HEADS, X_DH), F32)
    bspec = pl.BlockSpec((None, nb, X_HEADS, m_len, X_DH), lambda i, j: (i, j, 0, 0, 0))
    bshape = jax.ShapeDtypeStruct((depth, b, X_HEADS, m_len, X_DH), BF16)
    return pl.pallas_call(
        _memkv_kernel, grid=(depth, rows // tm),
        in_specs=[pl.BlockSpec((tm, d), lambda i, j: (j, 0)), pl.BlockSpec((None, 1, d), lambda i, j: (i, 0, 0)),
                  wspec, wspec],
        out_specs=[ospec, ospec, bspec, bspec],
        out_shape=(oshape, oshape, bshape, bshape),
        compiler_params=_params("parallel", "parallel"), name="memory_kv",
    )(mem2, norm_mem.reshape(depth, 1, d), wk, wv)


def _xattn_kernel(x_ref, g_ref, wq_ref, mk_ref, mv_ref, wo_ref, o_ref):
    x = x_ref[...]
    xn = _rms(x, g_ref[...]).astype(BF16)
    q = _dot(xn, wq_ref[...])
    hsl = [slice(h * X_DH, (h + 1) * X_DH) for h in range(X_HEADS)]

    def scores(h):
        return _dot_nt(q[:, hsl[h]].astype(BF16), mk_ref[h])

    out = x
    nxt = scores(0)
    for h in range(X_HEADS):
        s = nxt * (X_DH ** -0.5)
        if h + 1 < X_HEADS:
            nxt = scores(h + 1)
        e = jnp.exp(s - jnp.max(s, axis=-1, keepdims=True))
        pv = _dot(e.astype(BF16), mv_ref[h])
        oh = pv * (1.0 / jnp.sum(e, axis=-1, keepdims=True))
        out = out + _dot(oh.astype(BF16), wo_ref[hsl[h], :])
    o_ref[...] = out


def _xattn(x, layer, g, wq, mk, mv, wo):
    b, L, d = x.shape
    tl = min(XATTN_TILE, L)
    assert L % tl == 0 and d == X_HEADS * X_DH, (L, tl, d)
    m_len = mk.shape[3]
    tok = pl.BlockSpec((None, tl, d), lambda i, j: (i, j, 0))
    mem = pl.BlockSpec((None, None, X_HEADS, m_len, X_DH), lambda i, j: (layer, i, 0, 0, 0))
    wsp = pl.BlockSpec((None, d, d), lambda i, j: (layer, 0, 0))
    gsp = pl.BlockSpec((None, 1, d), lambda i, j: (layer, 0, 0))
    return pl.pallas_call(
        _xattn_kernel, grid=(b, L // tl),
        in_specs=[tok, gsp, wsp, mem, mem, wsp],
        out_specs=tok, out_shape=jax.ShapeDtypeStruct(x.shape, F32),
        compiler_params=_params("parallel", "arbitrary"), name="cross_attn",
    )(x, g, wq, mk, mv, wo)


def _mlp_kernel(*refs, final):
    if final:
        x_ref, g_ref, wu_ref, wd_ref, gf_ref, o_ref, xn_ref = refs
    else:
        x_ref, g_ref, wu_ref, wd_ref, o_ref, xn_ref = refs
    j = pl.program_id(1)

    @pl.when(j == 0)
    def _():
        x = x_ref[...]
        xn_ref[...] = _rms(x, g_ref[...]).astype(BF16)
        o_ref[...] = x

    hdn = jnp.maximum(_dot(xn_ref[...], wu_ref[...]), 0.0)
    o_ref[...] += _dot((hdn * hdn).astype(BF16), wd_ref[...])

    if final:
        @pl.when(j == pl.num_programs(1) - 1)
        def _():
            o_ref[...] = _rms(o_ref[...], gf_ref[...])


def _mlp(x, layer, g, wu, wd, gf=None):
    shape = x.shape
    d = shape[-1]
    x2 = x.reshape(-1, d)
    rows = x2.shape[0]
    ff = wu.shape[2]
    tm = min(MLP_ROW_TILE, rows)
    tf = min(MLP_FF_TILE, ff)
    assert rows % tm == 0 and ff % tf == 0, (rows, tm, ff, tf)
    final = gf is not None
    wmode = pl.Buffered(1) if tf == ff else None
    in_specs = [pl.BlockSpec((tm, d), lambda i, j: (i, 0)), pl.BlockSpec((None, 1, d), lambda i, j: (layer, 0, 0)),
                pl.BlockSpec((None, d, tf), lambda i, j: (layer, 0, j), pipeline_mode=wmode),
                pl.BlockSpec((None, tf, d), lambda i, j: (layer, j, 0), pipeline_mode=wmode)]
    args = [x2, g, wu, wd]
    if final:
        in_specs.append(_full((1, d)))
        args.append(gf)
    out = pl.pallas_call(
        functools.partial(_mlp_kernel, final=final), grid=(rows // tm, ff // tf),
        in_specs=in_specs, out_specs=pl.BlockSpec((tm, d), lambda i, j: (i, 0)),
        out_shape=jax.ShapeDtypeStruct(x2.shape, F32),
        scratch_shapes=[pltpu.VMEM((tm, d), BF16)],
        compiler_params=_params("parallel", "arbitrary"), name="mlp",
    )(*args)
    return out.reshape(shape)


def _trunk(x, pos0, st_ret, st_re, st_im, ml_state, conv_state, mem_k, mem_v, w):
    b, L, d = x.shape
    chunk = min(RET_CHUNK, L)
    nst = S5_GROUPS * S5_STATE // (SUBLANES * LANES)
    st_h = jnp.concatenate([st_re.reshape(b, nst, SUBLANES, LANES), st_im.reshape(b, nst, SUBLANES, LANES)], axis=1)
    x, s_new, h_new = _rs_layer(x, st_ret, st_h, _ret_consts(L, chunk, pos0), w)
    hr = h_new[:, :nst].reshape(b, S5_GROUPS, S5_STATE)
    hi = h_new[:, nst:].reshape(b, S5_GROUPS, S5_STATE)
    x = _xattn(x, 0, w['norm_cross'], w['x_wq'], mem_k, mem_v, w['x_wo'])
    x = _mlp(x, 0, w['norm_mlp'], w['mlp_w_up'], w['mlp_w_down'])
    q, kt, v, gates, sz, sx, new_conv = _ml_a(x, conv_state, w)
    x, cf, nf, mf = _ml_b(x, q, kt, v, gates, sz, sx, ml_state, w)
    x = _xattn(x, 1, w['norm_cross'], w['x_wq'], mem_k, mem_v, w['x_wo'])
    y = _mlp(x, 1, w['norm_mlp'], w['mlp_w_up'], w['mlp_w_down'], w['norm_final'])
    return y, s_new[None], hr[None], hi[None], cf[None], nf[None], mf[None], new_conv[None]


def kernel(x_prompt, x_sample, mem_prompt, state_ret, state_s5_re, state_s5_im, state_mlstm_c, state_mlstm_n, state_mlstm_m, cache_mlstm_conv, cache_mem_k, cache_mem_v, norm_mix, norm_cross, norm_mem, norm_mlp, norm_final, rs_w_in, rs_w_out, ret_gn, s5_a_re, s5_a_im, s5_log_dt, s5_b_re, s5_b_im, s5_c_re, s5_c_im, s5_d, s5_w_glu, s5_b_glu, ml_w_in, ml_conv_w, ml_conv_b, ml_wq, ml_wk, ml_wv, ml_w_gates, ml_b_gates, ml_gn, ml_skip, ml_w_out, x_wq, x_wk, x_wv, x_wo, mlp_w_up, mlp_w_down):
    d = x_prompt.shape[-1]
    bp = x_prompt.shape[0]
    bf = lambda t: t.astype(BF16)
    row = lambda t: t.reshape(1, -1).astype(F32)
    ab_re, ab_im, bb_re, bb_im = _s5_prep(s5_a_re[0], s5_a_im[0], s5_log_dt[0], s5_b_re[0], s5_b_im[0])
    s5_a, s5_b, s5_c = _s5_matrices(ab_re, ab_im, bb_re, bb_im, s5_c_re[0], s5_c_im[0])
    n_gate = ml_w_gates.shape[-1]
    w = {
        'norm_mix0': row(norm_mix[0]), 'norm_mix1': row(norm_mix[1]),
        'norm_cross': norm_cross.reshape(-1, 1, d), 'norm_mlp': norm_mlp.reshape(-1, 1, d),
        'norm_final': row(norm_final),
        'rs_w_in': bf(rs_w_in[0]), 'rs_w_out': bf(rs_w_out[0]), 'ret_gn': row(ret_gn[0]),
        's5_a': s5_a, 's5_b': s5_b, 's5_c': s5_c, 's5_d': row(s5_d[0]), 's5_w_glu': bf(s5_w_glu[0]),
        's5_b_glu': row(s5_b_glu[0]),
        'ml_w_in': bf(ml_w_in[0]), 'ml_conv_w': ml_conv_w[0], 'ml_conv_b': row(ml_conv_b[0]),
        'ml_wq': bf(ml_wq[0]), 'ml_wkt': bf(jnp.swapaxes(ml_wk[0], 1, 2)), 'ml_wv': bf(ml_wv[0]),
        'ml_w_gates': bf(_gate_fold(ml_wq[0], ml_wk[0], ml_wv[0],
                                    jnp.pad(ml_w_gates[0], ((0, 0), (0, LANES - n_gate))))),
        'ml_b_gates': jnp.pad(ml_b_gates[0], (0, LANES - n_gate)).reshape(1, LANES),
        'ml_gn': row(ml_gn[0]), 'ml_skip': row(ml_skip[0]), 'ml_w_out': bf(ml_w_out[0]),
        'x_wq': bf(x_wq), 'x_wo': bf(x_wo), 'mlp_w_up': bf(mlp_w_up), 'mlp_w_down': bf(mlp_w_down),
    }
    mk_p, mv_p, mkb_p, mvb_p = _memkv(mem_prompt, norm_mem, bf(x_wk), bf(x_wv))
    zeros = lambda *s: jnp.zeros(s, F32)
    out_p = _trunk(x_prompt, 0, zeros(bp, RET_HEADS, RET_DK, RET_DK), zeros(bp, S5_GROUPS, S5_STATE),
                   zeros(bp, S5_GROUPS, S5_STATE), None, zeros(bp, ML_CONV - 1, ML_HEADS * ML_DH), mkb_p, mvb_p, w)
    out_s = _trunk(x_sample, PAST_LEN, state_ret[0], state_s5_re[0], state_s5_im[0],
                   (state_mlstm_c[0], state_mlstm_n[0], state_mlstm_m[0]), cache_mlstm_conv[0],
                   bf(jnp.swapaxes(cache_mem_k, 2, 3)), bf(jnp.swapaxes(cache_mem_v, 2, 3)), w)
    return (out_p[0], out_s[0]) + tuple(out_p[1:]) + (mk_p, mv_p) + tuple(out_s[1:])
```
